```python
import jax, jax.numpy as jnp
from jax import lax
import numpy as np

D_MODEL = 2048
BATCH = 2
SEQ = 4096
DEPTH = 4
DEC_BATCH = 8
DEC_SEQ = 1
PAST_LEN = 16384
PAGE_SIZE = 128

HEAD_DIM = 128
N_EVEN = (DEPTH + 1) // 2
N_ODD = DEPTH // 2
A_DK = 128
A_DV = 128
A_HEADS = D_MODEL // 2 // A_DK
A_KW = A_HEADS * A_DK
A_VW = A_HEADS * A_DV
HGRN_CHUNK = 64
B_HEADS = D_MODEL // 2 // HEAD_DIM
B_WIDTH = B_HEADS * HEAD_DIM
DIL_PATTERNS = ((128, 1), (512, 4), (2048, 16))
DIL_MAX_WINDOW = 2048
BAND_BLOCK = 128
C_HEADS = D_MODEL // HEAD_DIM
C_KV_HEADS = 4
C_GROUP = C_HEADS // C_KV_HEADS
C_WIDTH = C_HEADS * HEAD_DIM
C_KV_WIDTH = C_KV_HEADS * HEAD_DIM
CMP_LEN = 32
SLC_BLOCK = 64
SLC_TOPN = 16
SLC_Q_BLOCK = 64
WIN = 512
D_FF = 4 * D_MODEL
EVEN_SPLIT = (A_KW, A_KW, A_VW, A_VW, B_WIDTH, B_WIDTH, B_WIDTH)
ODD_SPLIT = (C_WIDTH,) + (C_KV_WIDTH,) * 6 + (3 * C_HEADS,)
EVEN_IN = sum(EVEN_SPLIT)
EVEN_OUT = A_VW + B_WIDTH
ODD_IN = sum(ODD_SPLIT)
RMS_EPS = 1e-6
NEG_INF = -1e30
FORCE_SCORE = 1e6
SCALE = HEAD_DIM ** -0.5
F32 = jnp.float32

kernel_name = "hybrid_hgrn2_dilated_nsa_decode_step"


def rmsnorm(x, g):
    x32 = x.astype(F32)
    y = x32 * lax.rsqrt(jnp.mean(x32 * x32, axis=-1, keepdims=True) + RMS_EPS)
    return (y * g.astype(F32)).astype(x.dtype)


def split_cols(a, sizes):
    offs = np.cumsum(sizes)[:-1].tolist()
    return jnp.split(a, offs, axis=-1)


def sq_relu_mlp(h, w_up, w_down):
    u = jax.nn.relu(h @ w_up)
    return (u * u) @ w_down


def hgrn2_gates(z_f, lb):
    zf = z_f.astype(F32)
    lb = lb.astype(F32)
    log_f = jnp.logaddexp(jnp.log(lb), jnp.log1p(-lb) + jax.nn.log_sigmoid(zf))
    return log_f, -jnp.expm1(log_f)


def hgrn2_recurrence(q, k, v, log_f, s0):
    N, T, H, DK = q.shape
    DV = v.shape[-1]
    C = HGRN_CHUNK if T % HGRN_CHUNK == 0 else T
    nc = T // C

    def chunks(a):
        return a.astype(F32).reshape(N, nc, C, H, a.shape[-1]).transpose(1, 0, 3, 2, 4)

    causal = jnp.tril(jnp.ones((C, C), bool))[:, :, None]

    def step(S, inp):
        qc, kc, vc, gc = inp
        G = jnp.cumsum(gc, axis=2)
        inter = jnp.einsum('nhik,nhkv->nhiv', qc * jnp.exp(G), S)
        diff = jnp.where(causal, G[:, :, :, None, :] - G[:, :, None, :, :], NEG_INF)
        a = jnp.einsum('nhik,nhijk,nhjk->nhij', qc, jnp.exp(diff), kc)
        o = inter + jnp.einsum('nhij,nhjv->nhiv', a, vc)
        G_last = G[:, :, -1:, :]
        S = jnp.exp(G_last[:, :, 0, :, None]) * S + jnp.einsum('nhjk,nhjv->nhkv', kc * jnp.exp(G_last - G), vc)
        return S, o

    S, o = lax.scan(step, s0.astype(F32), (chunks(q), chunks(k), chunks(v), chunks(log_f)))
    return o.transpose(1, 0, 3, 2, 4).reshape(N, T, H, DV), S


def banded_stats(q, k, v, max_dist, block):
    N, L, Hk, G, Dh = q.shape
    n_prev = -(-max_dist // block)
    nb = -(-L // block)
    pad = nb * block - L
    span = (n_prev + 1) * block
    qb = jnp.pad(q, ((0, 0), (0, pad), (0, 0), (0, 0), (0, 0))).reshape(N, nb, block, Hk, G, Dh)
    kv_pad = ((0, 0), (n_prev * block, pad), (0, 0), (0, 0))
    idx = jnp.arange(nb)[:, None] * block + jnp.arange(span)[None, :]
    kb = jnp.pad(k, kv_pad)[:, idx]
    vb = jnp.pad(v, kv_pad)[:, idx]
    qpos = jnp.arange(nb)[:, None] * block + jnp.arange(block)[None, :]
    kpos = idx - n_prev * block
    dist = qpos[:, :, None] - kpos[:, None, :]
    mask = (dist >= 0) & (dist <= max_dist) & (kpos[:, None, :] >= 0)
    s = jnp.einsum('nbqhgd,nbshd->nbhgqs', qb, kb).astype(F32) * SCALE
    s = jnp.where(mask[None, :, None, None], s, NEG_INF)
    m = s.max(-1)
    p = jnp.exp(s - m[..., None])
    l = p.sum(-1)
    o = jnp.einsum('nbhgqs,nbshd->nbqhgd', p, vb)
    m = m.transpose(0, 1, 4, 2, 3).reshape(N, nb * block, Hk, G)[:, :L]
    l = l.transpose(0, 1, 4, 2, 3).reshape(N, nb * block, Hk, G)[:, :L]
    o = o.reshape(N, nb * block, Hk, G, Dh)[:, :L]
    return m, l, o


def combine_dilated(stats):
    ms = jnp.stack([m for m, _, _ in stats])
    ls = jnp.stack([l for _, l, _ in stats])
    os_ = jnp.stack([o for _, _, o in stats])
    w = jnp.exp(ms - ms.max(0))
    return (w[..., None] * os_).sum(0) / (w * ls).sum(0)[..., None]


def dilated_prompt(q, k, v):
    N, T, H, Dh = q.shape
    stats = []
    for window, dil in DIL_PATTERNS:
        def split(a):
            return a.reshape(N, T // dil, dil, *a.shape[2:]).swapaxes(1, 2).reshape(N * dil, T // dil, *a.shape[2:])

        def merge(a):
            return a.reshape(N, dil, T // dil, *a.shape[2:]).swapaxes(1, 2).reshape(N, T, *a.shape[2:])

        m, l, o = banded_stats(split(q)[:, :, :, None], split(k), split(v), window // dil, BAND_BLOCK)
        stats.append((merge(m[..., 0]), merge(l[..., 0]), merge(o[..., 0, :])))
    return combine_dilated(stats)


def dilated_sample(q, k_new, v_new, buf_k, buf_v):
    S = q.shape[1]
    Wb = buf_k.shape[1]
    kc = jnp.concatenate([buf_k, k_new], axis=1)
    vc = jnp.concatenate([buf_v, v_new], axis=1)
    stats = []
    for window, dil in DIL_PATTERNS:
        idx = Wb + jnp.arange(S)[:, None] - (jnp.arange(window // dil + 1) * dil)[None, :]
        valid = idx >= 0
        idx = jnp.maximum(idx, 0)
        kg = kc[:, idx]
        vg = vc[:, idx]
        s = jnp.einsum('nshd,nskhd->nshk', q, kg).astype(F32) * SCALE
        s = jnp.where(valid[None, :, None, :], s, NEG_INF)
        m = s.max(-1)
        p = jnp.exp(s - m[..., None])
        stats.append((m, p.sum(-1), jnp.einsum('nshk,nskhd->nshd', p, vg)))
    return combine_dilated(stats), kc[:, S:], vc[:, S:]


def even_mixer(h, w_in, w_out, lb, out_norm, q_norm, k_norm, s0, buf_k, buf_v):
    N, T, _ = h.shape
    qa, fa, ia, ga, qb, kb, vb = split_cols(h @ w_in, EVEN_SPLIT)
    qa = jax.nn.silu(qa.reshape(N, T, A_HEADS, A_DK))
    log_f, ka = hgrn2_gates(fa.reshape(N, T, A_HEADS, A_DK), lb.reshape(A_HEADS, A_DK))
    va = ia.reshape(N, T, A_HEADS, A_DV)
    if s0 is None:
        s0 = jnp.zeros((N, A_HEADS, A_DK, A_DV), F32)
    oa, s_new = hgrn2_recurrence(qa, ka, va, log_f, s0)
    oa = rmsnorm(oa, out_norm) * jax.nn.silu(ga.reshape(N, T, A_HEADS, A_DV).astype(F32))
    qb = rmsnorm(qb.reshape(N, T, B_HEADS, HEAD_DIM), q_norm)
    kb = rmsnorm(kb.reshape(N, T, B_HEADS, HEAD_DIM), k_norm)
    vb = vb.reshape(N, T, B_HEADS, HEAD_DIM)
    if buf_k is None:
        ob = dilated_prompt(qb, kb, vb)
        w = min(DIL_MAX_WINDOW, T)
        new_k, new_v = kb[:, T - w:], vb[:, T - w:]
    else:
        ob, new_k, new_v = dilated_sample(qb, kb, vb, buf_k, buf_v)
    mixed = jnp.concatenate([oa.reshape(N, T, A_VW), ob.reshape(N, T, B_WIDTH)], axis=-1).astype(h.dtype)
    return mixed @ w_out, s_new.astype(h.dtype), new_k, new_v


def compress(rows, pe, w1, w2):
    N, L, Hk, Dh = rows.shape
    nblk = -(-L // CMP_LEN)
    rows = jnp.pad(rows, ((0, 0), (0, nblk * CMP_LEN - L), (0, 0), (0, 0)))
    blk = rows.reshape(N, nblk, CMP_LEN, Hk, Dh) + pe[None, None, :, None, :]
    flat = blk.transpose(0, 1, 3, 2, 4).reshape(N, nblk, Hk, CMP_LEN * Dh)
    return jax.nn.gelu(flat @ w1) @ w2


def nsa_compressed(q, kc_rows, vc_rows, qpos, pe_k, pe_v, phi_k1, phi_k2, phi_v1, phi_v2, kn_cmp):
    kcmp = rmsnorm(compress(kc_rows, pe_k, phi_k1, phi_k2), kn_cmp)
    vcmp = compress(vc_rows, pe_v, phi_v1, phi_v2)
    nblk = kcmp.shape[1]
    valid = ((jnp.arange(nblk) + 1) * CMP_LEN - 1)[None, :] <= qpos[:, None]
    s = jnp.einsum('ntkgd,nbkd->nkgtb', q, kcmp).astype(F32) * SCALE
    s = jnp.where(valid, s, NEG_INF)
    m = s.max(-1, keepdims=True)
    p = jnp.where(valid, jnp.exp(s - m), 0.0)
    p = p / jnp.maximum(p.sum(-1, keepdims=True), 1.0)
    o = jnp.einsum('nkgtb,nbkd->ntkgd', p, vcmp)
    return o, p.sum(2)


def nsa_selected(q, ks_rows, vs_rows, qpos, p_blk):
    N, L, Hk, Dh = ks_rows.shape
    Tq, G = q.shape[1], q.shape[3]
    n_slc = -(-L // SLC_BLOCK)
    ratio = SLC_BLOCK // CMP_LEN
    p_pad = jnp.pad(p_blk, ((0, 0), (0, 0), (0, 0), (0, n_slc * ratio - p_blk.shape[-1])))
    imp = p_pad.reshape(N, Hk, Tq, n_slc, ratio).sum(-1)
    j = jnp.arange(n_slc)[None, :]
    cur = (qpos // SLC_BLOCK)[:, None]
    valid = j * SLC_BLOCK <= qpos[:, None]
    forced = (j == 0) | (j == cur) | (j == cur - 1)
    score = jnp.where(valid, jnp.where(forced, FORCE_SCORE, imp), -FORCE_SCORE)
    n_top = min(SLC_TOPN, n_slc)
    _, sel = lax.top_k(score, n_top)
    pad = n_slc * SLC_BLOCK - L
    kblk = jnp.pad(ks_rows, ((0, 0), (0, pad), (0, 0), (0, 0))).reshape(N, n_slc, SLC_BLOCK, Hk, Dh).transpose(0, 3, 1, 2, 4)
    vblk = jnp.pad(vs_rows, ((0, 0), (0, pad), (0, 0), (0, 0))).reshape(N, n_slc, SLC_BLOCK, Hk, Dh).transpose(0, 3, 1, 2, 4)
    tb = SLC_Q_BLOCK if Tq % SLC_Q_BLOCK == 0 else Tq
    nqb = Tq // tb
    q_blocks = q.reshape(N, nqb, tb, Hk, G, Dh).transpose(1, 0, 3, 2, 4, 5)
    sel_blocks = sel.reshape(N, Hk, nqb, tb, n_top).transpose(2, 0, 1, 3, 4)
    pos_blocks = qpos.reshape(nqb, tb)
    bi = jnp.arange(N)[:, None, None, None]
    hi = jnp.arange(Hk)[None, :, None, None]

    def attend(args):
        qb, sb, pb = args
        kg = kblk[bi, hi, sb].reshape(N, Hk, tb, n_top * SLC_BLOCK, Dh)
        vg = vblk[bi, hi, sb].reshape(N, Hk, tb, n_top * SLC_BLOCK, Dh)
        kpos = (sb[..., None] * SLC_BLOCK + jnp.arange(SLC_BLOCK)).reshape(N, Hk, tb, n_top * SLC_BLOCK)
        s = jnp.einsum('nhtgd,nhtsd->nhtgs', qb, kg).astype(F32) * SCALE
        s = jnp.where((kpos <= pb[None, None, :, None])[:, :, :, None, :], s, NEG_INF)
        p = jax.nn.softmax(s, axis=-1)
        return jnp.einsum('nhtgs,nhtsd->nhtgd', p, vg)

    o = lax.map(attend, (q_blocks, sel_blocks, pos_blocks))
    return o.transpose(1, 0, 3, 2, 4, 5).reshape(N, Tq, Hk, G, Dh)


def window_sample(q, k_new, v_new, buf_k, buf_v):
    S = q.shape[1]
    Wb = buf_k.shape[1]
    kc = jnp.concatenate([buf_k, k_new], axis=1)
    vc = jnp.concatenate([buf_v, v_new], axis=1)
    dist = (Wb + jnp.arange(S))[:, None] - jnp.arange(Wb + S)[None, :]
    mask = (dist >= 0) & (dist <= WIN)
    s = jnp.einsum('nshgd,nkhd->nhgsk', q, kc).astype(F32) * SCALE
    p = jax.nn.softmax(jnp.where(mask, s, NEG_INF), axis=-1)
    return jnp.einsum('nhgsk,nkhd->nshgd', p, vc), kc[:, S:], vc[:, S:]


def odd_mixer(h, w_in, w_out, q_norm, k_norm, pe_k, pe_v, phi_k1, phi_k2, phi_v1, phi_v2, past, win_k, win_v):
    N, T, _ = h.shape
    q, kc, vc, ks, vs, kw, vw, gate = split_cols(h @ w_in, ODD_SPLIT)
    q = rmsnorm(q.reshape(N, T, C_HEADS, HEAD_DIM), q_norm).reshape(N, T, C_KV_HEADS, C_GROUP, HEAD_DIM)
    kv = lambda a: a.reshape(N, T, C_KV_HEADS, HEAD_DIM)
    kc, vc, vs, vw = kv(kc), kv(vc), kv(vs), kv(vw)
    ks = rmsnorm(kv(ks), k_norm[1])
    kw = rmsnorm(kv(kw), k_norm[2])
    gate = jax.nn.sigmoid(gate.astype(F32)).reshape(N, T, C_KV_HEADS, C_GROUP, 3)
    if past is None:
        p0 = 0
        ctx = (kc, vc, ks, vs)
    else:
        p0 = past[0].shape[1]
        ctx = tuple(jnp.concatenate([pr, nr], axis=1) for pr, nr in zip(past, (kc, vc, ks, vs)))
    qpos = p0 + jnp.arange(T)
    o_cmp, p_blk = nsa_compressed(q, ctx[0], ctx[1], qpos, pe_k, pe_v, phi_k1, phi_k2, phi_v1, phi_v2, k_norm[0])
    o_slc = nsa_selected(q, ctx[2], ctx[3], qpos, p_blk)
    if win_k is None:
        _, l, o = banded_stats(q, kw, vw, WIN, BAND_BLOCK)
        o_win = o / l[..., None]
        w = min(WIN, T)
        new_wk, new_wv = kw[:, T - w:], vw[:, T - w:]
    else:
        o_win, new_wk, new_wv = window_sample(q, kw, vw, win_k, win_v)
    o = gate[..., 0:1] * o_cmp + gate[..., 1:2] * o_slc + gate[..., 2:3] * o_win
    y = o.astype(h.dtype).reshape(N, T, C_WIDTH) @ w_out
    return y, kc, vc, ks, vs, new_wk, new_wv


def setup_inputs(seed: int = 0) -> dict:
    key = jax.random.key(seed)
    keys = iter(jax.random.split(key, 48))

    def nrm(shape, scale=1.0):
        return jax.random.normal(next(keys), shape, F32) * scale

    def gain(shape):
        return 1.0 + 0.02 * jax.random.normal(next(keys), shape, F32)

    n_pages = PAST_LEN // PAGE_SIZE
    used = DEC_BATCH * n_pages
    n_pool = used + max(1, used // 4)
    w_dil = min(DIL_MAX_WINDOW, PAST_LEN)
    w_win = min(WIN, PAST_LEN)
    paged = (n_pool, N_ODD, PAGE_SIZE, C_KV_HEADS, HEAD_DIM)
    x_prompt = nrm((BATCH, SEQ, D_MODEL))
    x_sample = nrm((DEC_BATCH, DEC_SEQ, D_MODEL))
    state_hgrn = nrm((DEC_BATCH, N_EVEN, A_HEADS, A_DK, A_DV), 0.5)
    cache_dil_k = nrm((DEC_BATCH, N_EVEN, w_dil, B_HEADS, HEAD_DIM))
    cache_dil_v = nrm((DEC_BATCH, N_EVEN, w_dil, B_HEADS, HEAD_DIM))
    cache_cmp_k = nrm(paged)
    cache_cmp_v = nrm(paged)
    cache_slc_k = nrm(paged)
    cache_slc_v = nrm(paged)
    cache_win_k = nrm((DEC_BATCH, N_ODD, w_win, C_KV_HEADS, HEAD_DIM))
    cache_win_v = nrm((DEC_BATCH, N_ODD, w_win, C_KV_HEADS, HEAD_DIM))
    page_table = jax.random.permutation(next(keys), n_pool)[:used].reshape(DEC_BATCH, n_pages).astype(jnp.int32)
    return {
        "x_prompt": x_prompt,
        "x_sample": x_sample,
        "state_hgrn": state_hgrn,
        "cache_dil_k": cache_dil_k,
        "cache_dil_v": cache_dil_v,
        "cache_cmp_k": cache_cmp_k,
        "cache_cmp_v": cache_cmp_v,
        "cache_slc_k": cache_slc_k,
        "cache_slc_v": cache_slc_v,
        "cache_win_k": cache_win_k,
        "cache_win_v": cache_win_v,
        "page_table": page_table,
        "norm_mix": gain((DEPTH, D_MODEL)),
        "norm_mlp": gain((DEPTH, D_MODEL)),
        "w_in_even": nrm((N_EVEN, D_MODEL, EVEN_IN), D_MODEL ** -0.5),
        "w_out_even": nrm((N_EVEN, EVEN_OUT, D_MODEL), EVEN_OUT ** -0.5),
        "hgrn_lb_logits": nrm((N_EVEN, A_KW), 0.5),
        "hgrn_out_norm": gain((N_EVEN, A_DV)),
        "dil_q_norm": gain((N_EVEN, HEAD_DIM)),
        "dil_k_norm": gain((N_EVEN, HEAD_DIM)),
        "w_in_odd": nrm((N_ODD, D_MODEL, ODD_IN), D_MODEL ** -0.5),
        "w_out_odd": nrm((N_ODD, C_WIDTH, D_MODEL), C_WIDTH ** -0.5),
        "nsa_q_norm": gain((N_ODD, HEAD_DIM)),
        "nsa_k_norm": gain((N_ODD, 3, HEAD_DIM)),
        "nsa_pe_k": nrm((N_ODD, CMP_LEN, HEAD_DIM), 0.1),
        "nsa_pe_v": nrm((N_ODD, CMP_LEN, HEAD_DIM), 0.1),
        "nsa_phi_k1": nrm((N_ODD, CMP_LEN * HEAD_DIM, HEAD_DIM), (CMP_LEN * HEAD_DIM) ** -0.5),
        "nsa_phi_k2": nrm((N_ODD, HEAD_DIM, HEAD_DIM), HEAD_DIM ** -0.5),
        "nsa_phi_v1": nrm((N_ODD, CMP_LEN * HEAD_DIM, HEAD_DIM), (CMP_LEN * HEAD_DIM) ** -0.5),
        "nsa_phi_v2": nrm((N_ODD, HEAD_DIM, HEAD_DIM), HEAD_DIM ** -0.5),
        "w_mlp_up": nrm((DEPTH, D_MODEL, D_FF), D_MODEL ** -0.5),
        "w_mlp_down": nrm((DEPTH, D_FF, D_MODEL), D_FF ** -0.5),
    }


def reference(x_prompt, x_sample, state_hgrn, cache_dil_k, cache_dil_v, cache_cmp_k, cache_cmp_v,
              cache_slc_k, cache_slc_v, cache_win_k, cache_win_v, page_table, norm_mix, norm_mlp,
              w_in_even, w_out_even, hgrn_lb_logits, hgrn_out_norm, dil_q_norm, dil_k_norm,
              w_in_odd, w_out_odd, nsa_q_norm, nsa_k_norm, nsa_pe_k, nsa_pe_v, nsa_phi_k1, nsa_phi_k2,
              nsa_phi_v1, nsa_phi_v2, w_mlp_up, w_mlp_down):
    lb_cum = jnp.cumsum(jax.nn.softmax(hgrn_lb_logits.astype(F32), axis=0), axis=0)
    lower_bounds = lb_cum - lb_cum[0:1]
    dec_b = x_sample.shape[0]

    def gather_pages(pool, li):
        rows = pool[page_table, li]
        return rows.reshape(dec_b, -1, *rows.shape[3:])

    xp, xs = x_prompt, x_sample
    hg_p, hg_s, dk_p, dv_p, dk_s, dv_s = [], [], [], [], [], []
    ck_p, cv_p, sk_p, sv_p, wk_p, wv_p = [], [], [], [], [], []
    ck_s, cv_s, sk_s, sv_s, wk_s, wv_s = [], [], [], [], [], []
    for layer in range(DEPTH):
        hp = rmsnorm(xp, norm_mix[layer])
        hs = rmsnorm(xs, norm_mix[layer])
        li = layer // 2
        if layer % 2 == 0:
            wts = (w_in_even[li], w_out_even[li], lower_bounds[li], hgrn_out_norm[li], dil_q_norm[li], dil_k_norm[li])
            yp, s_p, k_p, v_p = even_mixer(hp, *wts, None, None, None)
            ys, s_s, k_s, v_s = even_mixer(hs, *wts, state_hgrn[:, li], cache_dil_k[:, li], cache_dil_v[:, li])
            hg_p.append(s_p); dk_p.append(k_p); dv_p.append(v_p)
            hg_s.append(s_s); dk_s.append(k_s); dv_s.append(v_s)
        else:
            wts = (w_in_odd[li], w_out_odd[li], nsa_q_norm[li], nsa_k_norm[li], nsa_pe_k[li], nsa_pe_v[li],
                   nsa_phi_k1[li], nsa_phi_k2[li], nsa_phi_v1[li], nsa_phi_v2[li])
            yp, a, b, c, d, e, f = odd_mixer(hp, *wts, None, None, None)
            ck_p.append(a); cv_p.append(b); sk_p.append(c); sv_p.append(d); wk_p.append(e); wv_p.append(f)
            past = (gather_pages(cache_cmp_k, li), gather_pages(cache_cmp_v, li),
                    gather_pages(cache_slc_k, li), gather_pages(cache_slc_v, li))
            ys, a, b, c, d, e, f = odd_mixer(hs, *wts, past, cache_win_k[:, li], cache_win_v[:, li])
            ck_s.append(a); cv_s.append(b); sk_s.append(c); sv_s.append(d); wk_s.append(e); wv_s.append(f)
        xp = xp + yp
        xs = xs + ys
        xp = xp + sq_relu_mlp(rmsnorm(xp, norm_mlp[layer]), w_mlp_up[layer], w_mlp_down[layer])
        xs = xs + sq_relu_mlp(rmsnorm(xs, norm_mlp[layer]), w_mlp_up[layer], w_mlp_down[layer])
    return (xp, xs,
            jnp.stack(hg_p, axis=1), jnp.stack(hg_s, axis=1),
            jnp.stack(dk_p, axis=1), jnp.stack(dv_p, axis=1), jnp.stack(dk_s, axis=1), jnp.stack(dv_s, axis=1),
            jnp.stack(ck_p, axis=1), jnp.stack(cv_p, axis=1), jnp.stack(sk_p, axis=1), jnp.stack(sv_p, axis=1),
            jnp.stack(wk_p, axis=1), jnp.stack(wv_p, axis=1),
            jnp.stack(ck_s, axis=1), jnp.stack(cv_s, axis=1), jnp.stack(sk_s, axis=1), jnp.stack(sv_s, axis=1),
            jnp.stack(wk_s, axis=1), jnp.stack(wv_s, axis=1))
```

```python
import functools

import numpy as np
import jax
import jax.numpy as jnp
from jax import lax
from jax.experimental import pallas as pl
from jax.experimental.pallas import tpu as pltpu

F32 = jnp.float32
BF16 = jnp.bfloat16
HIGHEST = lax.Precision.HIGHEST

HEAD_DIM = 128
LANES = 128
RMS_EPS = 1e-6
NEG_INF = -1e30
FORCE_SCORE = 1e6
SCALE = HEAD_DIM ** -0.5
HGRN_CHUNK = 64
HGRN_SUB = 16
DIL_PATTERNS = ((128, 1), (512, 4), (2048, 16))
DIL_MAX_WINDOW = 2048
CMP_LEN = 32
SLC_BLOCK = 64
SLC_TOPN = 16
WIN = 512
C_GROUP = 4
VMEM_LIMIT = 56 * 1024 * 1024


def _params(*sem):
    return pltpu.CompilerParams(dimension_semantics=sem, vmem_limit_bytes=VMEM_LIMIT)


def _rms(x, w):
    return x * lax.rsqrt(jnp.mean(x * x, axis=-1, keepdims=True) + RMS_EPS) * w


def _sigmoid(x):
    return 1.0 / (1.0 + jnp.exp(-x))


def _dot_nt(a, b):
    return lax.dot_general(a, b, (((1,), (1,)), ((), ())), preferred_element_type=F32)


def _dot(a, b, precision=None):
    return jnp.dot(a, b, preferred_element_type=F32, precision=precision)


def _rms_mm_body(x_ref, g_ref, w_ref, o_ref, h_ref):
    @pl.when(pl.program_id(1) == 0)
    def _():
        h_ref[...] = _rms(x_ref[...], g_ref[...]).astype(BF16)

    o_ref[...] = _dot(h_ref[...], w_ref[...])


def rms_matmul(x, g, w, tm, tn):
    M, D = x.shape
    N = w.shape[1]
    return pl.pallas_call(
        _rms_mm_body,
        grid=(M // tm, N // tn),
        in_specs=[pl.BlockSpec((tm, D), lambda i, j: (i, 0)),
                  pl.BlockSpec((1, D), lambda i, j: (0, 0)),
                  pl.BlockSpec((D, tn), lambda i, j: (0, j))],
        out_specs=pl.BlockSpec((tm, tn), lambda i, j: (i, j)),
        out_shape=jax.ShapeDtypeStruct((M, N), F32),
        scratch_shapes=[pltpu.VMEM((tm, D), BF16)],
        compiler_params=_params("parallel", "arbitrary"),
        name="rms_matmul",
    )(x, g.reshape(1, D), w)


def _proj_res_body(*refs, n_in):
    res_ref, o_ref = refs[2 * n_in], refs[2 * n_in + 1]
    acc = res_ref[...]
    for a_ref, w_ref in zip(refs[:n_in], refs[n_in:2 * n_in]):
        acc = acc + _dot(a_ref[...].astype(BF16), w_ref[...])
    o_ref[...] = acc


def proj_residual(lhs, ws, res, tm):
    M, D = res.shape
    n = len(lhs)
    in_specs = [pl.BlockSpec((tm, a.shape[1]), lambda i: (i, 0)) for a in lhs]
    in_specs += [pl.BlockSpec(w.shape, lambda i: (0, 0)) for w in ws]
    in_specs += [pl.BlockSpec((tm, D), lambda i: (i, 0))]
    return pl.pallas_call(
        functools.partial(_proj_res_body, n_in=n),
        grid=(M // tm,),
        in_specs=in_specs,
        out_specs=pl.BlockSpec((tm, D), lambda i: (i, 0)),
        out_shape=jax.ShapeDtypeStruct((M, D), F32),
        compiler_params=_params("parallel"),
        name="proj_residual",
    )(*lhs, *ws, res)


def _mlp_body(x_ref, g_ref, wu_ref, wd_ref, o_ref, h_ref):
    @pl.when(pl.program_id(1) == 0)
    def _():
        x = x_ref[...]
        h_ref[...] = _rms(x, g_ref[...]).astype(BF16)
        o_ref[...] = x

    u = jnp.maximum(_dot(h_ref[...], wu_ref[...]), 0.0)
    o_ref[...] += _dot((u * u).astype(BF16), wd_ref[...])


def mlp_residual(x, g, wu, wd, tm, tf):
    M, D = x.shape
    Fd = wu.shape[1]
    return pl.pallas_call(
        _mlp_body,
        grid=(M // tm, Fd // tf),
        in_specs=[pl.BlockSpec((tm, D), lambda i, j: (i, 0)),
                  pl.BlockSpec((1, D), lambda i, j: (0, 0)),
                  pl.BlockSpec((D, tf), lambda i, j: (0, j)),
                  pl.BlockSpec((tf, D), lambda i, j: (j, 0))],
        out_specs=pl.BlockSpec((tm, D), lambda i, j: (i, 0)),
        out_shape=jax.ShapeDtypeStruct((M, D), F32),
        scratch_shapes=[pltpu.VMEM((tm, D), BF16)],
        compiler_params=_params("parallel", "arbitrary"),
        name="mlp_residual",
    )(x, g.reshape(1, D), wu, wd)


def _hgrn_gates(z, lb):
    log_sig = jnp.minimum(z, 0.0) - jnp.log1p(jnp.exp(-jnp.abs(z)))
    a = jnp.log(lb)
    b = jnp.log1p(-lb) + log_sig
    log_f = jnp.maximum(a, b) + jnp.log1p(jnp.exp(-jnp.abs(a - b)))
    series = -(log_f + 0.5 * log_f * log_f + log_f * log_f * log_f * (1.0 / 6.0))
    k = jnp.where(log_f > -0.01, series, 1.0 - jnp.exp(log_f))
    return log_f, k


def _hgrn_out(o, on, g_raw):
    return _rms(o, on) * (g_raw * _sigmoid(g_raw))


def _hgrn_body(q_ref, f_ref, i_ref, g_ref, lb_ref, on_ref, o_ref, s_ref, st_ref, *, tb):
    C, SC = HGRN_CHUNK, HGRN_SUB
    t = pl.program_id(2)

    @pl.when(t == 0)
    def _():
        st_ref[...] = jnp.zeros_like(st_ref)

    lb = lb_ref[0]
    on = on_ref[...]
    tril = (lax.broadcasted_iota(jnp.int32, (C, C), 0) >= lax.broadcasted_iota(jnp.int32, (C, C), 1)).astype(F32)
    sub_r = lax.broadcasted_iota(jnp.int32, (SC, SC), 0)
    sub_c = lax.broadcasted_iota(jnp.int32, (SC, SC), 1)

    def chunk(c, carry):
        r0 = pl.multiple_of(c * C, C)
        rows = pl.ds(r0, C)
        qr = q_ref[rows, :]
        q = qr * _sigmoid(qr)
        log_f, kk = _hgrn_gates(f_ref[rows, :], lb)
        v = i_ref[rows, :]
        vb = v.astype(BF16)
        G = _dot(tril, log_f, precision=HIGHEST)
        st = st_ref[...]
        inter = _dot_nt((q * jnp.exp(G)).astype(BF16), st.astype(BF16))
        outs = []
        for I in range(C // SC):
            lo = I * SC
            GI, qI = G[lo:lo + SC], q[lo:lo + SC]
            oI = inter[lo:lo + SC]
            if I > 0:
                Gb = G[lo - 1:lo]
                qp = qI * jnp.exp(GI - Gb)
                kp = kk[:lo] * jnp.exp(Gb - G[:lo])
                a_off = _dot_nt(qp.astype(BF16), kp.astype(BF16))
                oI = oI + _dot(a_off.astype(BF16), vb[:lo])
            a_d = jnp.zeros((SC, SC), F32)
            for j in range(SC):
                e = jnp.exp(jnp.minimum(GI - GI[j:j + 1], 0.0))
                col = jnp.sum(qI * e * kk[lo + j:lo + j + 1], axis=1, keepdims=True)
                a_d = jnp.where(sub_c == j, col, a_d)
            a_d = jnp.where(sub_r >= sub_c, a_d, 0.0)
            oI = oI + _dot(a_d.astype(BF16), vb[lo:lo + SC])
            outs.append(oI)
        o = jnp.concatenate(outs, axis=0)
        Gl = G[C - 1:C]
        kd = kk * jnp.exp(Gl - G)
        st_ref[...] = jnp.exp(Gl) * st + _dot(v.T.astype(BF16), kd.astype(BF16))
        o_ref[rows, :] = _hgrn_out(o, on, g_ref[rows, :]).astype(o_ref.dtype)
        return carry

    lax.fori_loop(0, tb // C, chunk, 0)

    @pl.when(t == pl.num_programs(2) - 1)
    def _():
        s_ref[0, 0] = st_ref[...].T


def hgrn_prompt(proj, lb, on, n, T, heads, tb=256):
    nt = T // tb
    col = lambda k: pl.BlockSpec((tb, LANES), lambda b, h, t, k=k: (b * nt + t, k * heads + h))
    return pl.pallas_call(
        functools.partial(_hgrn_body, tb=tb),
        grid=(n, heads, nt),
        in_specs=[col(0), col(1), col(2), col(3),
                  pl.BlockSpec((1, 1, LANES), lambda b, h, t: (h, 0, 0)),
                  pl.BlockSpec((1, LANES), lambda b, h, t: (0, 0))],
        out_specs=[pl.BlockSpec((tb, LANES), lambda b, h, t: (b * nt + t, h)),
                   pl.BlockSpec((1, 1, LANES, LANES), lambda b, h, t: (b, h, 0, 0))],
        out_shape=[jax.ShapeDtypeStruct((n * T, heads * LANES), BF16),
                   jax.ShapeDtypeStruct((n, heads, LANES, LANES), F32)],
        scratch_shapes=[pltpu.VMEM((LANES, LANES), F32)],
        compiler_params=_params("parallel", "parallel", "arbitrary"),
        name="hgrn_prompt",
    )(proj, proj, proj, proj, lb.reshape(heads, 1, LANES), on.reshape(1, LANES))


def _col(eye, row):
    return jnp.sum(eye * row, axis=1, keepdims=True)


def _hgrn_s_body(q_ref, f_ref, i_ref, g_ref, lb_ref, on_ref, s_ref, o_ref, so_ref, *, nb):
    qr = q_ref[...]
    q = qr * _sigmoid(qr)
    log_f, kk = _hgrn_gates(f_ref[...], lb_ref[0])
    v = i_ref[...]
    f = jnp.exp(log_f)
    eye = (lax.broadcasted_iota(jnp.int32, (LANES, LANES), 0)
           == lax.broadcasted_iota(jnp.int32, (LANES, LANES), 1)).astype(F32)
    qf = (q * f).astype(BF16)
    a = jnp.sum(q * kk, axis=1, keepdims=True)
    rows = []
    for b in range(nb):
        S = s_ref[b, 0, 0]
        so_ref[b, 0] = _col(eye, f[b:b + 1]) * S + _col(eye, kk[b:b + 1]) * v[b:b + 1]
        rows.append(_dot(qf, S.astype(BF16))[b:b + 1])
    o = jnp.concatenate(rows, axis=0) + a * v
    o_ref[...] = _hgrn_out(o, on_ref[...], g_ref[...])


def hgrn_sample(proj, lb, on, state, li, heads):
    nb = proj.shape[0]
    col = lambda k: pl.BlockSpec((nb, LANES), lambda h, k=k: (0, k * heads + h))
    return pl.pallas_call(
        functools.partial(_hgrn_s_body, nb=nb),
        grid=(heads,),
        in_specs=[col(0), col(1), col(2), col(3),
                  pl.BlockSpec((1, 1, LANES), lambda h: (h, 0, 0)),
                  pl.BlockSpec((1, LANES), lambda h: (0, 0)),
                  pl.BlockSpec((nb, 1, 1, LANES, LANES), lambda h: (0, li, h, 0, 0))],
        out_specs=[pl.BlockSpec((nb, LANES), lambda h: (0, h)),
                   pl.BlockSpec((nb, 1, LANES, LANES), lambda h: (0, h, 0, 0))],
        out_shape=[jax.ShapeDtypeStruct((nb, heads * LANES), F32),
                   jax.ShapeDtypeStruct((nb, heads, LANES, LANES), F32)],
        compiler_params=_params("parallel"),
        name="hgrn_sample",
    )(proj, proj, proj, proj, lb.reshape(heads, 1, LANES), on.reshape(1, LANES), state)


def _dil_multiplicity(delta):
    c = np.zeros(delta.shape, np.float32)
    for window, dil in DIL_PATTERNS:
        c += ((delta >= 0) & (delta <= window) & (delta % dil == 0)).astype(np.float32)
    return c


def _dil_body(q_ref, k_ref, v_ref, qn_ref, kn_ref, c_ref, o_ref, ko_ref, vo_ref, kn_s, *, T, tq, W, nprev):
    qi = pl.program_id(2)

    @pl.when(qi == 0)
    def _():
        kw = kn_ref[...]

        def norm(c, carry):
            rows = pl.ds(pl.multiple_of(c * 512, 512), 512)
            kn_s[rows, :] = _rms(k_ref[rows, :], kw).astype(BF16)
            return carry

        lax.fori_loop(0, T // 512, norm, 0)
        ko_ref[0] = _rms(k_ref[T - W:, :], kw)
        vo_ref[0] = v_ref[T - W:, :]

    q = (_rms(q_ref[...], qn_ref[...]) * SCALE).astype(BF16)

    def body(s, carry):
        m, l, acc = carry
        rows = pl.ds(pl.multiple_of((qi - s) * tq, tq), tq)
        sc = _dot_nt(q, kn_s[rows, :])
        c = c_ref[s]
        sc = jnp.where(c > 0.0, sc, NEG_INF)
        m_new = jnp.maximum(m, jnp.max(sc, axis=1, keepdims=True))
        alpha = jnp.exp(m - m_new)
        p = c * jnp.exp(sc - m_new)
        l = alpha * l + jnp.sum(p, axis=1, keepdims=True)
        acc = alpha * acc + _dot(p.astype(BF16), v_ref[rows, :].astype(BF16))
        return m_new, l, acc

    init = (jnp.full((tq, 1), NEG_INF, F32), jnp.zeros((tq, 1), F32), jnp.zeros((tq, LANES), F32))
    _, l, acc = lax.fori_loop(0, jnp.minimum(qi, nprev) + 1, body, init)
    o_ref[...] = (acc / l).astype(o_ref.dtype)


def dil_prompt(proj, qn, kn, n, T, heads, col0, tq=128):
    W = min(DIL_MAX_WINDOW, T)
    nprev = min(DIL_MAX_WINDOW // tq, T // tq - 1)
    nt = T // tq
    d = np.arange(tq)[:, None] - np.arange(tq)[None, :]
    ctab = jnp.asarray(np.stack([_dil_multiplicity(s * tq + d) for s in range(nprev + 1)]))
    cb = col0 // LANES
    return pl.pallas_call(
        functools.partial(_dil_body, T=T, tq=tq, W=W, nprev=nprev),
        grid=(n, heads, nt),
        in_specs=[pl.BlockSpec((tq, LANES), lambda b, h, t: (b * nt + t, cb + h)),
                  pl.BlockSpec((T, LANES), lambda b, h, t: (b, cb + heads + h)),
                  pl.BlockSpec((T, LANES), lambda b, h, t: (b, cb + 2 * heads + h)),
                  pl.BlockSpec((1, LANES), lambda b, h, t: (0, 0)),
                  pl.BlockSpec((1, LANES), lambda b, h, t: (0, 0)),
                  pl.BlockSpec((nprev + 1, tq, tq), lambda b, h, t: (0, 0, 0))],
        out_specs=[pl.BlockSpec((tq, LANES), lambda b, h, t: (b * nt + t, h)),
                   pl.BlockSpec((1, W, LANES), lambda b, h, t: (b, 0, h)),
                   pl.BlockSpec((1, W, LANES), lambda b, h, t: (b, 0, h))],
        out_shape=[jax.ShapeDtypeStruct((n * T, heads * LANES), BF16),
                   jax.ShapeDtypeStruct((n, W, heads * LANES), F32),
                   jax.ShapeDtypeStruct((n, W, heads * LANES), F32)],
        scratch_shapes=[pltpu.VMEM((T, LANES), BF16)],
        compiler_params=_params("parallel", "parallel", "arbitrary"),
        name="dil_prompt",
    )(proj, proj, proj, qn.reshape(1, LANES), kn.reshape(1, LANES), ctab)


def _shift_in(buf, new_row):
    n = buf.shape[0]
    rolled = pltpu.roll(buf, n - 1, 0)
    return jnp.where(lax.broadcasted_iota(jnp.int32, buf.shape, 0) == n - 1, new_row, rolled)


def _dil_s_body(q_ref, k_ref, v_ref, qn_ref, kn_ref, c_ref, ck_ref, cv_ref, o_ref, ko_ref, vo_ref, *, c_new):
    b = pl.program_id(1)
    row = pl.ds(b, 1)
    q = _rms(q_ref[row, :], qn_ref[...]) * SCALE
    kn = _rms(k_ref[row, :], kn_ref[...])
    v = v_ref[row, :]
    kb, vb = ck_ref[0, 0], cv_ref[0, 0]
    q8 = jnp.broadcast_to(q, (8, LANES)).astype(BF16)
    s = _dot_nt(q8, kb.astype(BF16))
    c = c_ref[...]
    s = jnp.where(c > 0.0, s, NEG_INF)
    s_new = jnp.sum(q * kn, axis=1, keepdims=True)
    m = jnp.maximum(jnp.max(s, axis=1, keepdims=True), s_new)
    p = c * jnp.exp(s - m)
    p_new = c_new * jnp.exp(s_new - m)
    l = jnp.sum(p, axis=1, keepdims=True) + p_new
    o = (_dot(p.astype(BF16), vb.astype(BF16)) + p_new * v) / l
    o_ref[row, :] = o[0:1]
    ko_ref[0] = _shift_in(kb, kn)
    vo_ref[0] = _shift_in(vb, v)


def dil_sample(proj, qn, kn, cache_k, cache_v, li, heads, col0):
    nb = proj.shape[0]
    Wb = cache_k.shape[2]
    ck = cache_k.reshape(nb, cache_k.shape[1], Wb, heads * LANES)
    cv = cache_v.reshape(nb, cache_v.shape[1], Wb, heads * LANES)
    delta = Wb - np.arange(Wb)
    c_buf = jnp.asarray(_dil_multiplicity(delta)[None, :])
    c_new = float(_dil_multiplicity(np.zeros((1,), np.int64))[0])
    cb = col0 // LANES
    return pl.pallas_call(
        functools.partial(_dil_s_body, c_new=c_new),
        grid=(heads, nb),
        in_specs=[pl.BlockSpec((nb, LANES), lambda h, b: (0, cb + h)),
                  pl.BlockSpec((nb, LANES), lambda h, b: (0, cb + heads + h)),
                  pl.BlockSpec((nb, LANES), lambda h, b: (0, cb + 2 * heads + h)),
                  pl.BlockSpec((1, LANES), lambda h, b: (0, 0)),
                  pl.BlockSpec((1, LANES), lambda h, b: (0, 0)),
                  pl.BlockSpec((1, Wb), lambda h, b: (0, 0)),
                  pl.BlockSpec((1, 1, Wb, LANES), lambda h, b: (b, li, 0, h)),
                  pl.BlockSpec((1, 1, Wb, LANES), lambda h, b: (b, li, 0, h))],
        out_specs=[pl.BlockSpec((nb, LANES), lambda h, b: (0, h)),
                   pl.BlockSpec((1, Wb, LANES), lambda h, b: (b, 0, h)),
                   pl.BlockSpec((1, Wb, LANES), lambda h, b: (b, 0, h))],
        out_shape=[jax.ShapeDtypeStruct((nb, heads * LANES), F32),
                   jax.ShapeDtypeStruct((nb, Wb, heads * LANES), F32),
                   jax.ShapeDtypeStruct((nb, Wb, heads * LANES), F32)],
        compiler_params=_params("parallel", "arbitrary"),
        name="dil_sample",
    )(proj, proj, proj, qn.reshape(1, LANES), kn.reshape(1, LANES), c_buf, ck, cv)


def _gelu(x):
    return 0.5 * x * (1.0 + jnp.tanh(0.7978845608028654 * (x + 0.044715 * x * x * x)))


def _compress_rows(load_j, pe_ref, w1_ref, w2_ref, rows):
    acc = jnp.zeros((rows, LANES), F32)
    for j in range(CMP_LEN):
        acc = acc + _dot((load_j(j) + pe_ref[j:j + 1, :]).astype(BF16), w1_ref[j])
    return _dot(_gelu(acc).astype(BF16), w2_ref[...])


def _nsa_prep_body(kc_ref, vc_ref, ks_ref, vs_ref, kw_ref, vw_ref, knorm_ref, pek_ref, pev_ref,
                   w1k_ref, w2k_ref, w1v_ref, w2v_ref,
                   kcmp_ref, vcmp_ref, ksb_ref, ksf_ref, vst_ref, kwb_ref, kwf_ref, vwt_ref, *, T):
    nblk = T // CMP_LEN
    kcmp = _compress_rows(lambda j: kc_ref[pl.ds(j, nblk, stride=CMP_LEN), :], pek_ref, w1k_ref, w2k_ref, nblk)
    kcmp_ref[0, 0] = _rms(kcmp, knorm_ref[0:1, :])
    vcmp_ref[0, 0] = _compress_rows(lambda j: vc_ref[pl.ds(j, nblk, stride=CMP_LEN), :], pev_ref, w1v_ref, w2v_ref, nblk)
    ks_w, kw_w = knorm_ref[1:2, :], knorm_ref[2:3, :]

    def tile(c, carry):
        rows = pl.ds(pl.multiple_of(c * LANES, LANES), LANES)
        ksn = _rms(ks_ref[rows, :], ks_w)
        ksf_ref[0, rows, :] = ksn
        ksb_ref[0, 0, rows, :] = ksn.astype(BF16)
        kwb_ref[0, 0, rows, :] = _rms(kw_ref[rows, :], kw_w).astype(BF16)
        vst_ref[0, 0, c] = vs_ref[rows, :].T.astype(BF16)
        vwt_ref[0, 0, c] = vw_ref[rows, :].T.astype(BF16)
        return carry

    lax.fori_loop(0, T // LANES, tile, 0)
    ww = min(WIN, T)
    kwf_ref[0] = _rms(kw_ref[T - ww:, :], kw_w)


def nsa_prep(proj, knorm, pe_k, pe_v, w1k, w2k, w1v, w2v, n, T, hk):
    cb = (hk * C_GROUP * HEAD_DIM) // LANES
    nblk = T // CMP_LEN
    nt = T // LANES
    ww = min(WIN, T)
    col = lambda k: pl.BlockSpec((T, LANES), lambda b, h, k=k: (b, cb + k * hk + h))
    full = lambda a: pl.BlockSpec(a.shape, lambda b, h: (0,) * a.ndim)
    per = lambda *s: pl.BlockSpec((1, 1) + s, lambda b, h: (b, h) + (0,) * len(s))
    return pl.pallas_call(
        functools.partial(_nsa_prep_body, T=T),
        grid=(n, hk),
        in_specs=[col(0), col(1), col(2), col(3), col(4), col(5),
                  full(knorm), full(pe_k), full(pe_v), full(w1k), full(w2k), full(w1v), full(w2v)],
        out_specs=[per(nblk, LANES), per(nblk, LANES), per(T, LANES),
                   pl.BlockSpec((1, T, LANES), lambda b, h: (b, 0, h)),
                   per(nt, LANES, LANES), per(T, LANES),
                   pl.BlockSpec((1, ww, LANES), lambda b, h: (b, 0, h)),
                   per(nt, LANES, LANES)],
        out_shape=[jax.ShapeDtypeStruct((n, hk, nblk, LANES), F32),
                   jax.ShapeDtypeStruct((n, hk, nblk, LANES), F32),
                   jax.ShapeDtypeStruct((n, hk, T, LANES), BF16),
                   jax.ShapeDtypeStruct((n, T, hk * LANES), F32),
                   jax.ShapeDtypeStruct((n, hk, nt, LANES, LANES), BF16),
                   jax.ShapeDtypeStruct((n, hk, T, LANES), BF16),
                   jax.ShapeDtypeStruct((n, ww, hk * LANES), F32),
                   jax.ShapeDtypeStruct((n, hk, nt, LANES, LANES), BF16)],
        compiler_params=_params("parallel", "parallel"),
        name="nsa_prep",
    )(proj, proj, proj, proj, proj, proj, knorm, pe_k, pe_v, w1k, w2k, w1v, w2v)


def _tile4(x):
    return jnp.concatenate([x] * C_GROUP, axis=1)


def _nsa_body(q_ref, gate_ref, qn_ref, kcmp_ref, vcmp_ref, ks_ref, vst_ref, kw_ref, vwt_ref, o_ref,
              vct_s, pb_s, sel_s, gt_s, *, tq, nblk, nslc, hk_n):
    G = C_GROUP
    hk = pl.program_id(1)
    qi = pl.program_id(2)
    t0 = qi * tq

    @pl.when(qi == 0)
    def _():
        vct_s[...] = vcmp_ref[0, 0].T.astype(BF16)

    qw = qn_ref[...]
    q4 = jnp.concatenate(
        [(_rms(q_ref[:, g * LANES:(g + 1) * LANES], qw) * SCALE).astype(BF16) for g in range(G)], axis=0)

    st = _dot_nt(kcmp_ref[0, 0].astype(BF16), q4)
    blk = lax.broadcasted_iota(jnp.int32, (nblk, G * tq), 0)
    tpos = t0 + (lax.broadcasted_iota(jnp.int32, (nblk, G * tq), 1) & (tq - 1))
    valid = (blk + 1) * CMP_LEN - 1 <= tpos
    st = jnp.where(valid, st, NEG_INF)
    p = jnp.where(valid, jnp.exp(st - jnp.max(st, axis=0, keepdims=True)), 0.0)
    p = p / jnp.maximum(jnp.sum(p, axis=0, keepdims=True), 1.0)
    o_cmp = _dot(vct_s[...], p.astype(BF16))
    pb = p[:, 0:tq]
    for g in range(1, G):
        pb = pb + p[:, g * tq:(g + 1) * tq]
    pb_s[...] = pb
    ratio = SLC_BLOCK // CMP_LEN
    imp = pb_s[pl.ds(0, nslc, stride=ratio), :]
    for r in range(1, ratio):
        imp = imp + pb_s[pl.ds(r, nslc, stride=ratio), :]

    jb = lax.broadcasted_iota(jnp.int32, (nslc, tq), 0)
    tp = t0 + lax.broadcasted_iota(jnp.int32, (nslc, tq), 1)
    cur = tp // SLC_BLOCK
    forced = (jb == 0) | (jb == cur) | (jb == cur - 1)
    score = jnp.where(jb * SLC_BLOCK <= tp, jnp.where(forced, FORCE_SCORE, imp), -FORCE_SCORE)
    rank = jnp.zeros((nslc, tq), F32)
    for jp in range(nslc):
        row = score[jp:jp + 1, :]
        ahead = (row > score) | ((row == score) & (jb > jp))
        rank = rank + ahead.astype(F32)
    sel_s[...] = (rank < float(min(SLC_TOPN, nslc))).astype(F32)

    ksub = lax.broadcasted_iota(jnp.int32, (LANES, tq), 0)
    qlane = t0 + lax.broadcasted_iota(jnp.int32, (LANES, tq), 1)
    per_tile = LANES // SLC_BLOCK

    def attend(kt, carry, k_ref, vt_ref, ok):
        m, l, acc = carry
        rows = pl.ds(pl.multiple_of(kt * LANES, LANES), LANES)
        s = _dot_nt(k_ref[0, 0, rows, :], q4)
        s = s + _tile4(jnp.where(ok, 0.0, NEG_INF))
        m_new = jnp.maximum(m, jnp.max(s, axis=0, keepdims=True))
        alpha = jnp.exp(m - m_new)
        pp = jnp.exp(s - m_new) * _tile4(ok.astype(F32))
        l = alpha * l + jnp.sum(pp, axis=0, keepdims=True)
        acc = alpha * acc + _dot(vt_ref[0, 0, kt], pp.astype(BF16))
        return m_new, l, acc

    def slc_step(kt, carry):
        kpos = kt * LANES + ksub
        chosen = jnp.zeros((LANES, tq), F32)
        for r in range(per_tile):
            srow = sel_s[pl.ds(kt * per_tile + r, 1), :]
            chosen = jnp.where(ksub // SLC_BLOCK == r, srow, chosen)
        return attend(kt, carry, ks_ref, vst_ref, (chosen > 0.0) & (kpos <= qlane))

    def win_step(i, carry):
        kt = jnp.maximum(qi - WIN // LANES, 0) + i
        dist = qlane - (kt * LANES + ksub)
        return attend(kt, carry, kw_ref, vwt_ref, (dist >= 0) & (dist <= WIN))

    init = (jnp.full((1, G * tq), NEG_INF, F32), jnp.zeros((1, G * tq), F32), jnp.zeros((LANES, G * tq), F32))
    _, l_s, acc_s = lax.fori_loop(0, qi + 1, slc_step, init)
    _, l_w, acc_w = lax.fori_loop(0, jnp.minimum(qi, WIN // LANES) + 1, win_step, init)
    o_slc = acc_s / l_s
    o_win = acc_w / l_w

    gt_s[...] = _sigmoid(gate_ref[...]).T
    nh = G * hk_n
    for g in range(G):
        sl = slice(g * tq, (g + 1) * tq)
        g0 = gt_s[pl.ds(hk * G + g, 1), :]
        g1 = gt_s[pl.ds(nh + hk * G + g, 1), :]
        g2 = gt_s[pl.ds(2 * nh + hk * G + g, 1), :]
        o = g0 * o_cmp[:, sl] + g1 * o_slc[:, sl] + g2 * o_win[:, sl]
        o_ref[:, g * LANES:(g + 1) * LANES] = o.T.astype(o_ref.dtype)


def nsa_prompt(proj, qn, kcmp, vcmp, ksb, vst, kwb, vwt, n, T, hk, gate_col, tq=128):
    nblk, nslc, nt = T // CMP_LEN, T // SLC_BLOCK, T // tq
    gw = C_GROUP * LANES
    per = lambda *s: pl.BlockSpec((1, 1) + s, lambda b, h, t: (b, h) + (0,) * len(s))
    return pl.pallas_call(
        functools.partial(_nsa_body, tq=tq, nblk=nblk, nslc=nslc, hk_n=hk),
        grid=(n, hk, nt),
        in_specs=[pl.BlockSpec((tq, gw), lambda b, h, t: (b * nt + t, h)),
                  pl.BlockSpec((tq, LANES), lambda b, h, t: (b * nt + t, gate_col // LANES)),
                  pl.BlockSpec((1, LANES), lambda b, h, t: (0, 0)),
                  per(nblk, LANES), per(nblk, LANES), per(T, LANES), per(T // LANES, LANES, LANES),
                  per(T, LANES), per(T // LANES, LANES, LANES)],
        out_specs=pl.BlockSpec((tq, gw), lambda b, h, t: (b * nt + t, h)),
        out_shape=jax.ShapeDtypeStruct((n * T, hk * gw), BF16),
        scratch_shapes=[pltpu.VMEM((LANES, nblk), BF16), pltpu.VMEM((nblk, tq), F32),
                        pltpu.VMEM((nslc, tq), F32), pltpu.VMEM((LANES, tq), F32)],
        compiler_params=_params("parallel", "parallel", "arbitrary"),
        name="nsa_prompt",
    )(proj, proj, qn.reshape(1, LANES), kcmp, vcmp, ksb, vst, kwb, vwt)


def _cmp_pages_body(pt_ref, ck_hbm, cv_hbm, knorm_ref, pek_ref, pev_ref, w1k_ref, w2k_ref, w1v_ref, w2v_ref,
                    ko_ref, vo_ref, kbuf, vbuf, sem, *, li, P, hk, rows_per_page):
    s = pl.program_id(0)
    ns = pl.num_programs(0)

    def copies(step, slot):
        out = []
        for p in range(P):
            page = pt_ref[step * P + p]
            dst = pl.ds(p * rows_per_page, rows_per_page)
            out.append(pltpu.make_async_copy(ck_hbm.at[page, li], kbuf.at[slot, dst], sem.at[0, slot]))
            out.append(pltpu.make_async_copy(cv_hbm.at[page, li], vbuf.at[slot, dst], sem.at[1, slot]))
        return out

    @pl.when(s == 0)
    def _():
        for c in copies(0, 0):
            c.start()

    @pl.when(s + 1 < ns)
    def _():
        for c in copies(s + 1, (s + 1) % 2):
            c.start()

    slot = s % 2
    for c in copies(s, slot):
        c.wait()

    blocks = P * (rows_per_page // hk // CMP_LEN)
    stride = CMP_LEN * hk

    def loader(buf):
        def load_j(j):
            return jnp.concatenate(
                [buf[slot, pl.ds(j * hk + h, blocks, stride=stride), :] for h in range(hk)], axis=0)
        return load_j

    kcmp = _compress_rows(loader(kbuf), pek_ref, w1k_ref, w2k_ref, hk * blocks)
    kcmp = _rms(kcmp, knorm_ref[0:1, :])
    vcmp = _compress_rows(loader(vbuf), pev_ref, w1v_ref, w2v_ref, hk * blocks)
    for h in range(hk):
        ko_ref[0, h] = kcmp[h * blocks:(h + 1) * blocks]
        vo_ref[0, h] = vcmp[h * blocks:(h + 1) * blocks]


def cmp_pages(page_table, cache_k, cache_v, li, knorm, pe_k, pe_v, w1k, w2k, w1v, w2v, P=16):
    nb, n_pages = page_table.shape
    n_pool, n_l, page, hk, dh = cache_k.shape
    rpp = page * hk
    ck = cache_k.reshape(n_pool, n_l, rpp, dh)
    cv = cache_v.reshape(n_pool, n_l, rpp, dh)
    steps_per_b = n_pages // P
    blocks = P * (page // CMP_LEN)
    nblk = n_pages * (page // CMP_LEN)
    full = lambda a: pl.BlockSpec(a.shape, lambda s, pt: (0,) * a.ndim)
    out_spec = pl.BlockSpec((1, hk, blocks, dh), lambda s, pt: (s // steps_per_b, 0, s % steps_per_b, 0))
    grid_spec = pltpu.PrefetchScalarGridSpec(
        num_scalar_prefetch=1,
        grid=(nb * steps_per_b,),
        in_specs=[pl.BlockSpec(memory_space=pl.ANY), pl.BlockSpec(memory_space=pl.ANY),
                  full(knorm), full(pe_k), full(pe_v), full(w1k), full(w2k), full(w1v), full(w2v)],
        out_specs=[out_spec, out_spec],
        scratch_shapes=[pltpu.VMEM((2, P * rpp, dh), F32), pltpu.VMEM((2, P * rpp, dh), F32),
                        pltpu.SemaphoreType.DMA((2, 2))])
    return pl.pallas_call(
        functools.partial(_cmp_pages_body, li=li, P=P, hk=hk, rows_per_page=rpp),
        grid_spec=grid_spec,
        out_shape=[jax.ShapeDtypeStruct((nb, hk, nblk, dh), F32)] * 2,
        compiler_params=_params("arbitrary"),
        name="cmp_pages",
    )(page_table.reshape(-1), ck, cv, knorm, pe_k, pe_v, w1k, w2k, w1v, w2v)


def _nsa_s_select_body(q_ref, kc_ref, vc_ref, ks_ref, qn_ref, knorm_ref, pek_ref, pev_ref,
                       w1k_ref, w2k_ref, w1v_ref, w2v_ref, kcmp_ref, vcmp_ref,
                       qo_ref, ocmp_ref, sel_ref, kso_ref, *, hk, qpos, nblk):
    G = C_GROUP
    row = slice(None)
    qw = qn_ref[...]
    nslc = (nblk + 1 + 1) // 2
    lanes_blk = lax.broadcasted_iota(jnp.int32, (8, nblk), 1)
    valid = (lanes_blk + 1) * CMP_LEN - 1 <= qpos
    valid_x = (jnp.full((8, 1), (nblk + 1) * CMP_LEN - 1, jnp.int32) <= qpos)
    pair = (lax.broadcasted_iota(jnp.int32, (nblk, nblk // 2), 0) // 2
            == lax.broadcasted_iota(jnp.int32, (nblk, nblk // 2), 1)).astype(F32)
    sub8 = lax.broadcasted_iota(jnp.int32, (8, 1), 0)
    pe_rest_k = jnp.zeros((8, LANES), F32)
    pe_rest_v = jnp.zeros((8, LANES), F32)
    for j in range(1, CMP_LEN):
        pe_rest_k = pe_rest_k + _dot(jnp.broadcast_to(pek_ref[j:j + 1, :], (8, LANES)).astype(BF16), w1k_ref[j])
        pe_rest_v = pe_rest_v + _dot(jnp.broadcast_to(pev_ref[j:j + 1, :], (8, LANES)).astype(BF16), w1v_ref[j])
    for h in range(hk):
        qs = [_rms(q_ref[row, (h * G + g) * LANES:(h * G + g + 1) * LANES], qw) * SCALE for g in range(G)]
        q8 = jnp.concatenate(qs + [jnp.zeros((8 - G, LANES), F32)], axis=0)
        qo_ref[0, h] = q8
        kso_ref[0, h] = jnp.broadcast_to(_rms(ks_ref[row, h * LANES:(h + 1) * LANES], knorm_ref[1:2, :]), (8, LANES))
        xk = jnp.broadcast_to(kc_ref[row, h * LANES:(h + 1) * LANES] + pek_ref[0:1, :], (8, LANES))
        xv = jnp.broadcast_to(vc_ref[row, h * LANES:(h + 1) * LANES] + pev_ref[0:1, :], (8, LANES))
        k_x = _dot(_gelu(_dot(xk.astype(BF16), w1k_ref[0]) + pe_rest_k).astype(BF16), w2k_ref[...])
        k_x = _rms(k_x, knorm_ref[0:1, :])
        v_x = _dot(_gelu(_dot(xv.astype(BF16), w1v_ref[0]) + pe_rest_v).astype(BF16), w2v_ref[...])
        s = jnp.where(valid, _dot_nt(q8.astype(BF16), kcmp_ref[0, h].astype(BF16)), NEG_INF)
        s_x = jnp.where(valid_x, jnp.sum(q8 * k_x, axis=1, keepdims=True), NEG_INF)
        m = jnp.maximum(jnp.max(s, axis=1, keepdims=True), s_x)
        p = jnp.where(valid, jnp.exp(s - m), 0.0)
        p_x = jnp.where(valid_x, jnp.exp(s_x - m), 0.0)
        den = jnp.maximum(jnp.sum(p, axis=1, keepdims=True) + p_x, 1.0)
        p = jnp.where(sub8 < G, p / den, 0.0)
        p_x = jnp.where(sub8 < G, p_x / den, 0.0)
        ocmp_ref[0, h] = _dot(p.astype(BF16), vcmp_ref[0, h].astype(BF16)) + p_x * v_x
        pb = jnp.sum(p, axis=0, keepdims=True)
        pb_x = jnp.sum(p_x, axis=0, keepdims=True)
        imp = _dot(jnp.broadcast_to(pb, (8, nblk)), pair, precision=HIGHEST)[0:1]
        lane = lax.broadcasted_iota(jnp.int32, (1, LANES), 1)
        tail = jnp.where(lane == 0, pb_x, -jnp.inf)
        imp = jnp.concatenate([imp, tail], axis=1)
        width = imp.shape[1]
        jb = lax.broadcasted_iota(jnp.int32, (1, width), 1)
        cur = qpos // SLC_BLOCK
        forced = (jb == 0) | (jb == cur) | (jb == cur - 1)
        score = jnp.where(jb * SLC_BLOCK <= qpos, jnp.where(forced, FORCE_SCORE, imp), -FORCE_SCORE)
        score = jnp.where(jb < nslc, score, -jnp.inf)
        sel = jnp.zeros((1, LANES), jnp.int32)
        for r in range(SLC_TOPN):
            best = jnp.max(score, axis=1, keepdims=True)
            idx = jnp.min(jnp.where(score == best, jb, width), axis=1, keepdims=True)
            sel = jnp.where(lane == r, idx, sel)
            score = jnp.where(jb == idx, -jnp.inf, score)
        sel_ref[0, h] = jnp.broadcast_to(sel, (8, LANES))


def nsa_sample_select(proj, qn, knorm, pe_k, pe_v, w1k, w2k, w1v, w2v, kcmp, vcmp, hk, qpos):
    nb = proj.shape[0]
    nblk = kcmp.shape[2]
    heads = hk * C_GROUP
    cq, ckv = heads * LANES, hk * LANES
    full = lambda a: pl.BlockSpec(a.shape, lambda b: (0,) * a.ndim)
    per = lambda *s: pl.BlockSpec((1,) + s, lambda b: (b,) + (0,) * len(s))
    colspec = lambda c0, w: pl.BlockSpec((None, 1, w), lambda b: (b, 0, c0 // w))
    proj = proj.reshape(nb, 1, -1)
    out8 = jax.ShapeDtypeStruct((nb, hk, 8, LANES), F32)
    return pl.pallas_call(
        functools.partial(_nsa_s_select_body, hk=hk, qpos=qpos, nblk=nblk),
        grid=(nb,),
        in_specs=[colspec(0, cq), colspec(cq, ckv), colspec(cq + ckv, ckv), colspec(cq + 2 * ckv, ckv),
                  pl.BlockSpec((1, LANES), lambda b: (0, 0)),
                  full(knorm), full(pe_k), full(pe_v), full(w1k), full(w2k), full(w1v), full(w2v),
                  per(hk, nblk, LANES), per(hk, nblk, LANES)],
        out_specs=[per(hk, 8, LANES)] * 4,
        out_shape=[out8, out8, jax.ShapeDtypeStruct((nb, hk, 8, LANES), jnp.int32), out8],
        compiler_params=_params("arbitrary"),
        name="nsa_sample_select",
    )(proj, proj, proj, proj, qn.reshape(1, LANES), knorm, pe_k, pe_v, w1k, w2k, w1v, w2v, kcmp, vcmp)


def _nsa_s_slc_body(sel_ref, pt_ref, q_ref, ksn_ref, vsn_ref, ks_ref, vs_ref, o_ref, m_s, l_s, acc_s,
                    *, hk, n_past_blocks):
    b, h, r = pl.program_id(0), pl.program_id(1), pl.program_id(2)

    @pl.when(r == 0)
    def _():
        m_s[...] = jnp.full_like(m_s, NEG_INF)
        l_s[...] = jnp.zeros_like(l_s)
        acc_s[...] = jnp.zeros_like(acc_s)

    blk = sel_ref[(b * hk + h) * SLC_TOPN + r]
    is_new = blk >= n_past_blocks
    rows = pl.ds(h, SLC_BLOCK, stride=hk)
    first = lax.broadcasted_iota(jnp.int32, (SLC_BLOCK, LANES), 0) == 0
    k = jnp.where(is_new, jnp.where(first, ksn_ref[0, 0, 0:1, :], 0.0), ks_ref[0, 0, rows, :])
    v = jnp.where(is_new, jnp.where(first, vsn_ref[pl.ds(b, 1), :], 0.0), vs_ref[0, 0, rows, :])
    q8 = q_ref[0, 0]
    s = _dot_nt(q8.astype(BF16), k.astype(BF16))
    ok = jnp.logical_or(jnp.logical_not(is_new), lax.broadcasted_iota(jnp.int32, s.shape, 1) == 0)
    s = jnp.where(ok, s, NEG_INF)
    m_old = m_s[...]
    m_new = jnp.maximum(m_old, jnp.max(s, axis=1, keepdims=True))
    alpha = jnp.exp(m_old - m_new)
    p = jnp.where(ok, jnp.exp(s - m_new[:, 0:1]), 0.0)
    l_s[...] = alpha * l_s[...] + jnp.sum(p, axis=1, keepdims=True)
    acc_s[...] = alpha * acc_s[...] + _dot(p.astype(BF16), v.astype(BF16))
    m_s[...] = m_new

    @pl.when(r == pl.num_programs(2) - 1)
    def _():
        o_ref[0, 0] = acc_s[...] / l_s[...]


def nsa_sample_selected(sel, page_table, q8, ksn, proj, vs_col, cache_k, cache_v, li, hk):
    nb, n_pages = page_table.shape
    n_pool, n_l, page, _, dh = cache_k.shape
    rpp = page * hk
    per_page = page // SLC_BLOCK
    n_past_blocks = n_pages * per_page
    ck = cache_k.reshape(n_pool, n_l, rpp, dh)
    cv = cache_v.reshape(n_pool, n_l, rpp, dh)
    sel_flat = sel[:, :, 0, :SLC_TOPN].reshape(-1)

    def page_map(b, h, r, sel_r, pt_r):
        blk = jnp.minimum(sel_r[(b * hk + h) * SLC_TOPN + r], n_past_blocks - 1)
        return (pt_r[b * n_pages + blk // per_page], li, blk % per_page, 0)

    grid_spec = pltpu.PrefetchScalarGridSpec(
        num_scalar_prefetch=2,
        grid=(nb, hk, SLC_TOPN),
        in_specs=[pl.BlockSpec((1, 1, 8, LANES), lambda b, h, r, s_, p_: (b, h, 0, 0)),
                  pl.BlockSpec((1, 1, 8, LANES), lambda b, h, r, s_, p_: (b, h, 0, 0)),
                  pl.BlockSpec((nb, LANES), lambda b, h, r, s_, p_: (0, vs_col // LANES + h)),
                  pl.BlockSpec((1, 1, SLC_BLOCK * hk, dh), page_map),
                  pl.BlockSpec((1, 1, SLC_BLOCK * hk, dh), page_map)],
        out_specs=pl.BlockSpec((1, 1, 8, LANES), lambda b, h, r, s_, p_: (b, h, 0, 0)),
        scratch_shapes=[pltpu.VMEM((8, LANES), F32)] * 3)
    return pl.pallas_call(
        functools.partial(_nsa_s_slc_body, hk=hk, n_past_blocks=n_past_blocks),
        grid_spec=grid_spec,
        out_shape=jax.ShapeDtypeStruct((nb, hk, 8, LANES), F32),
        compiler_params=_params("arbitrary", "arbitrary", "arbitrary"),
        name="nsa_sample_selected",
    )(sel_flat, page_table.reshape(-1), q8, ksn, proj, ck, cv)


def _nsa_s_win_body(q_ref, ocmp_ref, oslc_ref, kw_ref, vw_ref, gate_ref, knorm_ref, wk_ref, wv_ref,
                    o_ref, wko_ref, wvo_ref, *, hk):
    G = C_GROUP
    row = slice(None)
    eye = (lax.broadcasted_iota(jnp.int32, (LANES, LANES), 0)
           == lax.broadcasted_iota(jnp.int32, (LANES, LANES), 1)).astype(F32)
    gcol = _col(eye, _sigmoid(gate_ref[row, :]))
    nh = G * hk
    for h in range(hk):
        lanes = slice(h * LANES, (h + 1) * LANES)
        q8 = q_ref[0, h]
        kn = _rms(kw_ref[row, lanes], knorm_ref[2:3, :])
        v = vw_ref[row, lanes]
        kb, vb = wk_ref[0, 0, :, lanes], wv_ref[0, 0, :, lanes]
        s = _dot_nt(q8.astype(BF16), kb.astype(BF16))
        s_new = jnp.sum(q8 * kn, axis=1, keepdims=True)
        m = jnp.maximum(jnp.max(s, axis=1, keepdims=True), s_new)
        p = jnp.exp(s - m)
        p_new = jnp.exp(s_new - m)
        l = jnp.sum(p, axis=1, keepdims=True) + p_new
        o_win = (_dot(p.astype(BF16), vb.astype(BF16)) + p_new * v) / l
        g0 = gcol[h * G:h * G + 8]
        g1 = gcol[nh + h * G:nh + h * G + 8]
        g2 = gcol[2 * nh + h * G:2 * nh + h * G + 8]
        o = g0 * ocmp_ref[0, h] + g1 * oslc_ref[0, h] + g2 * o_win
        for g in range(G):
            o_ref[row, (h * G + g) * LANES:(h * G + g + 1) * LANES] = o[g:g + 1]
        wko_ref[0, :, lanes] = _shift_in(kb, kn)
        wvo_ref[0, :, lanes] = _shift_in(vb, v)


def nsa_sample_window(q8, ocmp, oslc, proj, kw_col, gate_col, knorm, win_k, win_v, li, hk):
    nb = proj.shape[0]
    Wb = win_k.shape[2]
    ckv = hk * LANES
    wk = win_k.reshape(nb, win_k.shape[1], Wb, ckv)
    wv = win_v.reshape(nb, win_v.shape[1], Wb, ckv)
    per = lambda *s: pl.BlockSpec((1,) + s, lambda b: (b,) + (0,) * len(s))
    proj = proj.reshape(nb, 1, -1)
    cw = hk * C_GROUP * LANES
    o, wko, wvo = pl.pallas_call(
        functools.partial(_nsa_s_win_body, hk=hk),
        grid=(nb,),
        in_specs=[per(hk, 8, LANES), per(hk, 8, LANES), per(hk, 8, LANES),
                  pl.BlockSpec((None, 1, ckv), lambda b: (b, 0, kw_col // ckv)),
                  pl.BlockSpec((None, 1, ckv), lambda b: (b, 0, kw_col // ckv + 1)),
                  pl.BlockSpec((None, 1, LANES), lambda b: (b, 0, gate_col // LANES)),
                  pl.BlockSpec(knorm.shape, lambda b: (0, 0)),
                  pl.BlockSpec((1, 1, Wb, ckv), lambda b: (b, li, 0, 0)),
                  pl.BlockSpec((1, 1, Wb, ckv), lambda b: (b, li, 0, 0))],
        out_specs=[pl.BlockSpec((None, 1, cw), lambda b: (b, 0, 0)),
                   per(Wb, ckv), per(Wb, ckv)],
        out_shape=[jax.ShapeDtypeStruct((nb, 1, cw), F32),
                   jax.ShapeDtypeStruct((nb, Wb, ckv), F32),
                   jax.ShapeDtypeStruct((nb, Wb, ckv), F32)],
        compiler_params=_params("parallel"),
        name="nsa_sample_window",
    )(q8, ocmp, oslc, proj, proj, proj, knorm, wk, wv)
    return o.reshape(nb, cw), wko, wvo


def _pad_cols(w, mult):
    pad = (-w.shape[-1]) % mult
    return jnp.pad(w, ((0, 0),) * (w.ndim - 1) + ((0, pad),)) if pad else w


def kernel(x_prompt, x_sample, state_hgrn, cache_dil_k, cache_dil_v, cache_cmp_k, cache_cmp_v, cache_slc_k, cache_slc_v, cache_win_k, cache_win_v, page_table, norm_mix, norm_mlp, w_in_even, w_out_even, hgrn_lb_logits, hgrn_out_norm, dil_q_norm, dil_k_norm, w_in_odd, w_out_odd, nsa_q_norm, nsa_k_norm, nsa_pe_k, nsa_pe_v, nsa_phi_k1, nsa_phi_k2, nsa_phi_v1, nsa_phi_v2, w_mlp_up, w_mlp_down):
    n, T, D = x_prompt.shape
    nb = x_sample.shape[0]
    assert x_sample.shape[1] == 1
    depth = norm_mix.shape[0]
    a_heads = hgrn_lb_logits.shape[1] // LANES
    b_heads = cache_dil_k.shape[3]
    hk = cache_win_k.shape[3]
    c_heads = hk * C_GROUP
    past_len = page_table.shape[1] * cache_cmp_k.shape[2]
    a_w = a_heads * LANES
    TN = 896
    TM = 512

    lb_cum = jnp.cumsum(jax.nn.softmax(hgrn_lb_logits.astype(F32), axis=0), axis=0)
    lower_bounds = lb_cum - lb_cum[0:1]

    cq, ckv = c_heads * LANES, hk * LANES
    gate_col = cq + 6 * ckv
    gate_w = w_in_odd[:, :, gate_col:].reshape(-1, D, hk, C_GROUP, 3).transpose(0, 1, 4, 2, 3).reshape(-1, D, 3 * c_heads)
    w_in_odd_p = jnp.concatenate([w_in_odd[:, :, :gate_col], _pad_cols(gate_w, LANES)], axis=-1)
    w_in_odd_p = _pad_cols(w_in_odd_p, TN).astype(BF16)
    w_in_even_b = w_in_even.astype(BF16)
    w_out_even_b = w_out_even.astype(BF16)
    w_out_odd_b = w_out_odd.astype(BF16)
    w_up_b = w_mlp_up.astype(BF16)
    w_down_b = w_mlp_down.astype(BF16)
    phi_k1 = nsa_phi_k1.reshape(-1, CMP_LEN, LANES, LANES).astype(BF16)
    phi_v1 = nsa_phi_v1.reshape(-1, CMP_LEN, LANES, LANES).astype(BF16)
    phi_k2 = nsa_phi_k2.astype(BF16)
    phi_v2 = nsa_phi_v2.astype(BF16)

    xp = x_prompt.reshape(n * T, D)
    xs = x_sample.reshape(nb, D)
    outs = {k: [] for k in ("hg_p", "hg_s", "dk_p", "dv_p", "dk_s", "dv_s", "ck_p", "cv_p", "sk_p", "sv_p",
                            "wk_p", "wv_p", "ck_s", "cv_s", "sk_s", "sv_s", "wk_s", "wv_s")}
    for layer in range(depth):
        li = layer // 2
        if layer % 2 == 0:
            w_in, w_out = w_in_even_b[li], w_out_even_b[li]
            lb, on = lower_bounds[li], hgrn_out_norm[li]
            qn, kn = dil_q_norm[li], dil_k_norm[li]
            pp = rms_matmul(xp, norm_mix[layer], w_in, TM, TN)
            ps = rms_matmul(xs, norm_mix[layer], w_in, nb, TN)
            oa_p, st_p = hgrn_prompt(pp, lb, on, n, T, a_heads)
            ob_p, dk, dv = dil_prompt(pp, qn, kn, n, T, b_heads, 4 * a_w)
            oa_s, st_s = hgrn_sample(ps, lb, on, state_hgrn, li, a_heads)
            ob_s, dks, dvs = dil_sample(ps, qn, kn, cache_dil_k, cache_dil_v, li, b_heads, 4 * a_w)
            w_halves = [w_out[:a_w], w_out[a_w:]]
            xp = proj_residual([oa_p, ob_p], w_halves, xp, TM)
            xs = proj_residual([oa_s, ob_s], w_halves, xs, nb)
            outs["hg_p"].append(st_p); outs["hg_s"].append(st_s)
            outs["dk_p"].append(dk.reshape(n, -1, b_heads, LANES)); outs["dv_p"].append(dv.reshape(n, -1, b_heads, LANES))
            outs["dk_s"].append(dks.reshape(nb, -1, b_heads, LANES)); outs["dv_s"].append(dvs.reshape(nb, -1, b_heads, LANES))
        else:
            w_in, w_out = w_in_odd_p[li], w_out_odd_b[li]
            knorm = nsa_k_norm[li]
            cmp_w = (nsa_pe_k[li], nsa_pe_v[li], phi_k1[li], phi_k2[li], phi_v1[li], phi_v2[li])
            pp = rms_matmul(xp, norm_mix[layer], w_in, TM, TN)
            ps = rms_matmul(xs, norm_mix[layer], w_in, nb, TN)
            kcmp, vcmp, ksb, ksf, vst, kwb, kwf, vwt = nsa_prep(pp, knorm, *cmp_w, n, T, hk)
            o_p = nsa_prompt(pp, nsa_q_norm[li], kcmp, vcmp, ksb, vst, kwb, vwt, n, T, hk, gate_col)
            kcs, vcs = cmp_pages(page_table, cache_cmp_k, cache_cmp_v, li, knorm, *cmp_w)
            q8, ocmp, sel, ksn = nsa_sample_select(ps, nsa_q_norm[li], knorm, *cmp_w, kcs, vcs, hk, past_len)
            oslc = nsa_sample_selected(sel, page_table, q8, ksn, ps, cq + 3 * ckv, cache_slc_k, cache_slc_v, li, hk)
            o_s, wks, wvs = nsa_sample_window(q8, ocmp, oslc, ps, cq + 4 * ckv, gate_col, knorm,
                                              cache_win_k, cache_win_v, li, hk)
            xp = proj_residual([o_p], [w_out], xp, TM)
            xs = proj_residual([o_s], [w_out], xs, nb)
            kvp = lambda k: pp[:, cq + k * ckv:cq + (k + 1) * ckv].reshape(n, T, hk, LANES)
            kvs = lambda k: ps[:, cq + k * ckv:cq + (k + 1) * ckv].reshape(nb, 1, hk, LANES)
            ww = kwf.shape[1]
            outs["ck_p"].append(kvp(0)); outs["cv_p"].append(kvp(1))
            outs["sk_p"].append(ksf.reshape(n, T, hk, LANES)); outs["sv_p"].append(kvp(3))
            outs["wk_p"].append(kwf.reshape(n, ww, hk, LANES)); outs["wv_p"].append(kvp(5)[:, T - ww:])
            outs["ck_s"].append(kvs(0)); outs["cv_s"].append(kvs(1))
            outs["sk_s"].append(ksn[:, :, 0, :].reshape(nb, 1, hk, LANES)); outs["sv_s"].append(kvs(3))
            outs["wk_s"].append(wks.reshape(nb, -1, hk, LANES)); outs["wv_s"].append(wvs.reshape(nb, -1, hk, LANES))
        xp = mlp_residual(xp, norm_mlp[layer], w_up_b[layer], w_down_b[layer], TM, 512)
        xs = mlp_residual(xs, norm_mlp[layer], w_up_b[layer], w_down_b[layer], nb, 512)
    st = lambda k: jnp.stack(outs[k], axis=1)
    return (xp.reshape(n, T, D), xs.reshape(nb, 1, D),
            st("hg_p"), st("hg_s"), st("dk_p"), st("dv_p"), st("dk_s"), st("dv_s"),
            st("ck_p"), st("cv_p"), st("sk_p"), st("sv_p"), st("wk_p"), st("wv_p"),
            st("ck_s"), st("cv_s"), st("sk_s"), st("sv_s"), st("wk_s"), st("wv_s"))
```

```python
import functools

import numpy as np
import jax
import jax.numpy as jnp
from jax import lax
from jax.experimental import pallas as pl
from jax.experimental.pallas import tpu as pltpu

F32 = jnp.float32
BF16 = jnp.bfloat16
HIGHEST = lax.Precision.HIGHEST

HEAD_DIM = 128
LANES = 128
RMS_EPS = 1e-6
NEG_INF = -1e30
FORCE_SCORE = 1e6
SCALE = HEAD_DIM ** -0.5
HGRN_CHUNK = 64
HGRN_SUB = 8
DIL_PATTERNS = ((128, 1), (512, 4), (2048, 16))
DIL_MAX_WINDOW = 2048
CMP_LEN = 32
SLC_BLOCK = 64
SLC_TOPN = 16
WIN = 512
C_GROUP = 4
SLC_CHUNK = 512
VMEM_LIMIT = 56 * 1024 * 1024


def _params(*sem):
    return pltpu.CompilerParams(dimension_semantics=sem, vmem_limit_bytes=VMEM_LIMIT)


def _rms(x, w):
    return x * lax.rsqrt(jnp.mean(x * x, axis=-1, keepdims=True) + RMS_EPS) * w


def _sigmoid(x):
    return 1.0 / (1.0 + jnp.exp(-x))


def _dot_nt(a, b):
    return lax.dot_general(a, b, (((1,), (1,)), ((), ())), preferred_element_type=F32)


def _dot(a, b, precision=None):
    return jnp.dot(a, b, preferred_element_type=F32, precision=precision)


def _rms_mm_body(x_ref, g_ref, w_ref, o_ref, h_ref):
    @pl.when(pl.program_id(1) == 0)
    def _():
        h_ref[...] = _rms(x_ref[...], g_ref[...]).astype(BF16)

    o_ref[...] = _dot(h_ref[...], w_ref[...])


def rms_matmul(x, g, w, tm, tn):
    M, D = x.shape
    N = w.shape[1]
    return pl.pallas_call(
        _rms_mm_body,
        grid=(M // tm, N // tn),
        in_specs=[pl.BlockSpec((tm, D), lambda i, j: (i, 0)),
                  pl.BlockSpec((1, D), lambda i, j: (0, 0)),
                  pl.BlockSpec((D, tn), lambda i, j: (0, j))],
        out_specs=pl.BlockSpec((tm, tn), lambda i, j: (i, j)),
        out_shape=jax.ShapeDtypeStruct((M, N), F32),
        scratch_shapes=[pltpu.VMEM((tm, D), BF16)],
        compiler_params=_params("parallel", "arbitrary"),
        name="rms_matmul",
    )(x, g.reshape(1, D), w)


def _proj_res_body(*refs, n_in):
    res_ref, o_ref = refs[2 * n_in], refs[2 * n_in + 1]
    acc = res_ref[...]
    for a_ref, w_ref in zip(refs[:n_in], refs[n_in:2 * n_in]):
        acc = acc + _dot(a_ref[...].astype(BF16), w_ref[...])
    o_ref[...] = acc


def proj_residual(lhs, ws, res, tm):
    M, D = res.shape
    n = len(lhs)
    in_specs = [pl.BlockSpec((tm, a.shape[1]), lambda i: (i, 0)) for a in lhs]
    in_specs += [pl.BlockSpec(w.shape, lambda i: (0, 0)) for w in ws]
    in_specs += [pl.BlockSpec((tm, D), lambda i: (i, 0))]
    return pl.pallas_call(
        functools.partial(_proj_res_body, n_in=n),
        grid=(M // tm,),
        in_specs=in_specs,
        out_specs=pl.BlockSpec((tm, D), lambda i: (i, 0)),
        out_shape=jax.ShapeDtypeStruct((M, D), F32),
        compiler_params=_params("parallel"),
        name="proj_residual",
    )(*lhs, *ws, res)


def _mlp_body(x_ref, g_ref, wu_ref, wd_ref, o_ref, h_ref):
    @pl.when(pl.program_id(1) == 0)
    def _():
        x = x_ref[...]
        h_ref[...] = _rms(x, g_ref[...]).astype(BF16)
        o_ref[...] = x

    u = jnp.maximum(_dot(h_ref[...], wu_ref[...]), 0.0)
    o_ref[...] += _dot((u * u).astype(BF16), wd_ref[...])


def mlp_residual(x, g, wu, wd, tm, tf):
    M, D = x.shape
    Fd = wu.shape[1]
    return pl.pallas_call(
        _mlp_body,
        grid=(M // tm, Fd // tf),
        in_specs=[pl.BlockSpec((tm, D), lambda i, j: (i, 0)),
                  pl.BlockSpec((1, D), lambda i, j: (0, 0)),
                  pl.BlockSpec((D, tf), lambda i, j: (0, j)),
                  pl.BlockSpec((tf, D), lambda i, j: (j, 0))],
        out_specs=pl.BlockSpec((tm, D), lambda i, j: (i, 0)),
        out_shape=jax.ShapeDtypeStruct((M, D), F32),
        scratch_shapes=[pltpu.VMEM((tm, D), BF16)],
        compiler_params=_params("parallel", "arbitrary"),
        name="mlp_residual",
    )(x, g.reshape(1, D), wu, wd)


def _hgrn_gates(z, lb):
    log_sig = jnp.minimum(z, 0.0) - jnp.log1p(jnp.exp(-jnp.abs(z)))
    a = jnp.log(lb)
    b = jnp.log1p(-lb) + log_sig
    log_f = jnp.maximum(a, b) + jnp.log1p(jnp.exp(-jnp.abs(a - b)))
    series = -(log_f + 0.5 * log_f * log_f + log_f * log_f * log_f * (1.0 / 6.0))
    k = jnp.where(log_f > -0.01, series, 1.0 - jnp.exp(log_f))
    return log_f, k


def _hgrn_out(o, on, g_raw):
    return _rms(o, on) * (g_raw * _sigmoid(g_raw))


def _split3(x):
    hi = x.astype(BF16)
    r1 = x - hi.astype(F32)
    mid = r1.astype(BF16)
    lo = (r1 - mid.astype(F32)).astype(BF16)
    return jnp.concatenate([hi, mid, lo], axis=1)


def _hgrn_body(q_ref, f_ref, i_ref, g_ref, lb_ref, on_ref, o_ref, s_ref, st_ref, *, tb, nh):
    C, SC = HGRN_CHUNK, HGRN_SUB
    t = pl.program_id(2)
    heads = range(nh)

    @pl.when(t == 0)
    def _():
        st_ref[...] = jnp.zeros_like(st_ref)

    on = on_ref[...]
    r_i = lax.broadcasted_iota(jnp.int32, (C, C), 0)
    c_i = lax.broadcasted_iota(jnp.int32, (C, C), 1)
    tril = (r_i >= c_i).astype(BF16)
    row = lax.broadcasted_iota(jnp.int32, (C, LANES), 0)
    levels = []
    bs = C // 2
    while bs >= SC:
        levels.append((bs, (row // bs) % 2 == 1, ((r_i // bs) % 2 == 1) & (c_i // bs == r_i // bs - 1)))
        bs //= 2
    lane_c = lax.broadcasted_iota(jnp.int32, (SC, C), 1)
    sub_c = lax.broadcasted_iota(jnp.int32, (SC, C), 0)

    def chunk(c, carry):
        rows = pl.ds(pl.multiple_of(c * C, C), C)
        sl = [slice(h * LANES, (h + 1) * LANES) for h in heads]
        qr = [q_ref[rows, sl[h]] for h in heads]
        q = [x * _sigmoid(x) for x in qr]
        gates = [_hgrn_gates(f_ref[rows, sl[h]], lb_ref[h]) for h in heads]
        log_f, kk = [g[0] for g in gates], [g[1] for g in gates]
        v = [i_ref[rows, sl[h]] for h in heads]
        vb = [x.astype(BF16) for x in v]
        g3 = [_dot(tril, _split3(log_f[h])) for h in heads]
        G = [x[:, 0:LANES] + x[:, LANES:2 * LANES] + x[:, 2 * LANES:3 * LANES] for x in g3]
        st = [st_ref[h] for h in heads]
        inter = [_dot_nt((q[h] * jnp.exp(G[h])).astype(BF16), st[h].astype(BF16)) for h in heads]
        a_off = [jnp.zeros((C, C), F32) for _ in heads]
        for bs, odd, blk in levels:
            refs = [jnp.concatenate([jnp.broadcast_to(G[h][p + bs - 1:p + bs], (2 * bs, LANES))
                                     for p in range(0, C, 2 * bs)], axis=0) for h in heads]
            d = [G[h] - refs[h] for h in heads]
            qp = [(q[h] * jnp.exp(jnp.where(odd, d[h], NEG_INF))).astype(BF16) for h in heads]
            kp = [(kk[h] * jnp.exp(jnp.where(odd, NEG_INF, -d[h]))).astype(BF16) for h in heads]
            a_off = [a_off[h] + jnp.where(blk, _dot_nt(qp[h], kp[h]), 0.0) for h in heads]
        a_rows = [[] for _ in heads]
        for I in range(C // SC):
            lo = I * SC
            for h in heads:
                GI, qI = G[h][lo:lo + SC], q[h][lo:lo + SC]
                dg = jnp.zeros((SC, C), F32)
                for j in range(SC):
                    e = jnp.exp(jnp.minimum(GI - GI[j:j + 1], 0.0))
                    colv = jnp.sum(qI * e * kk[h][lo + j:lo + j + 1], axis=1, keepdims=True)
                    dg = jnp.where(lane_c == lo + j, colv, dg)
                a_rows[h].append(jnp.where(lane_c <= lo + sub_c, dg, 0.0))
        a = [(a_off[h] + jnp.concatenate(a_rows[h], axis=0)).astype(BF16) for h in heads]
        o = [inter[h] + _dot(a[h], vb[h]) for h in heads]
        Gl = [G[h][C - 1:C] for h in heads]
        kd = [(kk[h] * jnp.exp(Gl[h] - G[h])).astype(BF16) for h in heads]
        upd = [_dot(v[h].T.astype(BF16), kd[h]) for h in heads]
        for h in heads:
            st_ref[h] = jnp.exp(Gl[h]) * st[h] + upd[h]
            o_ref[rows, sl[h]] = _hgrn_out(o[h], on, g_ref[rows, sl[h]]).astype(o_ref.dtype)
        return carry

    lax.fori_loop(0, tb // C, chunk, 0)

    @pl.when(t == pl.num_programs(2) - 1)
    def _():
        for h in heads:
            s_ref[0, h] = st_ref[h].T


def hgrn_prompt(proj, lb, on, n, T, heads, tb=256, nh=4):
    nt = T // tb
    hg = heads // nh
    col = lambda k: pl.BlockSpec((tb, nh * LANES), lambda b, h, t, k=k: (b * nt + t, k * hg + h))
    return pl.pallas_call(
        functools.partial(_hgrn_body, tb=tb, nh=nh),
        grid=(n, hg, nt),
        in_specs=[col(0), col(1), col(2), col(3),
                  pl.BlockSpec((nh, 1, LANES), lambda b, h, t: (h, 0, 0)),
                  pl.BlockSpec((1, LANES), lambda b, h, t: (0, 0))],
        out_specs=[pl.BlockSpec((tb, nh * LANES), lambda b, h, t: (b * nt + t, h)),
                   pl.BlockSpec((1, nh, LANES, LANES), lambda b, h, t: (b, h, 0, 0))],
        out_shape=[jax.ShapeDtypeStruct((n * T, heads * LANES), BF16),
                   jax.ShapeDtypeStruct((n, heads, LANES, LANES), F32)],
        scratch_shapes=[pltpu.VMEM((nh, LANES, LANES), F32)],
        compiler_params=_params("parallel", "parallel", "arbitrary"),
        name="hgrn_prompt",
    )(proj, proj, proj, proj, lb.reshape(heads, 1, LANES), on.reshape(1, LANES))


def _col(eye, row):
    return jnp.sum(eye * row, axis=1, keepdims=True)


def _hgrn_s_body(q_ref, f_ref, i_ref, g_ref, lb_ref, on_ref, s_ref, o_ref, so_ref, *, nb):
    qr = q_ref[...]
    q = qr * _sigmoid(qr)
    log_f, kk = _hgrn_gates(f_ref[...], lb_ref[0])
    v = i_ref[...]
    f = jnp.exp(log_f)
    eye = (lax.broadcasted_iota(jnp.int32, (LANES, LANES), 0)
           == lax.broadcasted_iota(jnp.int32, (LANES, LANES), 1)).astype(F32)
    qf = (q * f).astype(BF16)
    a = jnp.sum(q * kk, axis=1, keepdims=True)
    rows = []
    for b in range(nb):
        S = s_ref[b, 0, 0]
        so_ref[b, 0] = _col(eye, f[b:b + 1]) * S + _col(eye, kk[b:b + 1]) * v[b:b + 1]
        rows.append(_dot(qf, S.astype(BF16))[b:b + 1])
    o = jnp.concatenate(rows, axis=0) + a * v
    o_ref[...] = _hgrn_out(o, on_ref[...], g_ref[...])


def hgrn_sample(proj, lb, on, state, li, heads):
    nb = proj.shape[0]
    col = lambda k: pl.BlockSpec((nb, LANES), lambda h, k=k: (0, k * heads + h))
    return pl.pallas_call(
        functools.partial(_hgrn_s_body, nb=nb),
        grid=(heads,),
        in_specs=[col(0), col(1), col(2), col(3),
                  pl.BlockSpec((1, 1, LANES), lambda h: (h, 0, 0)),
                  pl.BlockSpec((1, LANES), lambda h: (0, 0)),
                  pl.BlockSpec((nb, 1, 1, LANES, LANES), lambda h: (0, li, h, 0, 0))],
        out_specs=[pl.BlockSpec((nb, LANES), lambda h: (0, h)),
                   pl.BlockSpec((nb, 1, LANES, LANES), lambda h: (0, h, 0, 0))],
        out_shape=[jax.ShapeDtypeStruct((nb, heads * LANES), F32),
                   jax.ShapeDtypeStruct((nb, heads, LANES, LANES), F32)],
        compiler_params=_params("parallel"),
        name="hgrn_sample",
    )(proj, proj, proj, proj, lb.reshape(heads, 1, LANES), on.reshape(1, LANES), state)


def _dil_multiplicity(delta):
    c = np.zeros(delta.shape, np.float32)
    for window, dil in DIL_PATTERNS:
        c += ((delta >= 0) & (delta <= window) & (delta % dil == 0)).astype(np.float32)
    return c


def _dil_body(q_ref, k_ref, v_ref, qn_ref, kn_ref, c_ref, o_ref, ko_ref, vo_ref, kn_s, vb_s, *, T, tq, W, pad):
    qi = pl.program_id(2)
    span = pad + tq

    @pl.when(qi == 0)
    def _():
        kw = kn_ref[...]
        kn_s[0:pad, :] = jnp.zeros((pad, LANES), BF16)
        vb_s[0:pad, :] = jnp.zeros((pad, LANES), BF16)

        def norm(c, carry):
            rows = pl.ds(pl.multiple_of(c * 512, 512), 512)
            dst = pl.ds(pl.multiple_of(pad + c * 512, 512), 512)
            kn_s[dst, :] = _rms(k_ref[rows, :], kw).astype(BF16)
            vb_s[dst, :] = v_ref[rows, :].astype(BF16)
            return carry

        lax.fori_loop(0, T // 512, norm, 0)
        ko_ref[0] = _rms(k_ref[T - W:, :], kw)
        vo_ref[0] = v_ref[T - W:, :]

    q = (_rms(q_ref[...], qn_ref[...]) * SCALE).astype(BF16)
    rows = pl.ds(pl.multiple_of(qi * tq, tq), span)
    sc = _dot_nt(q, kn_s[rows, :])
    c = c_ref[...]
    exists = lax.broadcasted_iota(jnp.int32, (tq, span), 1) >= pad - qi * tq
    sc = jnp.where((c > 0.0) & exists, sc, NEG_INF)
    m = jnp.max(sc, axis=1, keepdims=True)
    p = c * jnp.exp(sc - m)
    l = jnp.sum(p, axis=1, keepdims=True)
    o_ref[...] = (_dot(p.astype(BF16), vb_s[rows, :]) / l).astype(o_ref.dtype)


def dil_prompt(proj, qn, kn, n, T, heads, col0, tq=256):
    W = min(DIL_MAX_WINDOW, T)
    pad = DIL_MAX_WINDOW
    nt = T // tq
    ctab = jnp.asarray(_dil_multiplicity(np.arange(tq)[:, None] + pad - np.arange(pad + tq)[None, :]))
    cb = col0 // LANES
    return pl.pallas_call(
        functools.partial(_dil_body, T=T, tq=tq, W=W, pad=pad),
        grid=(n, heads, nt),
        in_specs=[pl.BlockSpec((tq, LANES), lambda b, h, t: (b * nt + t, cb + h)),
                  pl.BlockSpec((T, LANES), lambda b, h, t: (b, cb + heads + h)),
                  pl.BlockSpec((T, LANES), lambda b, h, t: (b, cb + 2 * heads + h)),
                  pl.BlockSpec((1, LANES), lambda b, h, t: (0, 0)),
                  pl.BlockSpec((1, LANES), lambda b, h, t: (0, 0)),
                  pl.BlockSpec((tq, pad + tq), lambda b, h, t: (0, 0))],
        out_specs=[pl.BlockSpec((tq, LANES), lambda b, h, t: (b * nt + t, h)),
                   pl.BlockSpec((1, W, LANES), lambda b, h, t: (b, 0, h)),
                   pl.BlockSpec((1, W, LANES), lambda b, h, t: (b, 0, h))],
        out_shape=[jax.ShapeDtypeStruct((n * T, heads * LANES), BF16),
                   jax.ShapeDtypeStruct((n, W, heads * LANES), F32),
                   jax.ShapeDtypeStruct((n, W, heads * LANES), F32)],
        scratch_shapes=[pltpu.VMEM((T + pad, LANES), BF16), pltpu.VMEM((T + pad, LANES), BF16)],
        compiler_params=_params("parallel", "parallel", "arbitrary"),
        name="dil_prompt",
    )(proj, proj, proj, qn.reshape(1, LANES), kn.reshape(1, LANES), ctab)


def _shift_in(buf, new_row):
    n = buf.shape[0]
    rolled = pltpu.roll(buf, n - 1, 0)
    return jnp.where(lax.broadcasted_iota(jnp.int32, buf.shape, 0) == n - 1, new_row, rolled)


def _dil_s_body(q_ref, k_ref, v_ref, qn_ref, kn_ref, c_ref, ck_ref, cv_ref, o_ref, ko_ref, vo_ref, *, c_new):
    b = pl.program_id(1)
    row = pl.ds(b, 1)
    q = _rms(q_ref[row, :], qn_ref[...]) * SCALE
    kn = _rms(k_ref[row, :], kn_ref[...])
    v = v_ref[row, :]
    kb, vb = ck_ref[0, 0], cv_ref[0, 0]
    q8 = jnp.broadcast_to(q, (8, LANES)).astype(BF16)
    s = _dot_nt(q8, kb.astype(BF16))
    c = c_ref[...]
    s = jnp.where(c > 0.0, s, NEG_INF)
    s_new = jnp.sum(q * kn, axis=1, keepdims=True)
    m = jnp.maximum(jnp.max(s, axis=1, keepdims=True), s_new)
    p = c * jnp.exp(s - m)
    p_new = c_new * jnp.exp(s_new - m)
    l = jnp.sum(p, axis=1, keepdims=True) + p_new
    o = (_dot(p.astype(BF16), vb.astype(BF16)) + p_new * v) / l
    o_ref[row, :] = o[0:1]
    ko_ref[0] = _shift_in(kb, kn)
    vo_ref[0] = _shift_in(vb, v)


def dil_sample(proj, qn, kn, cache_k, cache_v, li, heads, col0):
    nb = proj.shape[0]
    Wb = cache_k.shape[2]
    ck = cache_k.reshape(nb, cache_k.shape[1], Wb, heads * LANES)
    cv = cache_v.reshape(nb, cache_v.shape[1], Wb, heads * LANES)
    delta = Wb - np.arange(Wb)
    c_buf = jnp.asarray(_dil_multiplicity(delta)[None, :])
    c_new = float(_dil_multiplicity(np.zeros((1,), np.int64))[0])
    cb = col0 // LANES
    return pl.pallas_call(
        functools.partial(_dil_s_body, c_new=c_new),
        grid=(heads, nb),
        in_specs=[pl.BlockSpec((nb, LANES), lambda h, b: (0, cb + h)),
                  pl.BlockSpec((nb, LANES), lambda h, b: (0, cb + heads + h)),
                  pl.BlockSpec((nb, LANES), lambda h, b: (0, cb + 2 * heads + h)),
                  pl.BlockSpec((1, LANES), lambda h, b: (0, 0)),
                  pl.BlockSpec((1, LANES), lambda h, b: (0, 0)),
                  pl.BlockSpec((1, Wb), lambda h, b: (0, 0)),
                  pl.BlockSpec((1, 1, Wb, LANES), lambda h, b: (b, li, 0, h)),
                  pl.BlockSpec((1, 1, Wb, LANES), lambda h, b: (b, li, 0, h))],
        out_specs=[pl.BlockSpec((nb, LANES), lambda h, b: (0, h)),
                   pl.BlockSpec((1, Wb, LANES), lambda h, b: (b, 0, h)),
                   pl.BlockSpec((1, Wb, LANES), lambda h, b: (b, 0, h))],
        out_shape=[jax.ShapeDtypeStruct((nb, heads * LANES), F32),
                   jax.ShapeDtypeStruct((nb, Wb, heads * LANES), F32),
                   jax.ShapeDtypeStruct((nb, Wb, heads * LANES), F32)],
        compiler_params=_params("parallel", "arbitrary"),
        name="dil_sample",
    )(proj, proj, proj, qn.reshape(1, LANES), kn.reshape(1, LANES), c_buf, ck, cv)


def _gelu(x):
    return 0.5 * x * (1.0 + jnp.tanh(0.7978845608028654 * (x + 0.044715 * x * x * x)))


def _compress_rows(load_j, pe_ref, w1_ref, w2_ref, rows):
    acc = jnp.zeros((rows, LANES), F32)
    for j in range(CMP_LEN):
        acc = acc + _dot((load_j(j) + pe_ref[j:j + 1, :]).astype(BF16), w1_ref[j])
    return _dot(_gelu(acc).astype(BF16), w2_ref[...])


def _nsa_prep_body(kc_ref, vc_ref, ks_ref, vs_ref, kw_ref, vw_ref, knorm_ref, pek_ref, pev_ref,
                   w1k_ref, w2k_ref, w1v_ref, w2v_ref,
                   kcmp_ref, vcmp_ref, ksb_ref, ksf_ref, vst_ref, kwb_ref, kwf_ref, vwt_ref, *, T):
    nblk = T // CMP_LEN
    kcmp = _compress_rows(lambda j: kc_ref[pl.ds(j, nblk, stride=CMP_LEN), :], pek_ref, w1k_ref, w2k_ref, nblk)
    kcmp_ref[0, 0] = _rms(kcmp, knorm_ref[0:1, :])
    vcmp_ref[0, 0] = _compress_rows(lambda j: vc_ref[pl.ds(j, nblk, stride=CMP_LEN), :], pev_ref, w1v_ref, w2v_ref, nblk)
    ks_w, kw_w = knorm_ref[1:2, :], knorm_ref[2:3, :]

    kwb_ref[0, 0, 0:WIN, :] = jnp.zeros((WIN, LANES), BF16)
    for i in range(WIN // LANES):
        vwt_ref[0, 0, i] = jnp.zeros((LANES, LANES), BF16)

    def tile(c, carry):
        rows = pl.ds(pl.multiple_of(c * LANES, LANES), LANES)
        ksn = _rms(ks_ref[rows, :], ks_w)
        ksf_ref[0, rows, :] = ksn
        ksb_ref[0, 0, rows, :] = ksn.astype(BF16)
        kwb_ref[0, 0, pl.ds(pl.multiple_of(WIN + c * LANES, LANES), LANES), :] = _rms(kw_ref[rows, :], kw_w).astype(BF16)
        vwt_ref[0, 0, WIN // LANES + c] = vw_ref[rows, :].T.astype(BF16)
        return carry

    lax.fori_loop(0, T // LANES, tile, 0)

    def chunk(c, carry):
        rows = pl.ds(pl.multiple_of(c * SLC_CHUNK, SLC_CHUNK), SLC_CHUNK)
        vst_ref[0, 0, c] = vs_ref[rows, :].T.astype(BF16)
        return carry

    lax.fori_loop(0, T // SLC_CHUNK, chunk, 0)
    ww = min(WIN, T)
    kwf_ref[0] = _rms(kw_ref[T - ww:, :], kw_w)


def nsa_prep(proj, knorm, pe_k, pe_v, w1k, w2k, w1v, w2v, n, T, hk):
    cb = (hk * C_GROUP * HEAD_DIM) // LANES
    nblk = T // CMP_LEN
    nt = T // LANES
    ww = min(WIN, T)
    col = lambda k: pl.BlockSpec((T, LANES), lambda b, h, k=k: (b, cb + k * hk + h))
    full = lambda a: pl.BlockSpec(a.shape, lambda b, h: (0,) * a.ndim)
    per = lambda *s: pl.BlockSpec((1, 1) + s, lambda b, h: (b, h) + (0,) * len(s))
    return pl.pallas_call(
        functools.partial(_nsa_prep_body, T=T),
        grid=(n, hk),
        in_specs=[col(0), col(1), col(2), col(3), col(4), col(5),
                  full(knorm), full(pe_k), full(pe_v), full(w1k), full(w2k), full(w1v), full(w2v)],
        out_specs=[per(nblk, LANES), per(nblk, LANES), per(T, LANES),
                   pl.BlockSpec((1, T, LANES), lambda b, h: (b, 0, h)),
                   per(T // SLC_CHUNK, LANES, SLC_CHUNK), per(T + WIN, LANES),
                   pl.BlockSpec((1, ww, LANES), lambda b, h: (b, 0, h)),
                   per(nt + WIN // LANES, LANES, LANES)],
        out_shape=[jax.ShapeDtypeStruct((n, hk, nblk, LANES), F32),
                   jax.ShapeDtypeStruct((n, hk, nblk, LANES), F32),
                   jax.ShapeDtypeStruct((n, hk, T, LANES), BF16),
                   jax.ShapeDtypeStruct((n, T, hk * LANES), F32),
                   jax.ShapeDtypeStruct((n, hk, T // SLC_CHUNK, LANES, SLC_CHUNK), BF16),
                   jax.ShapeDtypeStruct((n, hk, T + WIN, LANES), BF16),
                   jax.ShapeDtypeStruct((n, ww, hk * LANES), F32),
                   jax.ShapeDtypeStruct((n, hk, nt + WIN // LANES, LANES, LANES), BF16)],
        compiler_params=_params("parallel", "parallel"),
        name="nsa_prep",
    )(proj, proj, proj, proj, proj, proj, knorm, pe_k, pe_v, w1k, w2k, w1v, w2v)


def _tile4(x):
    return jnp.concatenate([x] * C_GROUP, axis=1)


def _nsa_body(q_ref, gate_ref, qn_ref, kcmp_ref, vcmp_ref, ks_ref, vst_ref, kw_ref, vwt_ref, o_ref,
              vct_s, pb_s, sel_s, gt_s, *, tq, nblk, nslc, hk_n):
    G = C_GROUP
    hk = pl.program_id(1)
    qi = pl.program_id(2)
    t0 = qi * tq

    @pl.when(qi == 0)
    def _():
        vct_s[...] = vcmp_ref[0, 0].T.astype(BF16)

    qw = qn_ref[...]
    q4 = jnp.concatenate(
        [(_rms(q_ref[:, g * LANES:(g + 1) * LANES], qw) * SCALE).astype(BF16) for g in range(G)], axis=0)

    st = _dot_nt(kcmp_ref[0, 0].astype(BF16), q4)
    blk = lax.broadcasted_iota(jnp.int32, (nblk, G * tq), 0)
    tpos = t0 + (lax.broadcasted_iota(jnp.int32, (nblk, G * tq), 1) & (tq - 1))
    valid = (blk + 1) * CMP_LEN - 1 <= tpos
    st = jnp.where(valid, st, NEG_INF)
    p = jnp.where(valid, jnp.exp(st - jnp.max(st, axis=0, keepdims=True)), 0.0)
    p = p / jnp.maximum(jnp.sum(p, axis=0, keepdims=True), 1.0)
    o_cmp = _dot(vct_s[...], p.astype(BF16))
    pb = p[:, 0:tq]
    for g in range(1, G):
        pb = pb + p[:, g * tq:(g + 1) * tq]
    pb_s[...] = pb
    ratio = SLC_BLOCK // CMP_LEN
    imp = pb_s[pl.ds(0, nslc, stride=ratio), :]
    for r in range(1, ratio):
        imp = imp + pb_s[pl.ds(r, nslc, stride=ratio), :]

    jb = lax.broadcasted_iota(jnp.int32, (nslc, tq), 0)
    tp = t0 + lax.broadcasted_iota(jnp.int32, (nslc, tq), 1)
    cur = tp // SLC_BLOCK
    forced = (jb == 0) | (jb == cur) | (jb == cur - 1)
    score = jnp.where(jb * SLC_BLOCK <= tp, jnp.where(forced, FORCE_SCORE, imp), -FORCE_SCORE)
    rank = jnp.zeros((nslc, tq), F32)
    for jp in range(nslc):
        row = score[jp:jp + 1, :]
        ahead = (row > score) | ((row == score) & (jb > jp))
        rank = rank + ahead.astype(F32)
    sel_s[...] = jnp.where(rank < float(min(SLC_TOPN, nslc)), 0.0, NEG_INF)

    KC = SLC_CHUNK
    per_chunk = KC // SLC_BLOCK

    def slc_step(kc, carry, causal):
        m, l, acc = carry
        rows = pl.ds(pl.multiple_of(kc * KC, KC), KC)
        s = _dot_nt(ks_ref[0, 0, rows, :], q4)
        bias = jnp.concatenate(
            [jnp.broadcast_to(sel_s[pl.ds(kc * per_chunk + r, 1), :], (SLC_BLOCK, tq)) for r in range(per_chunk)],
            axis=0)
        if causal:
            kpos = kc * KC + lax.broadcasted_iota(jnp.int32, (KC, tq), 0)
            bias = jnp.where(kpos <= t0 + lax.broadcasted_iota(jnp.int32, (KC, tq), 1), bias, NEG_INF)
        s = s + _tile4(bias)
        m_new = jnp.maximum(m, jnp.max(s, axis=0, keepdims=True))
        alpha = jnp.exp(m - m_new)
        pp = jnp.exp(s - m_new)
        l = alpha * l + jnp.sum(pp, axis=0, keepdims=True)
        acc = alpha * acc + _dot(vst_ref[0, 0, kc], pp.astype(BF16))
        return m_new, l, acc

    init = (jnp.full((1, G * tq), NEG_INF, F32), jnp.zeros((1, G * tq), F32), jnp.zeros((LANES, G * tq), F32))
    last = (t0 + tq - 1) // KC
    carry = lax.fori_loop(0, last, functools.partial(slc_step, causal=False), init)
    _, l_s, acc_s = slc_step(last, carry, causal=True)
    o_slc = acc_s / l_s

    wspan = WIN + tq
    wsub = lax.broadcasted_iota(jnp.int32, (wspan, tq), 0)
    dist = lax.broadcasted_iota(jnp.int32, (wspan, tq), 1) + WIN - wsub
    okw = (dist >= 0) & (dist <= WIN) & (wsub >= WIN - t0)
    sw = _dot_nt(kw_ref[0, 0, pl.ds(pl.multiple_of(t0, tq), wspan), :], q4)
    sw = sw + _tile4(jnp.where(okw, 0.0, NEG_INF))
    pw = jnp.exp(sw - jnp.max(sw, axis=0, keepdims=True))
    vw_t = jnp.concatenate([vwt_ref[0, 0, qi * (tq // LANES) + i] for i in range(wspan // LANES)], axis=1)
    o_win = _dot(vw_t, pw.astype(BF16)) / jnp.sum(pw, axis=0, keepdims=True)

    gt_s[...] = _sigmoid(gate_ref[...]).T
    nh = G * hk_n
    for g in range(G):
        sl = slice(g * tq, (g + 1) * tq)
        g0 = gt_s[pl.ds(hk * G + g, 1), :]
        g1 = gt_s[pl.ds(nh + hk * G + g, 1), :]
        g2 = gt_s[pl.ds(2 * nh + hk * G + g, 1), :]
        o = g0 * o_cmp[:, sl] + g1 * o_slc[:, sl] + g2 * o_win[:, sl]
        o_ref[:, g * LANES:(g + 1) * LANES] = o.T.astype(o_ref.dtype)


def nsa_prompt(proj, qn, kcmp, vcmp, ksb, vst, kwb, vwt, n, T, hk, gate_col, tq=128):
    nblk, nslc, nt = T // CMP_LEN, T // SLC_BLOCK, T // tq
    gw = C_GROUP * LANES
    per = lambda *s: pl.BlockSpec((1, 1) + s, lambda b, h, t: (b, h) + (0,) * len(s))
    return pl.pallas_call(
        functools.partial(_nsa_body, tq=tq, nblk=nblk, nslc=nslc, hk_n=hk),
        grid=(n, hk, nt),
        in_specs=[pl.BlockSpec((tq, gw), lambda b, h, t: (b * nt + t, h)),
                  pl.BlockSpec((tq, LANES), lambda b, h, t: (b * nt + t, gate_col // LANES)),
                  pl.BlockSpec((1, LANES), lambda b, h, t: (0, 0)),
                  per(nblk, LANES), per(nblk, LANES), per(T, LANES), per(T // SLC_CHUNK, LANES, SLC_CHUNK),
                  per(T + WIN, LANES), per((T + WIN) // LANES, LANES, LANES)],
        out_specs=pl.BlockSpec((tq, gw), lambda b, h, t: (b * nt + t, h)),
        out_shape=jax.ShapeDtypeStruct((n * T, hk * gw), BF16),
        scratch_shapes=[pltpu.VMEM((LANES, nblk), BF16), pltpu.VMEM((nblk, tq), F32),
                        pltpu.VMEM((nslc, tq), F32), pltpu.VMEM((LANES, tq), F32)],
        compiler_params=_params("parallel", "parallel", "arbitrary"),
        name="nsa_prompt",
    )(proj, proj, qn.reshape(1, LANES), kcmp, vcmp, ksb, vst, kwb, vwt)


def _cmp_pages_body(pt_ref, ck_hbm, cv_hbm, knorm_ref, pek_ref, pev_ref, w1k_ref, w2k_ref, w1v_ref, w2v_ref,
                    ko_ref, vo_ref, kbuf, vbuf, sem, *, li, P, hk, rows_per_page):
    s = pl.program_id(0)
    ns = pl.num_programs(0)

    def copies(step, slot):
        out = []
        for p in range(P):
            page = pt_ref[step * P + p]
            dst = pl.ds(p * rows_per_page, rows_per_page)
            out.append(pltpu.make_async_copy(ck_hbm.at[page, li], kbuf.at[slot, dst], sem.at[0, slot]))
            out.append(pltpu.make_async_copy(cv_hbm.at[page, li], vbuf.at[slot, dst], sem.at[1, slot]))
        return out

    @pl.when(s == 0)
    def _():
        for c in copies(0, 0):
            c.start()

    @pl.when(s + 1 < ns)
    def _():
        for c in copies(s + 1, (s + 1) % 2):
            c.start()

    slot = s % 2
    for c in copies(s, slot):
        c.wait()

    blocks = P * (rows_per_page // hk // CMP_LEN)
    stride = CMP_LEN * hk

    def loader(buf):
        def load_j(j):
            return jnp.concatenate(
                [buf[slot, pl.ds(j * hk + h, blocks, stride=stride), :] for h in range(hk)], axis=0)
        return load_j

    kcmp = _compress_rows(loader(kbuf), pek_ref, w1k_ref, w2k_ref, hk * blocks)
    kcmp = _rms(kcmp, knorm_ref[0:1, :])
    vcmp = _compress_rows(loader(vbuf), pev_ref, w1v_ref, w2v_ref, hk * blocks)
    for h in range(hk):
        ko_ref[0, h] = kcmp[h * blocks:(h + 1) * blocks]
        vo_ref[0, h] = vcmp[h * blocks:(h + 1) * blocks]


def cmp_pages(page_table, cache_k, cache_v, li, knorm, pe_k, pe_v, w1k, w2k, w1v, w2v, P=16):
    nb, n_pages = page_table.shape
    n_pool, n_l, page, hk, dh = cache_k.shape
    rpp = page * hk
    ck = cache_k.reshape(n_pool, n_l, rpp, dh)
    cv = cache_v.reshape(n_pool, n_l, rpp, dh)
    steps_per_b = n_pages // P
    blocks = P * (page // CMP_LEN)
    nblk = n_pages * (page // CMP_LEN)
    full = lambda a: pl.BlockSpec(a.shape, lambda s, pt: (0,) * a.ndim)
    out_spec = pl.BlockSpec((1, hk, blocks, dh), lambda s, pt: (s // steps_per_b, 0, s % steps_per_b, 0))
    grid_spec = pltpu.PrefetchScalarGridSpec(
        num_scalar_prefetch=1,
        grid=(nb * steps_per_b,),
        in_specs=[pl.BlockSpec(memory_space=pl.ANY), pl.BlockSpec(memory_space=pl.ANY),
                  full(knorm), full(pe_k), full(pe_v), full(w1k), full(w2k), full(w1v), full(w2v)],
        out_specs=[out_spec, out_spec],
        scratch_shapes=[pltpu.VMEM((2, P * rpp, dh), F32), pltpu.VMEM((2, P * rpp, dh), F32),
                        pltpu.SemaphoreType.DMA((2, 2))])
    return pl.pallas_call(
        functools.partial(_cmp_pages_body, li=li, P=P, hk=hk, rows_per_page=rpp),
        grid_spec=grid_spec,
        out_shape=[jax.ShapeDtypeStruct((nb, hk, nblk, dh), F32)] * 2,
        compiler_params=_params("arbitrary"),
        name="cmp_pages",
    )(page_table.reshape(-1), ck, cv, knorm, pe_k, pe_v, w1k, w2k, w1v, w2v)


def _nsa_s_select_body(q_ref, kc_ref, vc_ref, ks_ref, qn_ref, knorm_ref, pek_ref, pev_ref,
                       w1k_ref, w2k_ref, w1v_ref, w2v_ref, kcmp_ref, vcmp_ref,
                       qo_ref, ocmp_ref, sel_ref, kso_ref, *, hk, qpos, nblk):
    G = C_GROUP
    row = slice(None)
    qw = qn_ref[...]
    nslc = (nblk + 1 + 1) // 2
    lanes_blk = lax.broadcasted_iota(jnp.int32, (8, nblk), 1)
    valid = (lanes_blk + 1) * CMP_LEN - 1 <= qpos
    valid_x = (jnp.full((8, 1), (nblk + 1) * CMP_LEN - 1, jnp.int32) <= qpos)
    pair = (lax.broadcasted_iota(jnp.int32, (nblk, nblk // 2), 0) // 2
            == lax.broadcasted_iota(jnp.int32, (nblk, nblk // 2), 1)).astype(F32)
    sub8 = lax.broadcasted_iota(jnp.int32, (8, 1), 0)
    pe_rest_k = jnp.zeros((8, LANES), F32)
    pe_rest_v = jnp.zeros((8, LANES), F32)
    for j in range(1, CMP_LEN):
        pe_rest_k = pe_rest_k + _dot(jnp.broadcast_to(pek_ref[j:j + 1, :], (8, LANES)).astype(BF16), w1k_ref[j])
        pe_rest_v = pe_rest_v + _dot(jnp.broadcast_to(pev_ref[j:j + 1, :], (8, LANES)).astype(BF16), w1v_ref[j])
    for h in range(hk):
        qs = [_rms(q_ref[row, (h * G + g) * LANES:(h * G + g + 1) * LANES], qw) * SCALE for g in range(G)]
        q8 = jnp.concatenate(qs + [jnp.zeros((8 - G, LANES), F32)], axis=0)
        qo_ref[0, h] = q8
        kso_ref[0, h] = jnp.broadcast_to(_rms(ks_ref[row, h * LANES:(h + 1) * LANES], knorm_ref[1:2, :]), (8, LANES))
        xk = jnp.broadcast_to(kc_ref[row, h * LANES:(h + 1) * LANES] + pek_ref[0:1, :], (8, LANES))
        xv = jnp.broadcast_to(vc_ref[row, h * LANES:(h + 1) * LANES] + pev_ref[0:1, :], (8, LANES))
        k_x = _dot(_gelu(_dot(xk.astype(BF16), w1k_ref[0]) + pe_rest_k).astype(BF16), w2k_ref[...])
        k_x = _rms(k_x, knorm_ref[0:1, :])
        v_x = _dot(_gelu(_dot(xv.astype(BF16), w1v_ref[0]) + pe_rest_v).astype(BF16), w2v_ref[...])
        s = jnp.where(valid, _dot_nt(q8.astype(BF16), kcmp_ref[0, h].astype(BF16)), NEG_INF)
        s_x = jnp.where(valid_x, jnp.sum(q8 * k_x, axis=1, keepdims=True), NEG_INF)
        m = jnp.maximum(jnp.max(s, axis=1, keepdims=True), s_x)
        p = jnp.where(valid, jnp.exp(s - m), 0.0)
        p_x = jnp.where(valid_x, jnp.exp(s_x - m), 0.0)
        den = jnp.maximum(jnp.sum(p, axis=1, keepdims=True) + p_x, 1.0)
        p = jnp.where(sub8 < G, p / den, 0.0)
        p_x = jnp.where(sub8 < G, p_x / den, 0.0)
        ocmp_ref[0, h] = _dot(p.astype(BF16), vcmp_ref[0, h].astype(BF16)) + p_x * v_x
        pb = jnp.sum(p, axis=0, keepdims=True)
        pb_x = jnp.sum(p_x, axis=0, keepdims=True)
        imp = _dot(jnp.broadcast_to(pb, (8, nblk)), pair, precision=HIGHEST)[0:1]
        lane = lax.broadcasted_iota(jnp.int32, (1, LANES), 1)
        tail = jnp.where(lane == 0, pb_x, -jnp.inf)
        imp = jnp.concatenate([imp, tail], axis=1)
        width = imp.shape[1]
        jb = lax.broadcasted_iota(jnp.int32, (1, width), 1)
        cur = qpos // SLC_BLOCK
        forced = (jb == 0) | (jb == cur) | (jb == cur - 1)
        score = jnp.where(jb * SLC_BLOCK <= qpos, jnp.where(forced, FORCE_SCORE, imp), -FORCE_SCORE)
        score = jnp.where(jb < nslc, score, -jnp.inf)
        sel = jnp.zeros((1, LANES), jnp.int32)
        for r in range(SLC_TOPN):
            best = jnp.max(score, axis=1, keepdims=True)
            idx = jnp.min(jnp.where(score == best, jb, width), axis=1, keepdims=True)
            sel = jnp.where(lane == r, idx, sel)
            score = jnp.where(jb == idx, -jnp.inf, score)
        sel_ref[0, h] = jnp.broadcast_to(sel, (8, LANES))


def nsa_sample_select(proj, qn, knorm, pe_k, pe_v, w1k, w2k, w1v, w2v, kcmp, vcmp, hk, qpos):
    nb = proj.shape[0]
    nblk = kcmp.shape[2]
    heads = hk * C_GROUP
    cq, ckv = heads * LANES, hk * LANES
    full = lambda a: pl.BlockSpec(a.shape, lambda b: (0,) * a.ndim)
    per = lambda *s: pl.BlockSpec((1,) + s, lambda b: (b,) + (0,) * len(s))
    colspec = lambda c0, w: pl.BlockSpec((None, 1, w), lambda b: (b, 0, c0 // w))
    proj = proj.reshape(nb, 1, -1)
    out8 = jax.ShapeDtypeStruct((nb, hk, 8, LANES), F32)
    return pl.pallas_call(
        functools.partial(_nsa_s_select_body, hk=hk, qpos=qpos, nblk=nblk),
        grid=(nb,),
        in_specs=[colspec(0, cq), colspec(cq, ckv), colspec(cq + ckv, ckv), colspec(cq + 2 * ckv, ckv),
                  pl.BlockSpec((1, LANES), lambda b: (0, 0)),
                  full(knorm), full(pe_k), full(pe_v), full(w1k), full(w2k), full(w1v), full(w2v),
                  per(hk, nblk, LANES), per(hk, nblk, LANES)],
        out_specs=[per(hk, 8, LANES)] * 4,
        out_shape=[out8, out8, jax.ShapeDtypeStruct((nb, hk, 8, LANES), jnp.int32), out8],
        compiler_params=_params("arbitrary"),
        name="nsa_sample_select",
    )(proj, proj, proj, proj, qn.reshape(1, LANES), knorm, pe_k, pe_v, w1k, w2k, w1v, w2v, kcmp, vcmp)


def _nsa_s_slc_body(sel_ref, pt_ref, q_ref, ksn_ref, vsn_ref, ks_ref, vs_ref, o_ref, m_s, l_s, acc_s,
                    *, hk, n_past_blocks):
    b, h, r = pl.program_id(0), pl.program_id(1), pl.program_id(2)

    @pl.when(r == 0)
    def _():
        m_s[...] = jnp.full_like(m_s, NEG_INF)
        l_s[...] = jnp.zeros_like(l_s)
        acc_s[...] = jnp.zeros_like(acc_s)

    blk = sel_ref[(b * hk + h) * SLC_TOPN + r]
    is_new = blk >= n_past_blocks
    rows = pl.ds(h, SLC_BLOCK, stride=hk)
    first = lax.broadcasted_iota(jnp.int32, (SLC_BLOCK, LANES), 0) == 0
    k = jnp.where(is_new, jnp.where(first, ksn_ref[0, 0, 0:1, :], 0.0), ks_ref[0, 0, rows, :])
    v = jnp.where(is_new, jnp.where(first, vsn_ref[pl.ds(b, 1), :], 0.0), vs_ref[0, 0, rows, :])
    q8 = q_ref[0, 0]
    s = _dot_nt(q8.astype(BF16), k.astype(BF16))
    ok = jnp.logical_or(jnp.logical_not(is_new), lax.broadcasted_iota(jnp.int32, s.shape, 1) == 0)
    s = jnp.where(ok, s, NEG_INF)
    m_old = m_s[...]
    m_new = jnp.maximum(m_old, jnp.max(s, axis=1, keepdims=True))
    alpha = jnp.exp(m_old - m_new)
    p = jnp.where(ok, jnp.exp(s - m_new[:, 0:1]), 0.0)
    l_s[...] = alpha * l_s[...] + jnp.sum(p, axis=1, keepdims=True)
    acc_s[...] = alpha * acc_s[...] + _dot(p.astype(BF16), v.astype(BF16))
    m_s[...] = m_new

    @pl.when(r == pl.num_programs(2) - 1)
    def _():
        o_ref[0, 0] = acc_s[...] / l_s[...]


def nsa_sample_selected(sel, page_table, q8, ksn, proj, vs_col, cache_k, cache_v, li, hk):
    nb, n_pages = page_table.shape
    n_pool, n_l, page, _, dh = cache_k.shape
    rpp = page * hk
    per_page = page // SLC_BLOCK
    n_past_blocks = n_pages * per_page
    ck = cache_k.reshape(n_pool, n_l, rpp, dh)
    cv = cache_v.reshape(n_pool, n_l, rpp, dh)
    sel_flat = sel[:, :, 0, :SLC_TOPN].reshape(-1)

    def page_map(b, h, r, sel_r, pt_r):
        blk = jnp.minimum(sel_r[(b * hk + h) * SLC_TOPN + r], n_past_blocks - 1)
        return (pt_r[b * n_pages + blk // per_page], li, blk % per_page, 0)

    grid_spec = pltpu.PrefetchScalarGridSpec(
        num_scalar_prefetch=2,
        grid=(nb, hk, SLC_TOPN),
        in_specs=[pl.BlockSpec((1, 1, 8, LANES), lambda b, h, r, s_, p_: (b, h, 0, 0)),
                  pl.BlockSpec((1, 1, 8, LANES), lambda b, h, r, s_, p_: (b, h, 0, 0)),
                  pl.BlockSpec((nb, LANES), lambda b, h, r, s_, p_: (0, vs_col // LANES + h)),
                  pl.BlockSpec((1, 1, SLC_BLOCK * hk, dh), page_map),
                  pl.BlockSpec((1, 1, SLC_BLOCK * hk, dh), page_map)],
        out_specs=pl.BlockSpec((1, 1, 8, LANES), lambda b, h, r, s_, p_: (b, h, 0, 0)),
        scratch_shapes=[pltpu.VMEM((8, LANES), F32)] * 3)
    return pl.pallas_call(
        functools.partial(_nsa_s_slc_body, hk=hk, n_past_blocks=n_past_blocks),
        grid_spec=grid_spec,
        out_shape=jax.ShapeDtypeStruct((nb, hk, 8, LANES), F32),
        compiler_params=_params("arbitrary", "arbitrary", "arbitrary"),
        name="nsa_sample_selected",
    )(sel_flat, page_table.reshape(-1), q8, ksn, proj, ck, cv)


def _nsa_s_win_body(q_ref, ocmp_ref, oslc_ref, kw_ref, vw_ref, gate_ref, knorm_ref, wk_ref, wv_ref,
                    o_ref, wko_ref, wvo_ref, *, hk):
    G = C_GROUP
    row = slice(None)
    eye = (lax.broadcasted_iota(jnp.int32, (LANES, LANES), 0)
           == lax.broadcasted_iota(jnp.int32, (LANES, LANES), 1)).astype(F32)
    gcol = _col(eye, _sigmoid(gate_ref[row, :]))
    nh = G * hk
    for h in range(hk):
        lanes = slice(h * LANES, (h + 1) * LANES)
        q8 = q_ref[0, h]
        kn = _rms(kw_ref[row, lanes], knorm_ref[2:3, :])
        v = vw_ref[row, lanes]
        kb, vb = wk_ref[0, 0, :, lanes], wv_ref[0, 0, :, lanes]
        s = _dot_nt(q8.astype(BF16), kb.astype(BF16))
        s_new = jnp.sum(q8 * kn, axis=1, keepdims=True)
        m = jnp.maximum(jnp.max(s, axis=1, keepdims=True), s_new)
        p = jnp.exp(s - m)
        p_new = jnp.exp(s_new - m)
        l = jnp.sum(p, axis=1, keepdims=True) + p_new
        o_win = (_dot(p.astype(BF16), vb.astype(BF16)) + p_new * v) / l
        g0 = gcol[h * G:h * G + 8]
        g1 = gcol[nh + h * G:nh + h * G + 8]
        g2 = gcol[2 * nh + h * G:2 * nh + h * G + 8]
        o = g0 * ocmp_ref[0, h] + g1 * oslc_ref[0, h] + g2 * o_win
        for g in range(G):
            o_ref[row, (h * G + g) * LANES:(h * G + g + 1) * LANES] = o[g:g + 1]
        wko_ref[0, :, lanes] = _shift_in(kb, kn)
        wvo_ref[0, :, lanes] = _shift_in(vb, v)


def nsa_sample_window(q8, ocmp, oslc, proj, kw_col, gate_col, knorm, win_k, win_v, li, hk):
    nb = proj.shape[0]
    Wb = win_k.shape[2]
    ckv = hk * LANES
    wk = win_k.reshape(nb, win_k.shape[1], Wb, ckv)
    wv = win_v.reshape(nb, win_v.shape[1], Wb, ckv)
    per = lambda *s: pl.BlockSpec((1,) + s, lambda b: (b,) + (0,) * len(s))
    proj = proj.reshape(nb, 1, -1)
    cw = hk * C_GROUP * LANES
    o, wko, wvo = pl.pallas_call(
        functools.partial(_nsa_s_win_body, hk=hk),
        grid=(nb,),
        in_specs=[per(hk, 8, LANES), per(hk, 8, LANES), per(hk, 8, LANES),
                  pl.BlockSpec((None, 1, ckv), lambda b: (b, 0, kw_col // ckv)),
                  pl.BlockSpec((None, 1, ckv), lambda b: (b, 0, kw_col // ckv + 1)),
                  pl.BlockSpec((None, 1, LANES), lambda b: (b, 0, gate_col // LANES)),
                  pl.BlockSpec(knorm.shape, lambda b: (0, 0)),
                  pl.BlockSpec((1, 1, Wb, ckv), lambda b: (b, li, 0, 0)),
                  pl.BlockSpec((1, 1, Wb, ckv), lambda b: (b, li, 0, 0))],
        out_specs=[pl.BlockSpec((None, 1, cw), lambda b: (b, 0, 0)),
                   per(Wb, ckv), per(Wb, ckv)],
        out_shape=[jax.ShapeDtypeStruct((nb, 1, cw), F32),
                   jax.ShapeDtypeStruct((nb, Wb, ckv), F32),
                   jax.ShapeDtypeStruct((nb, Wb, ckv), F32)],
        compiler_params=_params("parallel"),
        name="nsa_sample_window",
    )(q8, ocmp, oslc, proj, proj, proj, knorm, wk, wv)
    return o.reshape(nb, cw), wko, wvo


def _pad_cols(w, mult):
    pad = (-w.shape[-1]) % mult
    return jnp.pad(w, ((0, 0),) * (w.ndim - 1) + ((0, pad),)) if pad else w


def kernel(x_prompt, x_sample, state_hgrn, cache_dil_k, cache_dil_v, cache_cmp_k, cache_cmp_v, cache_slc_k, cache_slc_v, cache_win_k, cache_win_v, page_table, norm_mix, norm_mlp, w_in_even, w_out_even, hgrn_lb_logits, hgrn_out_norm, dil_q_norm, dil_k_norm, w_in_odd, w_out_odd, nsa_q_norm, nsa_k_norm, nsa_pe_k, nsa_pe_v, nsa_phi_k1, nsa_phi_k2, nsa_phi_v1, nsa_phi_v2, w_mlp_up, w_mlp_down):
    n, T, D = x_prompt.shape
    nb = x_sample.shape[0]
    assert x_sample.shape[1] == 1
    depth = norm_mix.shape[0]
    a_heads = hgrn_lb_logits.shape[1] // LANES
    b_heads = cache_dil_k.shape[3]
    hk = cache_win_k.shape[3]
    c_heads = hk * C_GROUP
    past_len = page_table.shape[1] * cache_cmp_k.shape[2]
    a_w = a_heads * LANES
    TN = 896
    TM = 512

    lb_cum = jnp.cumsum(jax.nn.softmax(hgrn_lb_logits.astype(F32), axis=0), axis=0)
    lower_bounds = lb_cum - lb_cum[0:1]

    cq, ckv = c_heads * LANES, hk * LANES
    gate_col = cq + 6 * ckv
    gate_w = w_in_odd[:, :, gate_col:].reshape(-1, D, hk, C_GROUP, 3).transpose(0, 1, 4, 2, 3).reshape(-1, D, 3 * c_heads)
    w_in_odd_p = jnp.concatenate([w_in_odd[:, :, :gate_col], _pad_cols(gate_w, LANES)], axis=-1)
    w_in_odd_p = _pad_cols(w_in_odd_p, TN).astype(BF16)
    w_in_even_b = w_in_even.astype(BF16)
    w_out_even_b = w_out_even.astype(BF16)
    w_out_odd_b = w_out_odd.astype(BF16)
    w_up_b = w_mlp_up.astype(BF16)
    w_down_b = w_mlp_down.astype(BF16)
    phi_k1 = nsa_phi_k1.reshape(-1, CMP_LEN, LANES, LANES).astype(BF16)
    phi_v1 = nsa_phi_v1.reshape(-1, CMP_LEN, LANES, LANES).astype(BF16)
    phi_k2 = nsa_phi_k2.astype(BF16)
    phi_v2 = nsa_phi_v2.astype(BF16)

    xp = x_prompt.reshape(n * T, D)
    xs = x_sample.reshape(nb, D)
    outs = {k: [] for k in ("hg_p", "hg_s", "dk_p", "dv_p", "dk_s", "dv_s", "ck_p", "cv_p", "sk_p", "sv_p",
                            "wk_p", "wv_p", "ck_s", "cv_s", "sk_s", "sv_s", "wk_s", "wv_s")}
    for layer in range(depth):
        li = layer // 2
        if layer % 2 == 0:
            w_in, w_out = w_in_even_b[li], w_out_even_b[li]
            lb, on = lower_bounds[li], hgrn_out_norm[li]
            qn, kn = dil_q_norm[li], dil_k_norm[li]
            pp = rms_matmul(xp, norm_mix[layer], w_in, TM, TN)
            ps = rms_matmul(xs, norm_mix[layer], w_in, nb, TN)
            oa_p, st_p = hgrn_prompt(pp, lb, on, n, T, a_heads)
            ob_p, dk, dv = dil_prompt(pp, qn, kn, n, T, b_heads, 4 * a_w)
            oa_s, st_s = hgrn_sample(ps, lb, on, state_hgrn, li, a_heads)
            ob_s, dks, dvs = dil_sample(ps, qn, kn, cache_dil_k, cache_dil_v, li, b_heads, 4 * a_w)
            w_halves = [w_out[:a_w], w_out[a_w:]]
            xp = proj_residual([oa_p, ob_p], w_halves, xp, TM)
            xs = proj_residual([oa_s, ob_s], w_halves, xs, nb)
            outs["hg_p"].append(st_p); outs["hg_s"].append(st_s)
            outs["dk_p"].append(dk.reshape(n, -1, b_heads, LANES)); outs["dv_p"].append(dv.reshape(n, -1, b_heads, LANES))
            outs["dk_s"].append(dks.reshape(nb, -1, b_heads, LANES)); outs["dv_s"].append(dvs.reshape(nb, -1, b_heads, LANES))
        else:
            w_in, w_out = w_in_odd_p[li], w_out_odd_b[li]
            knorm = nsa_k_norm[li]
            cmp_w = (nsa_pe_k[li], nsa_pe_v[li], phi_k1[li], phi_k2[li], phi_v1[li], phi_v2[li])
            pp = rms_matmul(xp, norm_mix[layer], w_in, TM, TN)
            ps = rms_matmul(xs, norm_mix[layer], w_in, nb, TN)
            kcmp, vcmp, ksb, ksf, vst, kwb, kwf, vwt = nsa_prep(pp, knorm, *cmp_w, n, T, hk)
            o_p = nsa_prompt(pp, nsa_q_norm[li], kcmp, vcmp, ksb, vst, kwb, vwt, n, T, hk, gate_col)
            kcs, vcs = cmp_pages(page_table, cache_cmp_k, cache_cmp_v, li, knorm, *cmp_w)
            q8, ocmp, sel, ksn = nsa_sample_select(ps, nsa_q_norm[li], knorm, *cmp_w, kcs, vcs, hk, past_len)
            oslc = nsa_sample_selected(sel, page_table, q8, ksn, ps, cq + 3 * ckv, cache_slc_k, cache_slc_v, li, hk)
            o_s, wks, wvs = nsa_sample_window(q8, ocmp, oslc, ps, cq + 4 * ckv, gate_col, knorm,
                                              cache_win_k, cache_win_v, li, hk)
            xp = proj_residual([o_p], [w_out], xp, TM)
            xs = proj_residual([o_s], [w_out], xs, nb)
            kvp = lambda k: pp[:, cq + k * ckv:cq + (k + 1) * ckv].reshape(n, T, hk, LANES)
            kvs = lambda k: ps[:, cq + k * ckv:cq + (k + 1) * ckv].reshape(nb, 1, hk, LANES)
            ww = kwf.shape[1]
            outs["ck_p"].append(kvp(0)); outs["cv_p"].append(kvp(1))
            outs["sk_p"].append(ksf.reshape(n, T, hk, LANES)); outs["sv_p"].append(kvp(3))
            outs["wk_p"].append(kwf.reshape(n, ww, hk, LANES)); outs["wv_p"].append(kvp(5)[:, T - ww:])
            outs["ck_s"].append(kvs(0)); outs["cv_s"].append(kvs(1))
            outs["sk_s"].append(ksn[:, :, 0, :].reshape(nb, 1, hk, LANES)); outs["sv_s"].append(kvs(3))
            outs["wk_s"].append(wks.reshape(nb, -1, hk, LANES)); outs["wv_s"].append(wvs.reshape(nb, -1, hk, LANES))
        xp = mlp_residual(xp, norm_mlp[layer], w_up_b[layer], w_down_b[layer], TM, 512)
        xs = mlp_residual(xs, norm_mlp[layer], w_up_b[layer], w_down_b[layer], nb, 512)
    st = lambda k: jnp.stack(outs[k], axis=1)
    return (xp.reshape(n, T, D), xs.reshape(nb, 1, D),
            st("hg_p"), st("hg_s"), st("dk_p"), st("dv_p"), st("dk_s"), st("dv_s"),
            st("ck_p"), st("cv_p"), st("sk_p"), st("sv_p"), st("wk_p"), st("wv_p"),
            st("ck_s"), st("cv_s"), st("sk_s"), st("sv_s"), st("wk_s"), st("wv_s"))
```

```python
import functools

import numpy as np
import jax
import jax.numpy as jnp
from jax import lax
from jax.experimental import pallas as pl
from jax.experimental.pallas import tpu as pltpu

F32 = jnp.float32
BF16 = jnp.bfloat16
HIGHEST = lax.Precision.HIGHEST

HEAD_DIM = 128
LANES = 128
RMS_EPS = 1e-6
NEG_INF = -1e30
FORCE_SCORE = 1e6
SCALE = HEAD_DIM ** -0.5
HGRN_CHUNK = 64
HGRN_SUB = 8
DIL_PATTERNS = ((128, 1), (512, 4), (2048, 16))
DIL_MAX_WINDOW = 2048
CMP_LEN = 32
SLC_BLOCK = 64
SLC_TOPN = 16
WIN = 512
C_GROUP = 4
SLC_CHUNK = 512
VMEM_LIMIT = 56 * 1024 * 1024


def _params(*sem):
    return pltpu.CompilerParams(dimension_semantics=sem, vmem_limit_bytes=VMEM_LIMIT)


def _rms(x, w):
    return x * lax.rsqrt(jnp.mean(x * x, axis=-1, keepdims=True) + RMS_EPS) * w


def _sigmoid(x):
    return 1.0 / (1.0 + jnp.exp(-x))


def _dot_nt(a, b):
    return lax.dot_general(a, b, (((1,), (1,)), ((), ())), preferred_element_type=F32)


def _dot(a, b, precision=None):
    return jnp.dot(a, b, preferred_element_type=F32, precision=precision)


def _rms_mm_body(x_ref, g_ref, w_ref, o_ref, h_ref):
    @pl.when(pl.program_id(1) == 0)
    def _():
        h_ref[...] = _rms(x_ref[...], g_ref[...]).astype(BF16)

    o_ref[...] = _dot(h_ref[...], w_ref[...])


def rms_matmul(x, g, w, tm, tn):
    M, D = x.shape
    N = w.shape[1]
    return pl.pallas_call(
        _rms_mm_body,
        grid=(M // tm, N // tn),
        in_specs=[pl.BlockSpec((tm, D), lambda i, j: (i, 0)),
                  pl.BlockSpec((1, D), lambda i, j: (0, 0)),
                  pl.BlockSpec((D, tn), lambda i, j: (0, j))],
        out_specs=pl.BlockSpec((tm, tn), lambda i, j: (i, j)),
        out_shape=jax.ShapeDtypeStruct((M, N), F32),
        scratch_shapes=[pltpu.VMEM((tm, D), BF16)],
        compiler_params=_params("parallel", "arbitrary"),
        name="rms_matmul",
    )(x, g.reshape(1, D), w)


def _proj_res_body(*refs, n_in):
    res_ref, o_ref = refs[2 * n_in], refs[2 * n_in + 1]
    acc = res_ref[...]
    for a_ref, w_ref in zip(refs[:n_in], refs[n_in:2 * n_in]):
        acc = acc + _dot(a_ref[...].astype(BF16), w_ref[...])
    o_ref[...] = acc


def proj_residual(lhs, ws, res, tm):
    M, D = res.shape
    n = len(lhs)
    in_specs = [pl.BlockSpec((tm, a.shape[1]), lambda i: (i, 0)) for a in lhs]
    in_specs += [pl.BlockSpec(w.shape, lambda i: (0, 0)) for w in ws]
    in_specs += [pl.BlockSpec((tm, D), lambda i: (i, 0))]
    return pl.pallas_call(
        functools.partial(_proj_res_body, n_in=n),
        grid=(M // tm,),
        in_specs=in_specs,
        out_specs=pl.BlockSpec((tm, D), lambda i: (i, 0)),
        out_shape=jax.ShapeDtypeStruct((M, D), F32),
        compiler_params=_params("parallel"),
        name="proj_residual",
    )(*lhs, *ws, res)


def _mlp_body(x_ref, g_ref, wu_ref, wd_ref, o_ref, h_ref):
    @pl.when(pl.program_id(1) == 0)
    def _():
        x = x_ref[...]
        h_ref[...] = _rms(x, g_ref[...]).astype(BF16)
        o_ref[...] = x

    u = jnp.maximum(_dot(h_ref[...], wu_ref[...]), 0.0)
    o_ref[...] += _dot((u * u).astype(BF16), wd_ref[...])


def mlp_residual(x, g, wu, wd, tm, tf):
    M, D = x.shape
    Fd = wu.shape[1]
    return pl.pallas_call(
        _mlp_body,
        grid=(M // tm, Fd // tf),
        in_specs=[pl.BlockSpec((tm, D), lambda i, j: (i, 0)),
                  pl.BlockSpec((1, D), lambda i, j: (0, 0)),
                  pl.BlockSpec((D, tf), lambda i, j: (0, j)),
                  pl.BlockSpec((tf, D), lambda i, j: (j, 0))],
        out_specs=pl.BlockSpec((tm, D), lambda i, j: (i, 0)),
        out_shape=jax.ShapeDtypeStruct((M, D), F32),
        scratch_shapes=[pltpu.VMEM((tm, D), BF16)],
        compiler_params=_params("parallel", "arbitrary"),
        name="mlp_residual",
    )(x, g.reshape(1, D), wu, wd)


def _hgrn_gates(z, lb):
    log_sig = jnp.minimum(z, 0.0) - jnp.log1p(jnp.exp(-jnp.abs(z)))
    a = jnp.log(lb)
    b = jnp.log1p(-lb) + log_sig
    log_f = jnp.maximum(a, b) + jnp.log1p(jnp.exp(-jnp.abs(a - b)))
    series = -(log_f + 0.5 * log_f * log_f + log_f * log_f * log_f * (1.0 / 6.0))
    k = jnp.where(log_f > -0.01, series, 1.0 - jnp.exp(log_f))
    return log_f, k


def _hgrn_out(o, on, g_raw):
    return _rms(o, on) * (g_raw * _sigmoid(g_raw))


def _split3(x):
    hi = x.astype(BF16)
    r1 = x - hi.astype(F32)
    mid = r1.astype(BF16)
    lo = (r1 - mid.astype(F32)).astype(BF16)
    return jnp.concatenate([hi, mid, lo], axis=1)


def _hgrn_body(q_ref, f_ref, i_ref, g_ref, lb_ref, on_ref, o_ref, s_ref, st_ref, *, tb, nh):
    C, SC = HGRN_CHUNK, HGRN_SUB
    t = pl.program_id(2)
    heads = range(nh)

    @pl.when(t == 0)
    def _():
        st_ref[...] = jnp.zeros_like(st_ref)

    on = on_ref[...]
    r_i = lax.broadcasted_iota(jnp.int32, (C, C), 0)
    c_i = lax.broadcasted_iota(jnp.int32, (C, C), 1)
    tril = (r_i >= c_i).astype(BF16)
    row = lax.broadcasted_iota(jnp.int32, (C, LANES), 0)
    levels = []
    bs = C // 2
    while bs >= SC:
        levels.append((bs, (row // bs) % 2 == 1, ((r_i // bs) % 2 == 1) & (c_i // bs == r_i // bs - 1)))
        bs //= 2
    lane_c = lax.broadcasted_iota(jnp.int32, (SC, C), 1)
    sub_c = lax.broadcasted_iota(jnp.int32, (SC, C), 0)

    def chunk(c, carry):
        rows = pl.ds(pl.multiple_of(c * C, C), C)
        sl = [slice(h * LANES, (h + 1) * LANES) for h in heads]
        qr = [q_ref[rows, sl[h]] for h in heads]
        q = [x * _sigmoid(x) for x in qr]
        gates = [_hgrn_gates(f_ref[rows, sl[h]], lb_ref[h]) for h in heads]
        log_f, kk = [g[0] for g in gates], [g[1] for g in gates]
        v = [i_ref[rows, sl[h]] for h in heads]
        vb = [x.astype(BF16) for x in v]
        g3 = [_dot(tril, _split3(log_f[h])) for h in heads]
        G = [x[:, 0:LANES] + x[:, LANES:2 * LANES] + x[:, 2 * LANES:3 * LANES] for x in g3]
        st = [st_ref[h] for h in heads]
        inter = [_dot_nt((q[h] * jnp.exp(G[h])).astype(BF16), st[h].astype(BF16)) for h in heads]
        a_off = [jnp.zeros((C, C), F32) for _ in heads]
        for bs, odd, blk in levels:
            refs = [jnp.concatenate([jnp.broadcast_to(G[h][p + bs - 1:p + bs], (2 * bs, LANES))
                                     for p in range(0, C, 2 * bs)], axis=0) for h in heads]
            d = [G[h] - refs[h] for h in heads]
            qp = [(q[h] * jnp.exp(jnp.where(odd, d[h], NEG_INF))).astype(BF16) for h in heads]
            kp = [(kk[h] * jnp.exp(jnp.where(odd, NEG_INF, -d[h]))).astype(BF16) for h in heads]
            a_off = [a_off[h] + jnp.where(blk, _dot_nt(qp[h], kp[h]), 0.0) for h in heads]
        a_rows = [[] for _ in heads]
        for I in range(C // SC):
            lo = I * SC
            for h in heads:
                GI, qI = G[h][lo:lo + SC], q[h][lo:lo + SC]
                dg = jnp.zeros((SC, C), F32)
                for j in range(SC):
                    e = jnp.exp(jnp.minimum(GI - GI[j:j + 1], 0.0))
                    colv = jnp.sum(qI * e * kk[h][lo + j:lo + j + 1], axis=1, keepdims=True)
                    dg = jnp.where(lane_c == lo + j, colv, dg)
                a_rows[h].append(jnp.where(lane_c <= lo + sub_c, dg, 0.0))
        a = [(a_off[h] + jnp.concatenate(a_rows[h], axis=0)).astype(BF16) for h in heads]
        o = [inter[h] + _dot(a[h], vb[h]) for h in heads]
        Gl = [G[h][C - 1:C] for h in heads]
        kd = [(kk[h] * jnp.exp(Gl[h] - G[h])).astype(BF16) for h in heads]
        upd = [_dot(v[h].T.astype(BF16), kd[h]) for h in heads]
        for h in heads:
            st_ref[h] = jnp.exp(Gl[h]) * st[h] + upd[h]
            o_ref[rows, sl[h]] = _hgrn_out(o[h], on, g_ref[rows, sl[h]]).astype(o_ref.dtype)
        return carry

    lax.fori_loop(0, tb // C, chunk, 0)

    @pl.when(t == pl.num_programs(2) - 1)
    def _():
        for h in heads:
            s_ref[0, h] = st_ref[h].T


def hgrn_prompt(proj, lb, on, n, T, heads, tb=256, nh=4):
    nt = T // tb
    hg = heads // nh
    col = lambda k: pl.BlockSpec((tb, nh * LANES), lambda b, h, t, k=k: (b * nt + t, k * hg + h))
    return pl.pallas_call(
        functools.partial(_hgrn_body, tb=tb, nh=nh),
        grid=(n, hg, nt),
        in_specs=[col(0), col(1), col(2), col(3),
                  pl.BlockSpec((nh, 1, LANES), lambda b, h, t: (h, 0, 0)),
                  pl.BlockSpec((1, LANES), lambda b, h, t: (0, 0))],
        out_specs=[pl.BlockSpec((tb, nh * LANES), lambda b, h, t: (b * nt + t, h)),
                   pl.BlockSpec((1, nh, LANES, LANES), lambda b, h, t: (b, h, 0, 0))],
        out_shape=[jax.ShapeDtypeStruct((n * T, heads * LANES), BF16),
                   jax.ShapeDtypeStruct((n, heads, LANES, LANES), F32)],
        scratch_shapes=[pltpu.VMEM((nh, LANES, LANES), F32)],
        compiler_params=_params("parallel", "parallel", "arbitrary"),
        name="hgrn_prompt",
    )(proj, proj, proj, proj, lb.reshape(heads, 1, LANES), on.reshape(1, LANES))


def _col(eye, row):
    return jnp.sum(eye * row, axis=1, keepdims=True)


def _hgrn_s_body(q_ref, f_ref, i_ref, g_ref, lb_ref, on_ref, s_ref, o_ref, so_ref, *, nb):
    qr = q_ref[...]
    q = qr * _sigmoid(qr)
    log_f, kk = _hgrn_gates(f_ref[...], lb_ref[0])
    v = i_ref[...]
    f = jnp.exp(log_f)
    eye = (lax.broadcasted_iota(jnp.int32, (LANES, LANES), 0)
           == lax.broadcasted_iota(jnp.int32, (LANES, LANES), 1)).astype(F32)
    qf = (q * f).astype(BF16)
    a = jnp.sum(q * kk, axis=1, keepdims=True)
    rows = []
    for b in range(nb):
        S = s_ref[b, 0, 0]
        so_ref[b, 0] = _col(eye, f[b:b + 1]) * S + _col(eye, kk[b:b + 1]) * v[b:b + 1]
        rows.append(_dot(qf, S.astype(BF16))[b:b + 1])
    o = jnp.concatenate(rows, axis=0) + a * v
    o_ref[...] = _hgrn_out(o, on_ref[...], g_ref[...])


def hgrn_sample(proj, lb, on, state, li, heads):
    nb = proj.shape[0]
    col = lambda k: pl.BlockSpec((nb, LANES), lambda h, k=k: (0, k * heads + h))
    return pl.pallas_call(
        functools.partial(_hgrn_s_body, nb=nb),
        grid=(heads,),
        in_specs=[col(0), col(1), col(2), col(3),
                  pl.BlockSpec((1, 1, LANES), lambda h: (h, 0, 0)),
                  pl.BlockSpec((1, LANES), lambda h: (0, 0)),
                  pl.BlockSpec((nb, 1, 1, LANES, LANES), lambda h: (0, li, h, 0, 0))],
        out_specs=[pl.BlockSpec((nb, LANES), lambda h: (0, h)),
                   pl.BlockSpec((nb, 1, LANES, LANES), lambda h: (0, h, 0, 0))],
        out_shape=[jax.ShapeDtypeStruct((nb, heads * LANES), F32),
                   jax.ShapeDtypeStruct((nb, heads, LANES, LANES), F32)],
        compiler_params=_params("parallel"),
        name="hgrn_sample",
    )(proj, proj, proj, proj, lb.reshape(heads, 1, LANES), on.reshape(1, LANES), state)


def _dil_multiplicity(delta):
    c = np.zeros(delta.shape, np.float32)
    for window, dil in DIL_PATTERNS:
        c += ((delta >= 0) & (delta <= window) & (delta % dil == 0)).astype(np.float32)
    return c


def _dil_body(q_ref, k_ref, v_ref, qn_ref, kn_ref, c_ref, o_ref, ko_ref, vo_ref, kn_s, vb_s, *, T, tq, W, pad):
    qi = pl.program_id(2)
    span = pad + tq

    @pl.when(qi == 0)
    def _():
        kw = kn_ref[...]
        kn_s[0:pad, :] = jnp.zeros((pad, LANES), BF16)
        vb_s[0:pad, :] = jnp.zeros((pad, LANES), BF16)

        def norm(c, carry):
            rows = pl.ds(pl.multiple_of(c * 512, 512), 512)
            dst = pl.ds(pl.multiple_of(pad + c * 512, 512), 512)
            kn_s[dst, :] = _rms(k_ref[rows, :], kw).astype(BF16)
            vb_s[dst, :] = v_ref[rows, :].astype(BF16)
            return carry

        lax.fori_loop(0, T // 512, norm, 0)
        ko_ref[0] = _rms(k_ref[T - W:, :], kw)
        vo_ref[0] = v_ref[T - W:, :]

    q = (_rms(q_ref[...], qn_ref[...]) * SCALE).astype(BF16)
    rows = pl.ds(pl.multiple_of(qi * tq, tq), span)
    sc = _dot_nt(q, kn_s[rows, :])
    c = c_ref[...]
    exists = lax.broadcasted_iota(jnp.int32, (tq, span), 1) >= pad - qi * tq
    sc = jnp.where((c > 0.0) & exists, sc, NEG_INF)
    m = jnp.max(sc, axis=1, keepdims=True)
    p = c * jnp.exp(sc - m)
    l = jnp.sum(p, axis=1, keepdims=True)
    o_ref[...] = (_dot(p.astype(BF16), vb_s[rows, :]) / l).astype(o_ref.dtype)


def dil_prompt(proj, qn, kn, n, T, heads, col0, tq=256):
    W = min(DIL_MAX_WINDOW, T)
    pad = DIL_MAX_WINDOW
    nt = T // tq
    ctab = jnp.asarray(_dil_multiplicity(np.arange(tq)[:, None] + pad - np.arange(pad + tq)[None, :]))
    cb = col0 // LANES
    return pl.pallas_call(
        functools.partial(_dil_body, T=T, tq=tq, W=W, pad=pad),
        grid=(n, heads, nt),
        in_specs=[pl.BlockSpec((tq, LANES), lambda b, h, t: (b * nt + t, cb + h)),
                  pl.BlockSpec((T, LANES), lambda b, h, t: (b, cb + heads + h)),
                  pl.BlockSpec((T, LANES), lambda b, h, t: (b, cb + 2 * heads + h)),
                  pl.BlockSpec((1, LANES), lambda b, h, t: (0, 0)),
                  pl.BlockSpec((1, LANES), lambda b, h, t: (0, 0)),
                  pl.BlockSpec((tq, pad + tq), lambda b, h, t: (0, 0))],
        out_specs=[pl.BlockSpec((tq, LANES), lambda b, h, t: (b * nt + t, h)),
                   pl.BlockSpec((1, W, LANES), lambda b, h, t: (b, 0, h)),
                   pl.BlockSpec((1, W, LANES), lambda b, h, t: (b, 0, h))],
        out_shape=[jax.ShapeDtypeStruct((n * T, heads * LANES), BF16),
                   jax.ShapeDtypeStruct((n, W, heads * LANES), F32),
                   jax.ShapeDtypeStruct((n, W, heads * LANES), F32)],
        scratch_shapes=[pltpu.VMEM((T + pad, LANES), BF16), pltpu.VMEM((T + pad, LANES), BF16)],
        compiler_params=_params("parallel", "parallel", "arbitrary"),
        name="dil_prompt",
    )(proj, proj, proj, qn.reshape(1, LANES), kn.reshape(1, LANES), ctab)


def _shift_in(buf, new_row):
    n = buf.shape[0]
    rolled = pltpu.roll(buf, n - 1, 0)
    return jnp.where(lax.broadcasted_iota(jnp.int32, buf.shape, 0) == n - 1, new_row, rolled)


def _dil_s_body(q_ref, k_ref, v_ref, qn_ref, kn_ref, c_ref, ck_ref, cv_ref, ck_hbm, cv_hbm, *rest,
                heads, c_new, li, has_prev):
    o_ref, ko_hbm, vo_hbm, m_s, l_s, acc_s, new_s, sem = rest[2:] if has_prev else rest
    H = heads
    b, c = pl.program_id(0), pl.program_id(1)
    nrows = ck_hbm.shape[2]
    sub = lax.broadcasted_iota(jnp.int32, (H, LANES), 0)

    def heads_on_rows(ref):
        out = jnp.zeros((H, LANES), F32)
        for h in range(H):
            out = jnp.where(sub == h, ref[:, h * LANES:(h + 1) * LANES], out)
        return out

    q8 = _rms(heads_on_rows(q_ref), qn_ref[...]) * SCALE
    kn8 = _rms(heads_on_rows(k_ref), kn_ref[...])
    v8 = heads_on_rows(v_ref)

    def shift_copies():
        body, tail = pl.ds(0, nrows - H), pl.ds(nrows - H, H)
        out = [pltpu.make_async_copy(new_s.at[0], ko_hbm.at[b, li, tail], sem.at[2]),
               pltpu.make_async_copy(new_s.at[1], vo_hbm.at[b, li, tail], sem.at[3])]
        if not has_prev:
            for l in range(ck_hbm.shape[1]):
                out.append(pltpu.make_async_copy(ck_hbm.at[b, l, pl.ds(H, nrows - H)], ko_hbm.at[b, l, body], sem.at[0]))
                out.append(pltpu.make_async_copy(cv_hbm.at[b, l, pl.ds(H, nrows - H)], vo_hbm.at[b, l, body], sem.at[1]))
                if l != li:
                    out.append(pltpu.make_async_copy(new_s.at[2], ko_hbm.at[b, l, tail], sem.at[2]))
                    out.append(pltpu.make_async_copy(new_s.at[2], vo_hbm.at[b, l, tail], sem.at[3]))
        return out

    @pl.when(c == 0)
    def _():
        m_s[...] = jnp.full_like(m_s, NEG_INF)
        l_s[...] = jnp.zeros_like(l_s)
        acc_s[...] = jnp.zeros_like(acc_s)
        new_s[0] = kn8
        new_s[1] = v8
        new_s[2] = jnp.zeros((H, LANES), F32)
        for cp in shift_copies():
            cp.start()

    kc, vc = ck_ref[0, 0], cv_ref[0, 0]
    s = _dot_nt(q8.astype(BF16), kc.astype(BF16))
    own = lax.broadcasted_iota(jnp.int32, s.shape, 1) % H == lax.broadcasted_iota(jnp.int32, s.shape, 0)
    cm = jnp.where(own, c_ref[0], 0.0)
    s = jnp.where(cm > 0.0, s, NEG_INF)
    m_old = m_s[...]
    m_new = jnp.maximum(m_old, jnp.max(s, axis=1, keepdims=True))
    alpha = jnp.exp(m_old - m_new)
    p = cm * jnp.exp(s - m_new[:, 0:1])
    l_s[...] = alpha * l_s[...] + jnp.sum(p, axis=1, keepdims=True)
    acc_s[...] = alpha * acc_s[...] + _dot(p.astype(BF16), vc.astype(BF16))
    m_s[...] = m_new

    @pl.when(c == pl.num_programs(1) - 1)
    def _():
        s_new = jnp.sum(q8 * kn8, axis=1, keepdims=True)
        m_f = jnp.maximum(m_s[...], s_new)
        a = jnp.exp(m_s[...] - m_f)
        p_new = c_new * jnp.exp(s_new - m_f)
        o = (a * acc_s[...] + p_new * v8) / (a * l_s[...] + p_new)
        for h in range(H):
            o_ref[:, h * LANES:(h + 1) * LANES] = o[h:h + 1]
        for cp in shift_copies():
            cp.wait()


def dil_sample(proj, qn, kn, cache_k, cache_v, li, heads, col0, prev=None, chunk=512):
    nb, n_l, Wb = cache_k.shape[0], cache_k.shape[1], cache_k.shape[2]
    H = heads
    ck = cache_k.reshape(nb, n_l, Wb * H, LANES)
    cv = cache_v.reshape(nb, n_l, Wb * H, LANES)
    nch = Wb // chunk
    c_buf = jnp.asarray(np.repeat(_dil_multiplicity(Wb - np.arange(Wb)), H).reshape(nch, 1, chunk * H))
    c_new = float(_dil_multiplicity(np.zeros((1,), np.int64))[0])
    hw = H * LANES
    proj = proj.reshape(nb, 1, -1)
    col = lambda k: pl.BlockSpec((None, 1, hw), lambda b, c, k=k: (b, 0, col0 // hw + k))
    vec = pl.BlockSpec((1, LANES), lambda b, c: (0, 0))
    cache = pl.BlockSpec((1, 1, chunk * H, LANES), lambda b, c: (b, li, c, 0))
    anyspec = pl.BlockSpec(memory_space=pl.ANY)
    full = jax.ShapeDtypeStruct((nb, n_l, Wb * H, LANES), F32)
    args = [proj, proj, proj, qn.reshape(1, LANES), kn.reshape(1, LANES), c_buf, ck, cv, ck, cv]
    in_specs = [col(0), col(1), col(2), vec, vec, pl.BlockSpec((1, 1, chunk * H), lambda b, c: (c, 0, 0)),
                cache, cache, anyspec, anyspec]
    aliases = {}
    if prev is not None:
        aliases = {len(args): 1, len(args) + 1: 2}
        args += list(prev)
        in_specs += [anyspec, anyspec]
    o, ok, ov = pl.pallas_call(
        functools.partial(_dil_s_body, heads=H, c_new=c_new, li=li, has_prev=prev is not None),
        grid=(nb, nch),
        in_specs=in_specs,
        out_specs=[pl.BlockSpec((None, 1, hw), lambda b, c: (b, 0, 0)), anyspec, anyspec],
        out_shape=[jax.ShapeDtypeStruct((nb, 1, hw), F32), full, full],
        scratch_shapes=[pltpu.VMEM((H, LANES), F32), pltpu.VMEM((H, LANES), F32), pltpu.VMEM((H, LANES), F32),
                        pltpu.VMEM((3, H, LANES), F32), pltpu.SemaphoreType.DMA((4,))],
        input_output_aliases=aliases,
        compiler_params=_params("arbitrary", "arbitrary"),
        name="dil_sample",
    )(*args)
    return o.reshape(nb, hw), ok, ov


def _gelu(x):
    return 0.5 * x * (1.0 + jnp.tanh(0.7978845608028654 * (x + 0.044715 * x * x * x)))


def _compress_rows(load_j, pe_ref, w1_ref, w2_ref, rows):
    acc = jnp.zeros((rows, LANES), F32)
    for j in range(CMP_LEN):
        acc = acc + _dot((load_j(j) + pe_ref[j:j + 1, :]).astype(BF16), w1_ref[j])
    return _dot(_gelu(acc).astype(BF16), w2_ref[...])


def _nsa_prep_body(kc_ref, vc_ref, ks_ref, vs_ref, kw_ref, vw_ref, knorm_ref, pek_ref, pev_ref,
                   w1k_ref, w2k_ref, w1v_ref, w2v_ref,
                   kcmp_ref, vcmp_ref, ksb_ref, ksf_ref, vst_ref, kwb_ref, kwf_ref, vwt_ref, *, T):
    nblk = T // CMP_LEN
    kcmp = _compress_rows(lambda j: kc_ref[pl.ds(j, nblk, stride=CMP_LEN), :], pek_ref, w1k_ref, w2k_ref, nblk)
    kcmp_ref[0, 0] = _rms(kcmp, knorm_ref[0:1, :])
    vcmp_ref[0, 0] = _compress_rows(lambda j: vc_ref[pl.ds(j, nblk, stride=CMP_LEN), :], pev_ref, w1v_ref, w2v_ref, nblk)
    ks_w, kw_w = knorm_ref[1:2, :], knorm_ref[2:3, :]

    kwb_ref[0, 0, 0:WIN, :] = jnp.zeros((WIN, LANES), BF16)
    for i in range(WIN // LANES):
        vwt_ref[0, 0, i] = jnp.zeros((LANES, LANES), BF16)

    def tile(c, carry):
        rows = pl.ds(pl.multiple_of(c * LANES, LANES), LANES)
        ksn = _rms(ks_ref[rows, :], ks_w)
        ksf_ref[0, rows, :] = ksn
        ksb_ref[0, 0, rows, :] = ksn.astype(BF16)
        kwb_ref[0, 0, pl.ds(pl.multiple_of(WIN + c * LANES, LANES), LANES), :] = _rms(kw_ref[rows, :], kw_w).astype(BF16)
        vwt_ref[0, 0, WIN // LANES + c] = vw_ref[rows, :].T.astype(BF16)
        return carry

    lax.fori_loop(0, T // LANES, tile, 0)

    def chunk(c, carry):
        rows = pl.ds(pl.multiple_of(c * SLC_CHUNK, SLC_CHUNK), SLC_CHUNK)
        vst_ref[0, 0, c] = vs_ref[rows, :].T.astype(BF16)
        return carry

    lax.fori_loop(0, T // SLC_CHUNK, chunk, 0)
    ww = min(WIN, T)
    kwf_ref[0] = _rms(kw_ref[T - ww:, :], kw_w)


def nsa_prep(proj, knorm, pe_k, pe_v, w1k, w2k, w1v, w2v, n, T, hk):
    cb = (hk * C_GROUP * HEAD_DIM) // LANES
    nblk = T // CMP_LEN
    nt = T // LANES
    ww = min(WIN, T)
    col = lambda k: pl.BlockSpec((T, LANES), lambda b, h, k=k: (b, cb + k * hk + h))
    full = lambda a: pl.BlockSpec(a.shape, lambda b, h: (0,) * a.ndim)
    per = lambda *s: pl.BlockSpec((1, 1) + s, lambda b, h: (b, h) + (0,) * len(s))
    return pl.pallas_call(
        functools.partial(_nsa_prep_body, T=T),
        grid=(n, hk),
        in_specs=[col(0), col(1), col(2), col(3), col(4), col(5),
                  full(knorm), full(pe_k), full(pe_v), full(w1k), full(w2k), full(w1v), full(w2v)],
        out_specs=[per(nblk, LANES), per(nblk, LANES), per(T, LANES),
                   pl.BlockSpec((1, T, LANES), lambda b, h: (b, 0, h)),
                   per(T // SLC_CHUNK, LANES, SLC_CHUNK), per(T + WIN, LANES),
                   pl.BlockSpec((1, ww, LANES), lambda b, h: (b, 0, h)),
                   per(nt + WIN // LANES, LANES, LANES)],
        out_shape=[jax.ShapeDtypeStruct((n, hk, nblk, LANES), F32),
                   jax.ShapeDtypeStruct((n, hk, nblk, LANES), F32),
                   jax.ShapeDtypeStruct((n, hk, T, LANES), BF16),
                   jax.ShapeDtypeStruct((n, T, hk * LANES), F32),
                   jax.ShapeDtypeStruct((n, hk, T // SLC_CHUNK, LANES, SLC_CHUNK), BF16),
                   jax.ShapeDtypeStruct((n, hk, T + WIN, LANES), BF16),
                   jax.ShapeDtypeStruct((n, ww, hk * LANES), F32),
                   jax.ShapeDtypeStruct((n, hk, nt + WIN // LANES, LANES, LANES), BF16)],
        compiler_params=_params("parallel", "parallel"),
        name="nsa_prep",
    )(proj, proj, proj, proj, proj, proj, knorm, pe_k, pe_v, w1k, w2k, w1v, w2v)


def _tile4(x):
    return jnp.concatenate([x] * C_GROUP, axis=1)


def _nsa_body(q_ref, gate_ref, qn_ref, kcmp_ref, vcmp_ref, ks_ref, vst_ref, kw_ref, vwt_ref, o_ref,
              vct_s, pb_s, sel_s, gt_s, *, tq, nblk, nslc, hk_n):
    G = C_GROUP
    hk = pl.program_id(1)
    qi = pl.program_id(2)
    t0 = qi * tq

    @pl.when(qi == 0)
    def _():
        vct_s[...] = vcmp_ref[0, 0].T.astype(BF16)

    qw = qn_ref[...]
    q4 = jnp.concatenate(
        [(_rms(q_ref[:, g * LANES:(g + 1) * LANES], qw) * SCALE).astype(BF16) for g in range(G)], axis=0)

    st = _dot_nt(kcmp_ref[0, 0].astype(BF16), q4)
    blk = lax.broadcasted_iota(jnp.int32, (nblk, G * tq), 0)
    tpos = t0 + (lax.broadcasted_iota(jnp.int32, (nblk, G * tq), 1) & (tq - 1))
    valid = (blk + 1) * CMP_LEN - 1 <= tpos
    st = jnp.where(valid, st, NEG_INF)
    p = jnp.where(valid, jnp.exp(st - jnp.max(st, axis=0, keepdims=True)), 0.0)
    p = p / jnp.maximum(jnp.sum(p, axis=0, keepdims=True), 1.0)
    o_cmp = _dot(vct_s[...], p.astype(BF16))
    pb = p[:, 0:tq]
    for g in range(1, G):
        pb = pb + p[:, g * tq:(g + 1) * tq]
    ratio = SLC_BLOCK // CMP_LEN
    imps = []
    for i in range(tq // LANES):
        pb_s[i] = pb[:, i * LANES:(i + 1) * LANES]
        part = pb_s[i, pl.ds(0, nslc, stride=ratio), :]
        for r in range(1, ratio):
            part = part + pb_s[i, pl.ds(r, nslc, stride=ratio), :]
        imps.append(part)
    imp = jnp.concatenate(imps, axis=1)

    jb = lax.broadcasted_iota(jnp.int32, (nslc, tq), 0)
    tp = t0 + lax.broadcasted_iota(jnp.int32, (nslc, tq), 1)
    cur = tp // SLC_BLOCK
    forced = (jb == 0) | (jb == cur) | (jb == cur - 1)
    score = jnp.where(jb * SLC_BLOCK <= tp, jnp.where(forced, FORCE_SCORE, imp), -FORCE_SCORE)
    rank = jnp.zeros((nslc, tq), F32)
    for jp in range(nslc):
        row = score[jp:jp + 1, :]
        ahead = (row > score) | ((row == score) & (jb > jp))
        rank = rank + ahead.astype(F32)
    sel_s[...] = jnp.where(rank < float(min(SLC_TOPN, nslc)), 0.0, NEG_INF)

    KC = SLC_CHUNK
    per_chunk = KC // SLC_BLOCK

    def slc_step(kc, carry, causal):
        m, l, acc = carry
        rows = pl.ds(pl.multiple_of(kc * KC, KC), KC)
        s = _dot_nt(ks_ref[0, 0, rows, :], q4)
        bias = jnp.concatenate(
            [jnp.broadcast_to(sel_s[pl.ds(kc * per_chunk + r, 1), :], (SLC_BLOCK, tq)) for r in range(per_chunk)],
            axis=0)
        if causal:
            kpos = kc * KC + lax.broadcasted_iota(jnp.int32, (KC, tq), 0)
            bias = jnp.where(kpos <= t0 + lax.broadcasted_iota(jnp.int32, (KC, tq), 1), bias, NEG_INF)
        s = s + _tile4(bias)
        m_new = jnp.maximum(m, jnp.max(s, axis=0, keepdims=True))
        alpha = jnp.exp(m - m_new)
        pp = jnp.exp(s - m_new)
        l = alpha * l + jnp.sum(pp, axis=0, keepdims=True)
        acc = alpha * acc + _dot(vst_ref[0, 0, kc], pp.astype(BF16))
        return m_new, l, acc

    init = (jnp.full((1, G * tq), NEG_INF, F32), jnp.zeros((1, G * tq), F32), jnp.zeros((LANES, G * tq), F32))
    last = (t0 + tq - 1) // KC
    carry = lax.fori_loop(0, last, functools.partial(slc_step, causal=False), init)
    _, l_s, acc_s = slc_step(last, carry, causal=True)
    o_slc = acc_s / l_s

    wspan = WIN + tq
    wsub = lax.broadcasted_iota(jnp.int32, (wspan, tq), 0)
    dist = lax.broadcasted_iota(jnp.int32, (wspan, tq), 1) + WIN - wsub
    okw = (dist >= 0) & (dist <= WIN) & (wsub >= WIN - t0)
    sw = _dot_nt(kw_ref[0, 0, pl.ds(pl.multiple_of(t0, tq), wspan), :], q4)
    sw = sw + _tile4(jnp.where(okw, 0.0, NEG_INF))
    pw = jnp.exp(sw - jnp.max(sw, axis=0, keepdims=True))
    vw_t = jnp.concatenate([vwt_ref[0, 0, qi * (tq // LANES) + i] for i in range(wspan // LANES)], axis=1)
    o_win = _dot(vw_t, pw.astype(BF16)) / jnp.sum(pw, axis=0, keepdims=True)

    gt_s[...] = _sigmoid(gate_ref[...]).T
    nh = G * hk_n
    for g in range(G):
        sl = slice(g * tq, (g + 1) * tq)
        g0 = gt_s[pl.ds(hk * G + g, 1), :]
        g1 = gt_s[pl.ds(nh + hk * G + g, 1), :]
        g2 = gt_s[pl.ds(2 * nh + hk * G + g, 1), :]
        o = g0 * o_cmp[:, sl] + g1 * o_slc[:, sl] + g2 * o_win[:, sl]
        o_ref[:, g * LANES:(g + 1) * LANES] = o.T.astype(o_ref.dtype)


def nsa_prompt(proj, qn, kcmp, vcmp, ksb, vst, kwb, vwt, n, T, hk, gate_col, tq=256):
    nblk, nslc, nt = T // CMP_LEN, T // SLC_BLOCK, T // tq
    gw = C_GROUP * LANES
    per = lambda *s: pl.BlockSpec((1, 1) + s, lambda b, h, t: (b, h) + (0,) * len(s))
    return pl.pallas_call(
        functools.partial(_nsa_body, tq=tq, nblk=nblk, nslc=nslc, hk_n=hk),
        grid=(n, hk, nt),
        in_specs=[pl.BlockSpec((tq, gw), lambda b, h, t: (b * nt + t, h)),
                  pl.BlockSpec((tq, LANES), lambda b, h, t: (b * nt + t, gate_col // LANES)),
                  pl.BlockSpec((1, LANES), lambda b, h, t: (0, 0)),
                  per(nblk, LANES), per(nblk, LANES), per(T, LANES), per(T // SLC_CHUNK, LANES, SLC_CHUNK),
                  per(T + WIN, LANES), per((T + WIN) // LANES, LANES, LANES)],
        out_specs=pl.BlockSpec((tq, gw), lambda b, h, t: (b * nt + t, h)),
        out_shape=jax.ShapeDtypeStruct((n * T, hk * gw), BF16),
        scratch_shapes=[pltpu.VMEM((LANES, nblk), BF16), pltpu.VMEM((tq // LANES, nblk, LANES), F32),
                        pltpu.VMEM((nslc, tq), F32), pltpu.VMEM((LANES, tq), F32)],
        compiler_params=_params("parallel", "parallel", "arbitrary"),
        name="nsa_prompt",
    )(proj, proj, qn.reshape(1, LANES), kcmp, vcmp, ksb, vst, kwb, vwt)


def _cmp_pages_body(pt_ref, ck_hbm, cv_hbm, knorm_ref, pek_ref, pev_ref, w1k_ref, w2k_ref, w1v_ref, w2v_ref,
                    ko_ref, vo_ref, kbuf, vbuf, fold_s, sem, *, li, P, hk, rows_per_page):
    s = pl.program_id(0)
    ns = pl.num_programs(0)
    gpp = rows_per_page // hk // CMP_LEN

    def copies(step, slot):
        out = []
        for p in range(P):
            page = pt_ref[step * P + p]
            dst = pl.ds(p * gpp, gpp)
            out.append(pltpu.make_async_copy(ck_hbm.at[page, li], kbuf.at[slot, dst], sem.at[0, slot]))
            out.append(pltpu.make_async_copy(cv_hbm.at[page, li], vbuf.at[slot, dst], sem.at[1, slot]))
        return out

    @pl.when(s == 0)
    def _():
        for c in copies(0, 0):
            c.start()

    @pl.when(s + 1 < ns)
    def _():
        for c in copies(s + 1, (s + 1) % 2):
            c.start()

    slot = s % 2
    for c in copies(s, slot):
        c.wait()

    groups = P * (rows_per_page // hk // CMP_LEN)
    tpv = 8 // hk
    own = [lax.broadcasted_iota(jnp.int32, (groups * 8, LANES), 0) % 8 // hk == u for u in range(tpv)]

    def compress(buf, pe_ref, w1_ref, w2_ref):
        acc = jnp.zeros((groups * 8, LANES), F32)
        for jp in range(CMP_LEN // tpv):
            x = buf[slot, :, pl.ds(jp * 8, 8), :] + pe_ref[jp]
            y = _dot(x.reshape(groups * 8, LANES).astype(BF16), w1_ref[jp])
            part = y[:, 0:LANES]
            for u in range(1, tpv):
                part = jnp.where(own[u], y[:, u * LANES:(u + 1) * LANES], part)
            acc = acc + part
        tot = acc
        for u in range(1, tpv):
            tot = tot + pltpu.roll(acc, u * hk, 0)
        return _dot(_gelu(tot).astype(BF16), w2_ref[...])

    fold_s[0] = _rms(compress(kbuf, pek_ref, w1k_ref, w2k_ref), knorm_ref[0:1, :])
    fold_s[1] = compress(vbuf, pev_ref, w1v_ref, w2v_ref)
    for h in range(hk):
        rows = pl.ds(8 - hk + h, groups, stride=8)
        ko_ref[0, h] = fold_s[0, rows, :]
        vo_ref[0, h] = fold_s[1, rows, :]


def cmp_pages(page_table, cache_k, cache_v, li, knorm, pe_k, pe_v, w1k, w2k, w1v, w2v, P=16):
    nb, n_pages = page_table.shape
    n_pool, n_l, page, hk, dh = cache_k.shape
    rpp = page * hk
    gpp, grows = page // CMP_LEN, CMP_LEN * hk
    ck = cache_k.reshape(n_pool, n_l, gpp, grows, dh)
    cv = cache_v.reshape(n_pool, n_l, gpp, grows, dh)
    tpv = 8 // hk
    slab_pe = lambda pe: jnp.repeat(pe, hk, axis=0).reshape(CMP_LEN // tpv, 8, dh)
    slab_w = lambda w: w.reshape(CMP_LEN // tpv, tpv, dh, dh).transpose(0, 2, 1, 3).reshape(CMP_LEN // tpv, dh, tpv * dh)
    pe_k, pe_v, w1k, w1v = slab_pe(pe_k), slab_pe(pe_v), slab_w(w1k), slab_w(w1v)
    steps_per_b = n_pages // P
    blocks = P * gpp
    nblk = n_pages * gpp
    full = lambda a: pl.BlockSpec(a.shape, lambda s, pt: (0,) * a.ndim)
    out_spec = pl.BlockSpec((1, hk, blocks, dh), lambda s, pt: (s // steps_per_b, 0, s % steps_per_b, 0))
    grid_spec = pltpu.PrefetchScalarGridSpec(
        num_scalar_prefetch=1,
        grid=(nb * steps_per_b,),
        in_specs=[pl.BlockSpec(memory_space=pl.ANY), pl.BlockSpec(memory_space=pl.ANY),
                  full(knorm), full(pe_k), full(pe_v), full(w1k), full(w2k), full(w1v), full(w2v)],
        out_specs=[out_spec, out_spec],
        scratch_shapes=[pltpu.VMEM((2, blocks, grows, dh), F32), pltpu.VMEM((2, blocks, grows, dh), F32),
                        pltpu.VMEM((2, blocks * 8, dh), F32), pltpu.SemaphoreType.DMA((2, 2))])
    return pl.pallas_call(
        functools.partial(_cmp_pages_body, li=li, P=P, hk=hk, rows_per_page=rpp),
        grid_spec=grid_spec,
        out_shape=[jax.ShapeDtypeStruct((nb, hk, nblk, dh), F32)] * 2,
        compiler_params=_params("arbitrary"),
        name="cmp_pages",
    )(page_table.reshape(-1), ck, cv, knorm, pe_k, pe_v, w1k, w2k, w1v, w2v)


def _nsa_s_select_body(q_ref, kc_ref, vc_ref, ks_ref, qn_ref, knorm_ref, pek_ref, pev_ref,
                       w1k_ref, w2k_ref, w1v_ref, w2v_ref, kcmp_ref, vcmp_ref,
                       qo_ref, ocmp_ref, sel_ref, kso_ref, *, hk, qpos, nblk):
    G = C_GROUP
    row = slice(None)
    qw = qn_ref[...]
    nslc = (nblk + 1 + 1) // 2
    lanes_blk = lax.broadcasted_iota(jnp.int32, (8, nblk), 1)
    valid = (lanes_blk + 1) * CMP_LEN - 1 <= qpos
    valid_x = (jnp.full((8, 1), (nblk + 1) * CMP_LEN - 1, jnp.int32) <= qpos)
    pair = (lax.broadcasted_iota(jnp.int32, (nblk, nblk // 2), 0) // 2
            == lax.broadcasted_iota(jnp.int32, (nblk, nblk // 2), 1)).astype(F32)
    sub8 = lax.broadcasted_iota(jnp.int32, (8, 1), 0)
    pe_rest_k = jnp.zeros((8, LANES), F32)
    pe_rest_v = jnp.zeros((8, LANES), F32)
    for j in range(1, CMP_LEN):
        pe_rest_k = pe_rest_k + _dot(jnp.broadcast_to(pek_ref[j:j + 1, :], (8, LANES)).astype(BF16), w1k_ref[j])
        pe_rest_v = pe_rest_v + _dot(jnp.broadcast_to(pev_ref[j:j + 1, :], (8, LANES)).astype(BF16), w1v_ref[j])
    for h in range(hk):
        qs = [_rms(q_ref[row, (h * G + g) * LANES:(h * G + g + 1) * LANES], qw) * SCALE for g in range(G)]
        q8 = jnp.concatenate(qs + [jnp.zeros((8 - G, LANES), F32)], axis=0)
        qo_ref[0, h] = q8
        kso_ref[0, h] = jnp.broadcast_to(_rms(ks_ref[row, h * LANES:(h + 1) * LANES], knorm_ref[1:2, :]), (8, LANES))
        xk = jnp.broadcast_to(kc_ref[row, h * LANES:(h + 1) * LANES] + pek_ref[0:1, :], (8, LANES))
        xv = jnp.broadcast_to(vc_ref[row, h * LANES:(h + 1) * LANES] + pev_ref[0:1, :], (8, LANES))
        k_x = _dot(_gelu(_dot(xk.astype(BF16), w1k_ref[0]) + pe_rest_k).astype(BF16), w2k_ref[...])
        k_x = _rms(k_x, knorm_ref[0:1, :])
        v_x = _dot(_gelu(_dot(xv.astype(BF16), w1v_ref[0]) + pe_rest_v).astype(BF16), w2v_ref[...])
        s = jnp.where(valid, _dot_nt(q8.astype(BF16), kcmp_ref[0, h].astype(BF16)), NEG_INF)
        s_x = jnp.where(valid_x, jnp.sum(q8 * k_x, axis=1, keepdims=True), NEG_INF)
        m = jnp.maximum(jnp.max(s, axis=1, keepdims=True), s_x)
        p = jnp.where(valid, jnp.exp(s - m), 0.0)
        p_x = jnp.where(valid_x, jnp.exp(s_x - m), 0.0)
        den = jnp.maximum(jnp.sum(p, axis=1, keepdims=True) + p_x, 1.0)
        p = jnp.where(sub8 < G, p / den, 0.0)
        p_x = jnp.where(sub8 < G, p_x / den, 0.0)
        ocmp_ref[0, h] = _dot(p.astype(BF16), vcmp_ref[0, h].astype(BF16)) + p_x * v_x
        pb = jnp.sum(p, axis=0, keepdims=True)
        pb_x = jnp.sum(p_x, axis=0, keepdims=True)
        imp = _dot(jnp.broadcast_to(pb, (8, nblk)), pair, precision=HIGHEST)[0:1]
        lane = lax.broadcasted_iota(jnp.int32, (1, LANES), 1)
        tail = jnp.where(lane == 0, pb_x, -jnp.inf)
        imp = jnp.concatenate([imp, tail], axis=1)
        width = imp.shape[1]
        jb = lax.broadcasted_iota(jnp.int32, (1, width), 1)
        cur = qpos // SLC_BLOCK
        forced = (jb == 0) | (jb == cur) | (jb == cur - 1)
        score = jnp.where(jb * SLC_BLOCK <= qpos, jnp.where(forced, FORCE_SCORE, imp), -FORCE_SCORE)
        score = jnp.where(jb < nslc, score, -jnp.inf)
        sel = jnp.zeros((1, LANES), jnp.int32)
        for r in range(SLC_TOPN):
            best = jnp.max(score, axis=1, keepdims=True)
            idx = jnp.min(jnp.where(score == best, jb, width), axis=1, keepdims=True)
            sel = jnp.where(lane == r, idx, sel)
            score = jnp.where(jb == idx, -jnp.inf, score)
        sel_ref[0, h] = jnp.broadcast_to(sel, (8, LANES))


def nsa_sample_select(proj, qn, knorm, pe_k, pe_v, w1k, w2k, w1v, w2v, kcmp, vcmp, hk, qpos):
    nb = proj.shape[0]
    nblk = kcmp.shape[2]
    heads = hk * C_GROUP
    cq, ckv = heads * LANES, hk * LANES
    full = lambda a: pl.BlockSpec(a.shape, lambda b: (0,) * a.ndim)
    per = lambda *s: pl.BlockSpec((1,) + s, lambda b: (b,) + (0,) * len(s))
    colspec = lambda c0, w: pl.BlockSpec((None, 1, w), lambda b: (b, 0, c0 // w))
    proj = proj.reshape(nb, 1, -1)
    out8 = jax.ShapeDtypeStruct((nb, hk, 8, LANES), F32)
    return pl.pallas_call(
        functools.partial(_nsa_s_select_body, hk=hk, qpos=qpos, nblk=nblk),
        grid=(nb,),
        in_specs=[colspec(0, cq), colspec(cq, ckv), colspec(cq + ckv, ckv), colspec(cq + 2 * ckv, ckv),
                  pl.BlockSpec((1, LANES), lambda b: (0, 0)),
                  full(knorm), full(pe_k), full(pe_v), full(w1k), full(w2k), full(w1v), full(w2v),
                  per(hk, nblk, LANES), per(hk, nblk, LANES)],
        out_specs=[per(hk, 8, LANES)] * 4,
        out_shape=[out8, out8, jax.ShapeDtypeStruct((nb, hk, 8, LANES), jnp.int32), out8],
        compiler_params=_params("arbitrary"),
        name="nsa_sample_select",
    )(proj, proj, proj, proj, qn.reshape(1, LANES), knorm, pe_k, pe_v, w1k, w2k, w1v, w2v, kcmp, vcmp)


def _nsa_s_slc_body(sel_ref, pt_ref, q_ref, ksn_ref, vsn_ref, ck_hbm, cv_hbm, o_ref, kbuf, vbuf, sem,
                    *, hk, li, n_past_blocks, per_page, n_pages):
    s = pl.program_id(0)
    ns = pl.num_programs(0)
    rows = SLC_BLOCK * hk

    def copies(step, slot):
        out = []
        for r in range(SLC_TOPN):
            blk = jnp.minimum(sel_ref[step * SLC_TOPN + r], n_past_blocks - 1)
            page = pt_ref[(step // hk) * n_pages + blk // per_page]
            src = pl.ds(pl.multiple_of((blk % per_page) * rows, rows), rows)
            dst = pl.ds(r * rows, rows)
            out.append(pltpu.make_async_copy(ck_hbm.at[page, li, src], kbuf.at[slot, dst], sem.at[0, slot]))
            out.append(pltpu.make_async_copy(cv_hbm.at[page, li, src], vbuf.at[slot, dst], sem.at[1, slot]))
        return out

    @pl.when(s == 0)
    def _():
        for c in copies(0, 0):
            c.start()

    @pl.when(s + 1 < ns)
    def _():
        for c in copies(s + 1, (s + 1) % 2):
            c.start()

    slot = s % 2
    for c in copies(s, slot):
        c.wait()

    h = s % hk
    nkeys = SLC_TOPN * SLC_BLOCK
    first = lax.broadcasted_iota(jnp.int32, (SLC_BLOCK, LANES), 0) == 0
    lane = lax.broadcasted_iota(jnp.int32, (1, nkeys), 1)
    k_new, v_new = ksn_ref[0, 0, 0:1, :], vsn_ref[...]
    ks, vs = [], []
    okf = jnp.ones((1, nkeys), F32)
    for r in range(SLC_TOPN):
        is_new = sel_ref[s * SLC_TOPN + r] >= n_past_blocks
        rws = pl.ds(r * rows + h, SLC_BLOCK, stride=hk)
        ks.append(jnp.where(is_new, jnp.where(first, k_new, 0.0), kbuf[slot, rws, :]))
        vs.append(jnp.where(is_new, jnp.where(first, v_new, 0.0), vbuf[slot, rws, :]))
        okf = jnp.where((lane // SLC_BLOCK == r) & is_new, jnp.where(lane == r * SLC_BLOCK, 1.0, 0.0), okf)
    ok = okf > 0.5
    k = jnp.concatenate(ks, axis=0).astype(BF16)
    v = jnp.concatenate(vs, axis=0).astype(BF16)
    sc = jnp.where(ok, _dot_nt(q_ref[0, 0].astype(BF16), k), NEG_INF)
    p = jnp.where(ok, jnp.exp(sc - jnp.max(sc, axis=1, keepdims=True)), 0.0)
    o_ref[0, 0] = _dot(p.astype(BF16), v) / jnp.sum(p, axis=1, keepdims=True)


def nsa_sample_selected(sel, page_table, q8, ksn, proj, vs_col, cache_k, cache_v, li, hk):
    nb, n_pages = page_table.shape
    n_pool, n_l, page, _, dh = cache_k.shape
    rpp = page * hk
    per_page = page // SLC_BLOCK
    n_past_blocks = n_pages * per_page
    ck = cache_k.reshape(n_pool, n_l, rpp, dh)
    cv = cache_v.reshape(n_pool, n_l, rpp, dh)
    sel_flat = sel[:, :, 0, :SLC_TOPN].reshape(-1)
    proj = proj.reshape(nb, 1, -1)
    per = pl.BlockSpec((1, 1, 8, LANES), lambda s, s_, p_: (s // hk, s % hk, 0, 0))
    buf = pltpu.VMEM((2, SLC_TOPN * SLC_BLOCK * hk, dh), F32)
    grid_spec = pltpu.PrefetchScalarGridSpec(
        num_scalar_prefetch=2,
        grid=(nb * hk,),
        in_specs=[per, per,
                  pl.BlockSpec((None, 1, LANES), lambda s, s_, p_: (s // hk, 0, vs_col // LANES + s % hk)),
                  pl.BlockSpec(memory_space=pl.ANY), pl.BlockSpec(memory_space=pl.ANY)],
        out_specs=per,
        scratch_shapes=[buf, buf, pltpu.SemaphoreType.DMA((2, 2))])
    return pl.pallas_call(
        functools.partial(_nsa_s_slc_body, hk=hk, li=li, n_past_blocks=n_past_blocks, per_page=per_page,
                          n_pages=n_pages),
        grid_spec=grid_spec,
        out_shape=jax.ShapeDtypeStruct((nb, hk, 8, LANES), F32),
        compiler_params=_params("arbitrary"),
        name="nsa_sample_selected",
    )(sel_flat, page_table.reshape(-1), q8, ksn, proj, ck, cv)


def _nsa_s_win_body(q_ref, ocmp_ref, oslc_ref, kw_ref, vw_ref, gate_ref, knorm_ref, wk_ref, wv_ref,
                    o_ref, wko_ref, wvo_ref, *, hk):
    G = C_GROUP
    row = slice(None)
    eye = (lax.broadcasted_iota(jnp.int32, (LANES, LANES), 0)
           == lax.broadcasted_iota(jnp.int32, (LANES, LANES), 1)).astype(F32)
    gcol = _col(eye, _sigmoid(gate_ref[row, :]))
    nh = G * hk
    for h in range(hk):
        lanes = slice(h * LANES, (h + 1) * LANES)
        q8 = q_ref[0, h]
        kn = _rms(kw_ref[row, lanes], knorm_ref[2:3, :])
        v = vw_ref[row, lanes]
        kb, vb = wk_ref[0, 0, :, lanes], wv_ref[0, 0, :, lanes]
        s = _dot_nt(q8.astype(BF16), kb.astype(BF16))
        s_new = jnp.sum(q8 * kn, axis=1, keepdims=True)
        m = jnp.maximum(jnp.max(s, axis=1, keepdims=True), s_new)
        p = jnp.exp(s - m)
        p_new = jnp.exp(s_new - m)
        l = jnp.sum(p, axis=1, keepdims=True) + p_new
        o_win = (_dot(p.astype(BF16), vb.astype(BF16)) + p_new * v) / l
        g0 = gcol[h * G:h * G + 8]
        g1 = gcol[nh + h * G:nh + h * G + 8]
        g2 = gcol[2 * nh + h * G:2 * nh + h * G + 8]
        o = g0 * ocmp_ref[0, h] + g1 * oslc_ref[0, h] + g2 * o_win
        for g in range(G):
            o_ref[row, (h * G + g) * LANES:(h * G + g + 1) * LANES] = o[g:g + 1]
        wko_ref[0, :, lanes] = _shift_in(kb, kn)
        wvo_ref[0, :, lanes] = _shift_in(vb, v)


def nsa_sample_window(q8, ocmp, oslc, proj, kw_col, gate_col, knorm, win_k, win_v, li, hk):
    nb = proj.shape[0]
    Wb = win_k.shape[2]
    ckv = hk * LANES
    wk = win_k.reshape(nb, win_k.shape[1], Wb, ckv)
    wv = win_v.reshape(nb, win_v.shape[1], Wb, ckv)
    per = lambda *s: pl.BlockSpec((1,) + s, lambda b: (b,) + (0,) * len(s))
    proj = proj.reshape(nb, 1, -1)
    cw = hk * C_GROUP * LANES
    o, wko, wvo = pl.pallas_call(
        functools.partial(_nsa_s_win_body, hk=hk),
        grid=(nb,),
        in_specs=[per(hk, 8, LANES), per(hk, 8, LANES), per(hk, 8, LANES),
                  pl.BlockSpec((None, 1, ckv), lambda b: (b, 0, kw_col // ckv)),
                  pl.BlockSpec((None, 1, ckv), lambda b: (b, 0, kw_col // ckv + 1)),
                  pl.BlockSpec((None, 1, LANES), lambda b: (b, 0, gate_col // LANES)),
                  pl.BlockSpec(knorm.shape, lambda b: (0, 0)),
                  pl.BlockSpec((1, 1, Wb, ckv), lambda b: (b, li, 0, 0)),
                  pl.BlockSpec((1, 1, Wb, ckv), lambda b: (b, li, 0, 0))],
        out_specs=[pl.BlockSpec((None, 1, cw), lambda b: (b, 0, 0)),
                   per(Wb, ckv), per(Wb, ckv)],
        out_shape=[jax.ShapeDtypeStruct((nb, 1, cw), F32),
                   jax.ShapeDtypeStruct((nb, Wb, ckv), F32),
                   jax.ShapeDtypeStruct((nb, Wb, ckv), F32)],
        compiler_params=_params("parallel"),
        name="nsa_sample_window",
    )(q8, ocmp, oslc, proj, proj, proj, knorm, wk, wv)
    return o.reshape(nb, cw), wko, wvo


def _pad_cols(w, mult):
    pad = (-w.shape[-1]) % mult
    return jnp.pad(w, ((0, 0),) * (w.ndim - 1) + ((0, pad),)) if pad else w


def kernel(x_prompt, x_sample, state_hgrn, cache_dil_k, cache_dil_v, cache_cmp_k, cache_cmp_v, cache_slc_k, cache_slc_v, cache_win_k, cache_win_v, page_table, norm_mix, norm_mlp, w_in_even, w_out_even, hgrn_lb_logits, hgrn_out_norm, dil_q_norm, dil_k_norm, w_in_odd, w_out_odd, nsa_q_norm, nsa_k_norm, nsa_pe_k, nsa_pe_v, nsa_phi_k1, nsa_phi_k2, nsa_phi_v1, nsa_phi_v2, w_mlp_up, w_mlp_down):
    n, T, D = x_prompt.shape
    nb = x_sample.shape[0]
    assert x_sample.shape[1] == 1
    depth = norm_mix.shape[0]
    a_heads = hgrn_lb_logits.shape[1] // LANES
    b_heads = cache_dil_k.shape[3]
    hk = cache_win_k.shape[3]
    c_heads = hk * C_GROUP
    past_len = page_table.shape[1] * cache_cmp_k.shape[2]
    a_w = a_heads * LANES
    TN = 896
    TM = 512
    TM_IN = 1024 if (n * T) % 1024 == 0 else TM

    lb_cum = jnp.cumsum(jax.nn.softmax(hgrn_lb_logits.astype(F32), axis=0), axis=0)
    lower_bounds = lb_cum - lb_cum[0:1]

    cq, ckv = c_heads * LANES, hk * LANES
    gate_col = cq + 6 * ckv
    gate_w = w_in_odd[:, :, gate_col:].reshape(-1, D, hk, C_GROUP, 3).transpose(0, 1, 4, 2, 3).reshape(-1, D, 3 * c_heads)
    w_in_odd_p = jnp.concatenate([w_in_odd[:, :, :gate_col], _pad_cols(gate_w, LANES)], axis=-1)
    w_in_odd_p = _pad_cols(w_in_odd_p, TN).astype(BF16)
    w_in_even_b = w_in_even.astype(BF16)
    w_out_even_b = w_out_even.astype(BF16)
    w_out_odd_b = w_out_odd.astype(BF16)
    w_up_b = w_mlp_up.astype(BF16)
    w_down_b = w_mlp_down.astype(BF16)
    phi_k1 = nsa_phi_k1.reshape(-1, CMP_LEN, LANES, LANES).astype(BF16)
    phi_v1 = nsa_phi_v1.reshape(-1, CMP_LEN, LANES, LANES).astype(BF16)
    phi_k2 = nsa_phi_k2.astype(BF16)
    phi_v2 = nsa_phi_v2.astype(BF16)

    xp = x_prompt.reshape(n * T, D)
    xs = x_sample.reshape(nb, D)
    outs = {k: [] for k in ("hg_p", "hg_s", "dk_p", "dv_p", "dk_s", "dv_s", "ck_p", "cv_p", "sk_p", "sv_p",
                            "wk_p", "wv_p", "ck_s", "cv_s", "sk_s", "sv_s", "wk_s", "wv_s")}
    dil_bufs = None
    for layer in range(depth):
        li = layer // 2
        if layer % 2 == 0:
            w_in, w_out = w_in_even_b[li], w_out_even_b[li]
            lb, on = lower_bounds[li], hgrn_out_norm[li]
            qn, kn = dil_q_norm[li], dil_k_norm[li]
            pp = rms_matmul(xp, norm_mix[layer], w_in, TM_IN, TN)
            ps = rms_matmul(xs, norm_mix[layer], w_in, nb, TN)
            oa_p, st_p = hgrn_prompt(pp, lb, on, n, T, a_heads)
            ob_p, dk, dv = dil_prompt(pp, qn, kn, n, T, b_heads, 4 * a_w)
            oa_s, st_s = hgrn_sample(ps, lb, on, state_hgrn, li, a_heads)
            ob_s, *dil_bufs = dil_sample(ps, qn, kn, cache_dil_k, cache_dil_v, li, b_heads, 4 * a_w, prev=dil_bufs)
            w_halves = [w_out[:a_w], w_out[a_w:]]
            xp = proj_residual([oa_p, ob_p], w_halves, xp, TM)
            xs = proj_residual([oa_s, ob_s], w_halves, xs, nb)
            outs["hg_p"].append(st_p); outs["hg_s"].append(st_s)
            outs["dk_p"].append(dk.reshape(n, -1, b_heads, LANES)); outs["dv_p"].append(dv.reshape(n, -1, b_heads, LANES))
        else:
            w_in, w_out = w_in_odd_p[li], w_out_odd_b[li]
            knorm = nsa_k_norm[li]
            cmp_w = (nsa_pe_k[li], nsa_pe_v[li], phi_k1[li], phi_k2[li], phi_v1[li], phi_v2[li])
            pp = rms_matmul(xp, norm_mix[layer], w_in, TM_IN, TN)
            ps = rms_matmul(xs, norm_mix[layer], w_in, nb, TN)
            kcmp, vcmp, ksb, ksf, vst, kwb, kwf, vwt = nsa_prep(pp, knorm, *cmp_w, n, T, hk)
            o_p = nsa_prompt(pp, nsa_q_norm[li], kcmp, vcmp, ksb, vst, kwb, vwt, n, T, hk, gate_col)
            kcs, vcs = cmp_pages(page_table, cache_cmp_k, cache_cmp_v, li, knorm, *cmp_w)
            q8, ocmp, sel, ksn = nsa_sample_select(ps, nsa_q_norm[li], knorm, *cmp_w, kcs, vcs, hk, past_len)
            oslc = nsa_sample_selected(sel, page_table, q8, ksn, ps, cq + 3 * ckv, cache_slc_k, cache_slc_v, li, hk)
            o_s, wks, wvs = nsa_sample_window(q8, ocmp, oslc, ps, cq + 4 * ckv, gate_col, knorm,
                                              cache_win_k, cache_win_v, li, hk)
            xp = proj_residual([o_p], [w_out], xp, TM)
            xs = proj_residual([o_s], [w_out], xs, nb)
            kvp = lambda k: pp[:, cq + k * ckv:cq + (k + 1) * ckv].reshape(n, T, hk, LANES)
            kvs = lambda k: ps[:, cq + k * ckv:cq + (k + 1) * ckv].reshape(nb, 1, hk, LANES)
            ww = kwf.shape[1]
            outs["ck_p"].append(kvp(0)); outs["cv_p"].append(kvp(1))
            outs["sk_p"].append(ksf.reshape(n, T, hk, LANES)); outs["sv_p"].append(kvp(3))
            outs["wk_p"].append(kwf.reshape(n, ww, hk, LANES)); outs["wv_p"].append(kvp(5)[:, T - ww:])
            outs["ck_s"].append(kvs(0)); outs["cv_s"].append(kvs(1))
            outs["sk_s"].append(ksn[:, :, 0, :].reshape(nb, 1, hk, LANES)); outs["sv_s"].append(kvs(3))
            outs["wk_s"].append(wks.reshape(nb, -1, hk, LANES)); outs["wv_s"].append(wvs.reshape(nb, -1, hk, LANES))
        xp = mlp_residual(xp, norm_mlp[layer], w_up_b[layer], w_down_b[layer], TM, 512)
        xs = mlp_residual(xs, norm_mlp[layer], w_up_b[layer], w_down_b[layer], nb, 512)
    st = lambda k: jnp.stack(outs[k], axis=1)
    return (xp.reshape(n, T, D), xs.reshape(nb, 1, D),
            st("hg_p"), st("hg_s"), st("dk_p"), st("dv_p"),
            dil_bufs[0].reshape(cache_dil_k.shape), dil_bufs[1].reshape(cache_dil_v.shape),
            st("ck_p"), st("cv_p"), st("sk_p"), st("sv_p"), st("wk_p"), st("wv_p"),
            st("ck_s"), st("cv_s"), st("sk_s"), st("sv_s"), st("wk_s"), st("wv_s"))
```

```python
import functools

import numpy as np
import jax
import jax.numpy as jnp
from jax import lax
from jax.experimental import pallas as pl
from jax.experimental.pallas import tpu as pltpu

F32 = jnp.float32
BF16 = jnp.bfloat16
HIGHEST = lax.Precision.HIGHEST

HEAD_DIM = 128
LANES = 128
RMS_EPS = 1e-6
NEG_INF = -1e30
FORCE_SCORE = 1e6
SCALE = HEAD_DIM ** -0.5
HGRN_CHUNK = 64
HGRN_SUB = 8
DIL_PATTERNS = ((128, 1), (512, 4), (2048, 16))
DIL_MAX_WINDOW = 2048
CMP_LEN = 32
SLC_BLOCK = 64
SLC_TOPN = 16
WIN = 512
C_GROUP = 4
SLC_CHUNK = 512
VMEM_LIMIT = 56 * 1024 * 1024


def _params(*sem):
    return pltpu.CompilerParams(dimension_semantics=sem, vmem_limit_bytes=VMEM_LIMIT)


def _rms(x, w):
    return x * lax.rsqrt(jnp.mean(x * x, axis=-1, keepdims=True) + RMS_EPS) * w


def _sigmoid(x):
    return 1.0 / (1.0 + jnp.exp(-x))


def _dot_nt(a, b):
    return lax.dot_general(a, b, (((1,), (1,)), ((), ())), preferred_element_type=F32)


def _dot(a, b, precision=None):
    return jnp.dot(a, b, preferred_element_type=F32, precision=precision)


def _rms_mm_body(x_ref, g_ref, w_ref, o_ref, h_ref):
    @pl.when(pl.program_id(1) == 0)
    def _():
        h_ref[...] = _rms(x_ref[...], g_ref[...]).astype(BF16)

    o_ref[...] = _dot(h_ref[...], w_ref[...])


def rms_matmul(x, g, w, tm, tn):
    M, D = x.shape
    N = w.shape[1]
    return pl.pallas_call(
        _rms_mm_body,
        grid=(M // tm, N // tn),
        in_specs=[pl.BlockSpec((tm, D), lambda i, j: (i, 0)),
                  pl.BlockSpec((1, D), lambda i, j: (0, 0)),
                  pl.BlockSpec((D, tn), lambda i, j: (0, j))],
        out_specs=pl.BlockSpec((tm, tn), lambda i, j: (i, j)),
        out_shape=jax.ShapeDtypeStruct((M, N), F32),
        scratch_shapes=[pltpu.VMEM((tm, D), BF16)],
        compiler_params=_params("parallel", "arbitrary"),
        name="rms_matmul",
    )(x, g.reshape(1, D), w)


def _proj_res_body(*refs, n_in):
    res_ref, o_ref = refs[2 * n_in], refs[2 * n_in + 1]
    acc = res_ref[...]
    for a_ref, w_ref in zip(refs[:n_in], refs[n_in:2 * n_in]):
        acc = acc + _dot(a_ref[...].astype(BF16), w_ref[...])
    o_ref[...] = acc


def proj_residual(lhs, ws, res, tm):
    M, D = res.shape
    n = len(lhs)
    in_specs = [pl.BlockSpec((tm, a.shape[1]), lambda i: (i, 0)) for a in lhs]
    in_specs += [pl.BlockSpec(w.shape, lambda i: (0, 0)) for w in ws]
    in_specs += [pl.BlockSpec((tm, D), lambda i: (i, 0))]
    return pl.pallas_call(
        functools.partial(_proj_res_body, n_in=n),
        grid=(M // tm,),
        in_specs=in_specs,
        out_specs=pl.BlockSpec((tm, D), lambda i: (i, 0)),
        out_shape=jax.ShapeDtypeStruct((M, D), F32),
        compiler_params=_params("parallel"),
        name="proj_residual",
    )(*lhs, *ws, res)


def _mlp_body(x_ref, g_ref, wu_ref, wd_ref, o_ref, h_ref):
    @pl.when(pl.program_id(1) == 0)
    def _():
        x = x_ref[...]
        h_ref[...] = _rms(x, g_ref[...]).astype(BF16)
        o_ref[...] = x

    u = jnp.maximum(_dot(h_ref[...], wu_ref[...]), 0.0)
    o_ref[...] += _dot((u * u).astype(BF16), wd_ref[...])


def mlp_residual(x, g, wu, wd, tm, tf):
    M, D = x.shape
    Fd = wu.shape[1]
    return pl.pallas_call(
        _mlp_body,
        grid=(M // tm, Fd // tf),
        in_specs=[pl.BlockSpec((tm, D), lambda i, j: (i, 0)),
                  pl.BlockSpec((1, D), lambda i, j: (0, 0)),
                  pl.BlockSpec((D, tf), lambda i, j: (0, j)),
                  pl.BlockSpec((tf, D), lambda i, j: (j, 0))],
        out_specs=pl.BlockSpec((tm, D), lambda i, j: (i, 0)),
        out_shape=jax.ShapeDtypeStruct((M, D), F32),
        scratch_shapes=[pltpu.VMEM((tm, D), BF16)],
        compiler_params=_params("parallel", "arbitrary"),
        name="mlp_residual",
    )(x, g.reshape(1, D), wu, wd)


def _hgrn_gates(z, lb):
    log_sig = jnp.minimum(z, 0.0) - jnp.log1p(jnp.exp(-jnp.abs(z)))
    a = jnp.log(lb)
    b = jnp.log1p(-lb) + log_sig
    log_f = jnp.maximum(a, b) + jnp.log1p(jnp.exp(-jnp.abs(a - b)))
    series = -(log_f + 0.5 * log_f * log_f + log_f * log_f * log_f * (1.0 / 6.0))
    k = jnp.where(log_f > -0.01, series, 1.0 - jnp.exp(log_f))
    return log_f, k


def _hgrn_out(o, on, g_raw):
    return _rms(o, on) * (g_raw * _sigmoid(g_raw))


def _split3(x):
    hi = x.astype(BF16)
    r1 = x - hi.astype(F32)
    mid = r1.astype(BF16)
    lo = (r1 - mid.astype(F32)).astype(BF16)
    return jnp.concatenate([hi, mid, lo], axis=1)


def _hgrn_body(q_ref, f_ref, i_ref, g_ref, lb_ref, on_ref, o_ref, s_ref, st_ref, *, tb, nh):
    C, SC = HGRN_CHUNK, HGRN_SUB
    t = pl.program_id(2)
    heads = range(nh)

    @pl.when(t == 0)
    def _():
        st_ref[...] = jnp.zeros_like(st_ref)

    on = on_ref[...]
    r_i = lax.broadcasted_iota(jnp.int32, (C, C), 0)
    c_i = lax.broadcasted_iota(jnp.int32, (C, C), 1)
    tril = (r_i >= c_i).astype(BF16)
    row = lax.broadcasted_iota(jnp.int32, (C, LANES), 0)
    levels = []
    bs = C // 2
    while bs >= SC:
        levels.append((bs, (row // bs) % 2 == 1, ((r_i // bs) % 2 == 1) & (c_i // bs == r_i // bs - 1)))
        bs //= 2
    lane_c = lax.broadcasted_iota(jnp.int32, (SC, C), 1)
    sub_c = lax.broadcasted_iota(jnp.int32, (SC, C), 0)

    def chunk(c, carry):
        rows = pl.ds(pl.multiple_of(c * C, C), C)
        sl = [slice(h * LANES, (h + 1) * LANES) for h in heads]
        qr = [q_ref[rows, sl[h]] for h in heads]
        q = [x * _sigmoid(x) for x in qr]
        gates = [_hgrn_gates(f_ref[rows, sl[h]], lb_ref[h]) for h in heads]
        log_f, kk = [g[0] for g in gates], [g[1] for g in gates]
        v = [i_ref[rows, sl[h]] for h in heads]
        vb = [x.astype(BF16) for x in v]
        g3 = [_dot(tril, _split3(log_f[h])) for h in heads]
        G = [x[:, 0:LANES] + x[:, LANES:2 * LANES] + x[:, 2 * LANES:3 * LANES] for x in g3]
        st = [st_ref[h] for h in heads]
        inter = [_dot_nt((q[h] * jnp.exp(G[h])).astype(BF16), st[h].astype(BF16)) for h in heads]
        a_off = [jnp.zeros((C, C), F32) for _ in heads]
        for bs, odd, blk in levels:
            refs = [jnp.concatenate([jnp.broadcast_to(G[h][p + bs - 1:p + bs], (2 * bs, LANES))
                                     for p in range(0, C, 2 * bs)], axis=0) for h in heads]
            d = [G[h] - refs[h] for h in heads]
            qp = [(q[h] * jnp.exp(jnp.where(odd, d[h], NEG_INF))).astype(BF16) for h in heads]
            kp = [(kk[h] * jnp.exp(jnp.where(odd, NEG_INF, -d[h]))).astype(BF16) for h in heads]
            a_off = [a_off[h] + jnp.where(blk, _dot_nt(qp[h], kp[h]), 0.0) for h in heads]
        a_rows = [[] for _ in heads]
        for I in range(C // SC):
            lo = I * SC
            for h in heads:
                GI, qI = G[h][lo:lo + SC], q[h][lo:lo + SC]
                dg = jnp.zeros((SC, C), F32)
                for j in range(SC):
                    e = jnp.exp(jnp.minimum(GI - GI[j:j + 1], 0.0))
                    colv = jnp.sum(qI * e * kk[h][lo + j:lo + j + 1], axis=1, keepdims=True)
                    dg = jnp.where(lane_c == lo + j, colv, dg)
                a_rows[h].append(jnp.where(lane_c <= lo + sub_c, dg, 0.0))
        a = [(a_off[h] + jnp.concatenate(a_rows[h], axis=0)).astype(BF16) for h in heads]
        o = [inter[h] + _dot(a[h], vb[h]) for h in heads]
        Gl = [G[h][C - 1:C] for h in heads]
        kd = [(kk[h] * jnp.exp(Gl[h] - G[h])).astype(BF16) for h in heads]
        upd = [_dot(v[h].T.astype(BF16), kd[h]) for h in heads]
        for h in heads:
            st_ref[h] = jnp.exp(Gl[h]) * st[h] + upd[h]
            o_ref[rows, sl[h]] = _hgrn_out(o[h], on, g_ref[rows, sl[h]]).astype(o_ref.dtype)
        return carry

    lax.fori_loop(0, tb // C, chunk, 0)

    @pl.when(t == pl.num_programs(2) - 1)
    def _():
        for h in heads:
            s_ref[0, h] = st_ref[h].T


def hgrn_prompt(proj, lb, on, n, T, heads, tb=256, nh=4):
    nt = T // tb
    hg = heads // nh
    col = lambda k: pl.BlockSpec((tb, nh * LANES), lambda b, h, t, k=k: (b * nt + t, k * hg + h))
    return pl.pallas_call(
        functools.partial(_hgrn_body, tb=tb, nh=nh),
        grid=(n, hg, nt),
        in_specs=[col(0), col(1), col(2), col(3),
                  pl.BlockSpec((nh, 1, LANES), lambda b, h, t: (h, 0, 0)),
                  pl.BlockSpec((1, LANES), lambda b, h, t: (0, 0))],
        out_specs=[pl.BlockSpec((tb, nh * LANES), lambda b, h, t: (b * nt + t, h)),
                   pl.BlockSpec((1, nh, LANES, LANES), lambda b, h, t: (b, h, 0, 0))],
        out_shape=[jax.ShapeDtypeStruct((n * T, heads * LANES), BF16),
                   jax.ShapeDtypeStruct((n, heads, LANES, LANES), F32)],
        scratch_shapes=[pltpu.VMEM((nh, LANES, LANES), F32)],
        compiler_params=_params("parallel", "parallel", "arbitrary"),
        name="hgrn_prompt",
    )(proj, proj, proj, proj, lb.reshape(heads, 1, LANES), on.reshape(1, LANES))


def _col(eye, row):
    return jnp.sum(eye * row, axis=1, keepdims=True)


def _hgrn_s_body(q_ref, f_ref, i_ref, g_ref, lb_ref, on_ref, s_ref, o_ref, so_ref, *, nb):
    qr = q_ref[...]
    q = qr * _sigmoid(qr)
    log_f, kk = _hgrn_gates(f_ref[...], lb_ref[0])
    v = i_ref[...]
    f = jnp.exp(log_f)
    eye = (lax.broadcasted_iota(jnp.int32, (LANES, LANES), 0)
           == lax.broadcasted_iota(jnp.int32, (LANES, LANES), 1)).astype(F32)
    qf = (q * f).astype(BF16)
    a = jnp.sum(q * kk, axis=1, keepdims=True)
    rows = []
    for b in range(nb):
        S = s_ref[b, 0, 0]
        so_ref[b, 0] = _col(eye, f[b:b + 1]) * S + _col(eye, kk[b:b + 1]) * v[b:b + 1]
        rows.append(_dot(qf, S.astype(BF16))[b:b + 1])
    o = jnp.concatenate(rows, axis=0) + a * v
    o_ref[...] = _hgrn_out(o, on_ref[...], g_ref[...])


def hgrn_sample(proj, lb, on, state, li, heads):
    nb = proj.shape[0]
    col = lambda k: pl.BlockSpec((nb, LANES), lambda h, k=k: (0, k * heads + h))
    return pl.pallas_call(
        functools.partial(_hgrn_s_body, nb=nb),
        grid=(heads,),
        in_specs=[col(0), col(1), col(2), col(3),
                  pl.BlockSpec((1, 1, LANES), lambda h: (h, 0, 0)),
                  pl.BlockSpec((1, LANES), lambda h: (0, 0)),
                  pl.BlockSpec((nb, 1, 1, LANES, LANES), lambda h: (0, li, h, 0, 0))],
        out_specs=[pl.BlockSpec((nb, LANES), lambda h: (0, h)),
                   pl.BlockSpec((nb, 1, LANES, LANES), lambda h: (0, h, 0, 0))],
        out_shape=[jax.ShapeDtypeStruct((nb, heads * LANES), F32),
                   jax.ShapeDtypeStruct((nb, heads, LANES, LANES), F32)],
        compiler_params=_params("parallel"),
        name="hgrn_sample",
    )(proj, proj, proj, proj, lb.reshape(heads, 1, LANES), on.reshape(1, LANES), state)


def _dil_multiplicity(delta):
    c = np.zeros(delta.shape, np.float32)
    for window, dil in DIL_PATTERNS:
        c += ((delta >= 0) & (delta <= window) & (delta % dil == 0)).astype(np.float32)
    return c


def _dil_body(q_ref, k_ref, v_ref, qn_ref, kn_ref, c_ref, o_ref, ko_ref, vo_ref, kn_s, vb_s,
              *, T, tq, W, pad, nh):
    qi = pl.program_id(2)
    span = pad + tq
    sl = [slice(h * LANES, (h + 1) * LANES) for h in range(nh)]

    @pl.when(qi == 0)
    def _():
        kw = kn_ref[...]
        kn_s[:, 0:pad, :] = jnp.zeros((nh, pad, LANES), BF16)
        vb_s[:, 0:pad, :] = jnp.zeros((nh, pad, LANES), BF16)

        def norm(c, carry):
            rows = pl.ds(pl.multiple_of(c * 512, 512), 512)
            dst = pl.ds(pl.multiple_of(pad + c * 512, 512), 512)
            for h in range(nh):
                kn_s[h, dst, :] = _rms(k_ref[rows, sl[h]], kw).astype(BF16)
                vb_s[h, dst, :] = v_ref[rows, sl[h]].astype(BF16)
            return carry

        lax.fori_loop(0, T // 512, norm, 0)
        for h in range(nh):
            ko_ref[0, :, sl[h]] = _rms(k_ref[T - W:, sl[h]], kw)
        vo_ref[0] = v_ref[T - W:, :]

    rows = pl.ds(pl.multiple_of(qi * tq, tq), span)
    q = [(_rms(q_ref[:, sl[h]], qn_ref[...]) * SCALE).astype(BF16) for h in range(nh)]
    sc = [_dot_nt(q[h], kn_s[h, rows, :]) for h in range(nh)]
    c = c_ref[...]
    valid = (c > 0.0) & (lax.broadcasted_iota(jnp.int32, (tq, span), 1) >= pad - qi * tq)
    sc = [jnp.where(valid, x, NEG_INF) for x in sc]
    p = [c * jnp.exp(x - jnp.max(x, axis=1, keepdims=True)) for x in sc]
    o = [_dot(p[h].astype(BF16), vb_s[h, rows, :]) for h in range(nh)]
    for h in range(nh):
        o_ref[:, sl[h]] = (o[h] / jnp.sum(p[h], axis=1, keepdims=True)).astype(o_ref.dtype)


def dil_prompt(proj, qn, kn, n, T, heads, col0, tq=256, nh=2):
    W = min(DIL_MAX_WINDOW, T)
    pad = DIL_MAX_WINDOW
    nt = T // tq
    ctab = jnp.asarray(_dil_multiplicity(np.arange(tq)[:, None] + pad - np.arange(pad + tq)[None, :]))
    hw = nh * LANES
    cb, hg = col0 // hw, heads // nh
    return pl.pallas_call(
        functools.partial(_dil_body, T=T, tq=tq, W=W, pad=pad, nh=nh),
        grid=(n, hg, nt),
        in_specs=[pl.BlockSpec((tq, hw), lambda b, h, t: (b * nt + t, cb + h)),
                  pl.BlockSpec((T, hw), lambda b, h, t: (b, cb + hg + h)),
                  pl.BlockSpec((T, hw), lambda b, h, t: (b, cb + 2 * hg + h)),
                  pl.BlockSpec((1, LANES), lambda b, h, t: (0, 0)),
                  pl.BlockSpec((1, LANES), lambda b, h, t: (0, 0)),
                  pl.BlockSpec((tq, pad + tq), lambda b, h, t: (0, 0))],
        out_specs=[pl.BlockSpec((tq, hw), lambda b, h, t: (b * nt + t, h)),
                   pl.BlockSpec((1, W, hw), lambda b, h, t: (b, 0, h)),
                   pl.BlockSpec((1, W, hw), lambda b, h, t: (b, 0, h))],
        out_shape=[jax.ShapeDtypeStruct((n * T, heads * LANES), BF16),
                   jax.ShapeDtypeStruct((n, W, heads * LANES), F32),
                   jax.ShapeDtypeStruct((n, W, heads * LANES), F32)],
        scratch_shapes=[pltpu.VMEM((nh, T + pad, LANES), BF16), pltpu.VMEM((nh, T + pad, LANES), BF16)],
        compiler_params=_params("parallel", "parallel", "arbitrary"),
        name="dil_prompt",
    )(proj, proj, proj, qn.reshape(1, LANES), kn.reshape(1, LANES), ctab)


def _shift_in(buf, new_row):
    n = buf.shape[0]
    rolled = pltpu.roll(buf, n - 1, 0)
    return jnp.where(lax.broadcasted_iota(jnp.int32, buf.shape, 0) == n - 1, new_row, rolled)


def _dil_s_body(q_ref, k_ref, v_ref, qn_ref, kn_ref, c_ref, *rest, heads, c_new, li, n_l, has_prev):
    n_src = 1 if has_prev else n_l
    ck_refs, cv_refs = rest[:n_src], rest[n_src:2 * n_src]
    o_ref, ko_hbm, vo_hbm, m_s, l_s, acc_s, new_s, sem = rest[2 * n_src + (2 if has_prev else 0):]
    ck_ref, cv_ref = (ck_refs[0], cv_refs[0]) if has_prev else (ck_refs[li], cv_refs[li])
    H = heads
    b, c = pl.program_id(0), pl.program_id(1)
    nrows, cr = ko_hbm.shape[2], ck_ref.shape[2]
    sub = lax.broadcasted_iota(jnp.int32, (H, LANES), 0)

    def heads_on_rows(ref):
        out = jnp.zeros((H, LANES), F32)
        for h in range(H):
            out = jnp.where(sub == h, ref[:, h * LANES:(h + 1) * LANES], out)
        return out

    q8 = _rms(heads_on_rows(q_ref), qn_ref[...]) * SCALE
    kn8 = _rms(heads_on_rows(k_ref), kn_ref[...])
    v8 = heads_on_rows(v_ref)

    def chunk_copies(first):
        if first:
            src, dst = pl.ds(H, cr - H), pl.ds(0, cr - H)
        else:
            src, dst = pl.ds(0, cr), pl.ds(pl.multiple_of(c * cr - H, H), cr)
        out = []
        for l in range(0 if has_prev else n_l):
            out.append(pltpu.make_async_copy(ck_refs[l].at[0, 0, src], ko_hbm.at[b, l, dst], sem.at[0]))
            out.append(pltpu.make_async_copy(cv_refs[l].at[0, 0, src], vo_hbm.at[b, l, dst], sem.at[1]))
        return out

    def tail_copies():
        tail = pl.ds(nrows - H, H)
        out = [pltpu.make_async_copy(new_s.at[0], ko_hbm.at[b, li, tail], sem.at[2]),
               pltpu.make_async_copy(new_s.at[1], vo_hbm.at[b, li, tail], sem.at[3])]
        for l in range(0 if has_prev else n_l):
            if l != li:
                out.append(pltpu.make_async_copy(new_s.at[2], ko_hbm.at[b, l, tail], sem.at[2]))
                out.append(pltpu.make_async_copy(new_s.at[2], vo_hbm.at[b, l, tail], sem.at[3]))
        return out

    @pl.when(c == 0)
    def _():
        m_s[...] = jnp.full_like(m_s, NEG_INF)
        l_s[...] = jnp.zeros_like(l_s)
        acc_s[...] = jnp.zeros_like(acc_s)
        for cp in chunk_copies(True):
            cp.start()

    @pl.when(c > 0)
    def _():
        for cp in chunk_copies(False):
            cp.start()

    kc, vc = ck_ref[0, 0], cv_ref[0, 0]
    s = _dot_nt(q8.astype(BF16), kc.astype(BF16))
    own = lax.broadcasted_iota(jnp.int32, s.shape, 1) % H == lax.broadcasted_iota(jnp.int32, s.shape, 0)
    cm = jnp.where(own, c_ref[0], 0.0)
    s = jnp.where(cm > 0.0, s, NEG_INF)
    m_old = m_s[...]
    m_new = jnp.maximum(m_old, jnp.max(s, axis=1, keepdims=True))
    alpha = jnp.exp(m_old - m_new)
    p = cm * jnp.exp(s - m_new[:, 0:1])
    l_s[...] = alpha * l_s[...] + jnp.sum(p, axis=1, keepdims=True)
    acc_s[...] = alpha * acc_s[...] + _dot(p.astype(BF16), vc.astype(BF16))
    m_s[...] = m_new

    @pl.when(c == 0)
    def _():
        for cp in chunk_copies(True):
            cp.wait()

    @pl.when(c > 0)
    def _():
        for cp in chunk_copies(False):
            cp.wait()

    @pl.when(c == pl.num_programs(1) - 1)
    def _():
        s_new = jnp.sum(q8 * kn8, axis=1, keepdims=True)
        m_f = jnp.maximum(m_s[...], s_new)
        a = jnp.exp(m_s[...] - m_f)
        p_new = c_new * jnp.exp(s_new - m_f)
        o = (a * acc_s[...] + p_new * v8) / (a * l_s[...] + p_new)
        for h in range(H):
            o_ref[:, h * LANES:(h + 1) * LANES] = o[h:h + 1]
        new_s[0] = kn8
        new_s[1] = v8
        new_s[2] = jnp.zeros((H, LANES), F32)
        for cp in tail_copies():
            cp.start()
        for cp in tail_copies():
            cp.wait()


def dil_sample(proj, qn, kn, cache_k, cache_v, li, heads, col0, prev=None, chunk=512):
    nb, n_l, Wb = cache_k.shape[0], cache_k.shape[1], cache_k.shape[2]
    H = heads
    ck = cache_k.reshape(nb, n_l, Wb * H, LANES)
    cv = cache_v.reshape(nb, n_l, Wb * H, LANES)
    nch = Wb // chunk
    c_buf = jnp.asarray(np.repeat(_dil_multiplicity(Wb - np.arange(Wb)), H).reshape(nch, 1, chunk * H))
    c_new = float(_dil_multiplicity(np.zeros((1,), np.int64))[0])
    hw = H * LANES
    proj = proj.reshape(nb, 1, -1)
    col = lambda k: pl.BlockSpec((None, 1, hw), lambda b, c, k=k: (b, 0, col0 // hw + k))
    vec = pl.BlockSpec((1, LANES), lambda b, c: (0, 0))
    cache = lambda l: pl.BlockSpec((1, 1, chunk * H, LANES), lambda b, c, l=l: (b, l, c, 0))
    anyspec = pl.BlockSpec(memory_space=pl.ANY)
    full = jax.ShapeDtypeStruct((nb, n_l, Wb * H, LANES), F32)
    layers = [li] if prev is not None else list(range(n_l))
    args = [proj, proj, proj, qn.reshape(1, LANES), kn.reshape(1, LANES), c_buf]
    args += [ck] * len(layers) + [cv] * len(layers)
    in_specs = [col(0), col(1), col(2), vec, vec, pl.BlockSpec((1, 1, chunk * H), lambda b, c: (c, 0, 0))]
    in_specs += [cache(l) for l in layers] * 2
    aliases = {}
    if prev is not None:
        aliases = {len(args): 1, len(args) + 1: 2}
        args += list(prev)
        in_specs += [anyspec, anyspec]
    o, ok, ov = pl.pallas_call(
        functools.partial(_dil_s_body, heads=H, c_new=c_new, li=li, n_l=n_l, has_prev=prev is not None),
        grid=(nb, nch),
        in_specs=in_specs,
        out_specs=[pl.BlockSpec((None, 1, hw), lambda b, c: (b, 0, 0)), anyspec, anyspec],
        out_shape=[jax.ShapeDtypeStruct((nb, 1, hw), F32), full, full],
        scratch_shapes=[pltpu.VMEM((H, LANES), F32), pltpu.VMEM((H, LANES), F32), pltpu.VMEM((H, LANES), F32),
                        pltpu.VMEM((3, H, LANES), F32), pltpu.SemaphoreType.DMA((4,))],
        input_output_aliases=aliases,
        compiler_params=_params("arbitrary", "arbitrary"),
        name="dil_sample",
    )(*args)
    return o.reshape(nb, hw), ok, ov


def _gelu(x):
    return 0.5 * x * (1.0 + jnp.tanh(0.7978845608028654 * (x + 0.044715 * x * x * x)))


def _compress_rows(load_j, pe_ref, w1_ref, w2_ref, rows):
    acc = jnp.zeros((rows, LANES), F32)
    for j in range(CMP_LEN):
        acc = acc + _dot((load_j(j) + pe_ref[j:j + 1, :]).astype(BF16), w1_ref[j])
    return _dot(_gelu(acc).astype(BF16), w2_ref[...])


def _nsa_prep_body(kc_ref, vc_ref, ks_ref, vs_ref, kw_ref, vw_ref, knorm_ref, pek_ref, pev_ref,
                   w1k_ref, w2k_ref, w1v_ref, w2v_ref,
                   kcmp_ref, vcmp_ref, ksb_ref, ksf_ref, vst_ref, kwb_ref, kwf_ref, vwt_ref, *, T):
    nblk = T // CMP_LEN
    kcmp = _compress_rows(lambda j: kc_ref[pl.ds(j, nblk, stride=CMP_LEN), :], pek_ref, w1k_ref, w2k_ref, nblk)
    kcmp_ref[0, 0] = _rms(kcmp, knorm_ref[0:1, :])
    vcmp_ref[0, 0] = _compress_rows(lambda j: vc_ref[pl.ds(j, nblk, stride=CMP_LEN), :], pev_ref, w1v_ref, w2v_ref, nblk)
    ks_w, kw_w = knorm_ref[1:2, :], knorm_ref[2:3, :]

    kwb_ref[0, 0, 0:WIN, :] = jnp.zeros((WIN, LANES), BF16)
    for i in range(WIN // LANES):
        vwt_ref[0, 0, i] = jnp.zeros((LANES, LANES), BF16)

    def tile(c, carry):
        rows = pl.ds(pl.multiple_of(c * LANES, LANES), LANES)
        ksn = _rms(ks_ref[rows, :], ks_w)
        ksf_ref[0, rows, :] = ksn
        ksb_ref[0, 0, rows, :] = ksn.astype(BF16)
        kwb_ref[0, 0, pl.ds(pl.multiple_of(WIN + c * LANES, LANES), LANES), :] = _rms(kw_ref[rows, :], kw_w).astype(BF16)
        vwt_ref[0, 0, WIN // LANES + c] = vw_ref[rows, :].T.astype(BF16)
        return carry

    lax.fori_loop(0, T // LANES, tile, 0)

    def chunk(c, carry):
        rows = pl.ds(pl.multiple_of(c * SLC_CHUNK, SLC_CHUNK), SLC_CHUNK)
        vst_ref[0, 0, c] = vs_ref[rows, :].T.astype(BF16)
        return carry

    lax.fori_loop(0, T // SLC_CHUNK, chunk, 0)
    ww = min(WIN, T)
    kwf_ref[0] = _rms(kw_ref[T - ww:, :], kw_w)


def nsa_prep(proj, knorm, pe_k, pe_v, w1k, w2k, w1v, w2v, n, T, hk):
    cb = (hk * C_GROUP * HEAD_DIM) // LANES
    nblk = T // CMP_LEN
    nt = T // LANES
    ww = min(WIN, T)
    col = lambda k: pl.BlockSpec((T, LANES), lambda b, h, k=k: (b, cb + k * hk + h))
    full = lambda a: pl.BlockSpec(a.shape, lambda b, h: (0,) * a.ndim)
    per = lambda *s: pl.BlockSpec((1, 1) + s, lambda b, h: (b, h) + (0,) * len(s))
    return pl.pallas_call(
        functools.partial(_nsa_prep_body, T=T),
        grid=(n, hk),
        in_specs=[col(0), col(1), col(2), col(3), col(4), col(5),
                  full(knorm), full(pe_k), full(pe_v), full(w1k), full(w2k), full(w1v), full(w2v)],
        out_specs=[per(nblk, LANES), per(nblk, LANES), per(T, LANES),
                   pl.BlockSpec((1, T, LANES), lambda b, h: (b, 0, h)),
                   per(T // SLC_CHUNK, LANES, SLC_CHUNK), per(T + WIN, LANES),
                   pl.BlockSpec((1, ww, LANES), lambda b, h: (b, 0, h)),
                   per(nt + WIN // LANES, LANES, LANES)],
        out_shape=[jax.ShapeDtypeStruct((n, hk, nblk, LANES), F32),
                   jax.ShapeDtypeStruct((n, hk, nblk, LANES), F32),
                   jax.ShapeDtypeStruct((n, hk, T, LANES), BF16),
                   jax.ShapeDtypeStruct((n, T, hk * LANES), F32),
                   jax.ShapeDtypeStruct((n, hk, T // SLC_CHUNK, LANES, SLC_CHUNK), BF16),
                   jax.ShapeDtypeStruct((n, hk, T + WIN, LANES), BF16),
                   jax.ShapeDtypeStruct((n, ww, hk * LANES), F32),
                   jax.ShapeDtypeStruct((n, hk, nt + WIN // LANES, LANES, LANES), BF16)],
        compiler_params=_params("parallel", "parallel"),
        name="nsa_prep",
    )(proj, proj, proj, proj, proj, proj, knorm, pe_k, pe_v, w1k, w2k, w1v, w2v)


def _tile4(x):
    return jnp.concatenate([x] * C_GROUP, axis=1)


def _nsa_body(q_ref, gate_ref, qn_ref, kcmp_ref, vcmp_ref, ks_ref, vst_ref, kw_ref, vwt_ref, o_ref,
              vct_s, pb_s, sel_s, gt_s, *, tq, nblk, nslc, hk_n, hpb):
    G = C_GROUP
    HB = range(hpb)
    gw = G * LANES
    hk0 = pl.program_id(1) * hpb
    qi = pl.program_id(2)
    t0 = qi * tq

    @pl.when(qi == 0)
    def _():
        for h in HB:
            vct_s[h] = vcmp_ref[0, h].T.astype(BF16)

    qw = qn_ref[...]
    q4 = [jnp.concatenate([(_rms(q_ref[:, h * gw + g * LANES:h * gw + (g + 1) * LANES], qw) * SCALE).astype(BF16)
                           for g in range(G)], axis=0) for h in HB]

    st = [_dot_nt(kcmp_ref[0, h].astype(BF16), q4[h]) for h in HB]
    blk = lax.broadcasted_iota(jnp.int32, (nblk, G * tq), 0)
    tpos = t0 + (lax.broadcasted_iota(jnp.int32, (nblk, G * tq), 1) & (tq - 1))
    valid = (blk + 1) * CMP_LEN - 1 <= tpos
    st = [jnp.where(valid, x, NEG_INF) for x in st]
    p = [jnp.where(valid, jnp.exp(x - jnp.max(x, axis=0, keepdims=True)), 0.0) for x in st]
    p = [x / jnp.maximum(jnp.sum(x, axis=0, keepdims=True), 1.0) for x in p]
    o_cmp = [_dot(vct_s[h], p[h].astype(BF16)) for h in HB]
    ratio = SLC_BLOCK // CMP_LEN
    imp = []
    for h in HB:
        pb = p[h][:, 0:tq]
        for g in range(1, G):
            pb = pb + p[h][:, g * tq:(g + 1) * tq]
        parts = []
        for i in range(tq // LANES):
            pb_s[h, i] = pb[:, i * LANES:(i + 1) * LANES]
            part = pb_s[h, i, pl.ds(0, nslc, stride=ratio), :]
            for r in range(1, ratio):
                part = part + pb_s[h, i, pl.ds(r, nslc, stride=ratio), :]
            parts.append(part)
        imp.append(jnp.concatenate(parts, axis=1))

    jb = lax.broadcasted_iota(jnp.int32, (nslc, tq), 0)
    tp = t0 + lax.broadcasted_iota(jnp.int32, (nslc, tq), 1)
    cur = tp // SLC_BLOCK
    forced = (jb == 0) | (jb == cur) | (jb == cur - 1)
    in_past = jb * SLC_BLOCK <= tp
    score = [jnp.where(in_past, jnp.where(forced, FORCE_SCORE, x), -FORCE_SCORE) for x in imp]
    rank = [jnp.zeros((nslc, tq), F32) for _ in HB]
    for jp in range(nslc):
        later = jb > jp
        for h in HB:
            row = score[h][jp:jp + 1, :]
            rank[h] = rank[h] + ((row > score[h]) | ((row == score[h]) & later)).astype(F32)
    for h in HB:
        sel_s[h] = jnp.where(rank[h] < float(min(SLC_TOPN, nslc)), 0.0, NEG_INF)

    KC = SLC_CHUNK
    per_chunk = KC // SLC_BLOCK

    def slc_step(kc, carry, causal):
        m, l, acc = carry
        rows = pl.ds(pl.multiple_of(kc * KC, KC), KC)
        s = [_dot_nt(ks_ref[0, h, rows, :], q4[h]) for h in HB]
        bias = [jnp.concatenate([jnp.broadcast_to(sel_s[h, pl.ds(kc * per_chunk + r, 1), :], (SLC_BLOCK, tq))
                                 for r in range(per_chunk)], axis=0) for h in HB]
        if causal:
            kpos = kc * KC + lax.broadcasted_iota(jnp.int32, (KC, tq), 0)
            seen = kpos <= t0 + lax.broadcasted_iota(jnp.int32, (KC, tq), 1)
            bias = [jnp.where(seen, x, NEG_INF) for x in bias]
        s = [s[h] + _tile4(bias[h]) for h in HB]
        m_new = [jnp.maximum(m[h], jnp.max(s[h], axis=0, keepdims=True)) for h in HB]
        alpha = [jnp.exp(m[h] - m_new[h]) for h in HB]
        pp = [jnp.exp(s[h] - m_new[h]) for h in HB]
        pv = [_dot(vst_ref[0, h, kc], pp[h].astype(BF16)) for h in HB]
        l = [alpha[h] * l[h] + jnp.sum(pp[h], axis=0, keepdims=True) for h in HB]
        acc = [alpha[h] * acc[h] + pv[h] for h in HB]
        return m_new, l, acc

    init = ([jnp.full((1, G * tq), NEG_INF, F32) for _ in HB], [jnp.zeros((1, G * tq), F32) for _ in HB],
            [jnp.zeros((LANES, G * tq), F32) for _ in HB])
    last = (t0 + tq - 1) // KC
    carry = lax.fori_loop(0, last, functools.partial(slc_step, causal=False), init)
    _, l_s, acc_s = slc_step(last, carry, causal=True)
    o_slc = [acc_s[h] / l_s[h] for h in HB]

    wspan = WIN + tq
    wsub = lax.broadcasted_iota(jnp.int32, (wspan, tq), 0)
    dist = lax.broadcasted_iota(jnp.int32, (wspan, tq), 1) + WIN - wsub
    wbias = _tile4(jnp.where((dist >= 0) & (dist <= WIN) & (wsub >= WIN - t0), 0.0, NEG_INF))
    wrows = pl.ds(pl.multiple_of(t0, tq), wspan)
    sw = [_dot_nt(kw_ref[0, h, wrows, :], q4[h]) + wbias for h in HB]
    pw = [jnp.exp(x - jnp.max(x, axis=0, keepdims=True)) for x in sw]
    vw_t = [jnp.concatenate([vwt_ref[0, h, qi * (tq // LANES) + i] for i in range(wspan // LANES)], axis=1)
            for h in HB]
    o_win = [_dot(vw_t[h], pw[h].astype(BF16)) / jnp.sum(pw[h], axis=0, keepdims=True) for h in HB]

    gt_s[...] = _sigmoid(gate_ref[...]).T
    nh = G * hk_n
    for h in HB:
        for g in range(G):
            sl = slice(g * tq, (g + 1) * tq)
            head = (hk0 + h) * G + g
            g0 = gt_s[pl.ds(head, 1), :]
            g1 = gt_s[pl.ds(nh + head, 1), :]
            g2 = gt_s[pl.ds(2 * nh + head, 1), :]
            o = g0 * o_cmp[h][:, sl] + g1 * o_slc[h][:, sl] + g2 * o_win[h][:, sl]
            o_ref[:, h * gw + g * LANES:h * gw + (g + 1) * LANES] = o.T.astype(o_ref.dtype)


def nsa_prompt(proj, qn, kcmp, vcmp, ksb, vst, kwb, vwt, n, T, hk, gate_col, tq=256, hpb=2):
    nblk, nslc, nt = T // CMP_LEN, T // SLC_BLOCK, T // tq
    gw = hpb * C_GROUP * LANES
    per = lambda *s: pl.BlockSpec((1, hpb) + s, lambda b, h, t: (b, h) + (0,) * len(s))
    return pl.pallas_call(
        functools.partial(_nsa_body, tq=tq, nblk=nblk, nslc=nslc, hk_n=hk, hpb=hpb),
        grid=(n, hk // hpb, nt),
        in_specs=[pl.BlockSpec((tq, gw), lambda b, h, t: (b * nt + t, h)),
                  pl.BlockSpec((tq, LANES), lambda b, h, t: (b * nt + t, gate_col // LANES)),
                  pl.BlockSpec((1, LANES), lambda b, h, t: (0, 0)),
                  per(nblk, LANES), per(nblk, LANES), per(T, LANES), per(T // SLC_CHUNK, LANES, SLC_CHUNK),
                  per(T + WIN, LANES), per((T + WIN) // LANES, LANES, LANES)],
        out_specs=pl.BlockSpec((tq, gw), lambda b, h, t: (b * nt + t, h)),
        out_shape=jax.ShapeDtypeStruct((n * T, hk * C_GROUP * LANES), BF16),
        scratch_shapes=[pltpu.VMEM((hpb, LANES, nblk), BF16), pltpu.VMEM((hpb, tq // LANES, nblk, LANES), F32),
                        pltpu.VMEM((hpb, nslc, tq), F32), pltpu.VMEM((LANES, tq), F32)],
        compiler_params=_params("parallel", "parallel", "arbitrary"),
        name="nsa_prompt",
    )(proj, proj, qn.reshape(1, LANES), kcmp, vcmp, ksb, vst, kwb, vwt)


def _cmp_pages_body(pt_ref, ck_hbm, cv_hbm, knorm_ref, pek_ref, pev_ref, w1k_ref, w2k_ref, w1v_ref, w2v_ref,
                    ko_ref, vo_ref, kbuf, vbuf, fold_s, sem, *, li, P, hk, rows_per_page):
    s = pl.program_id(0)
    ns = pl.num_programs(0)
    gpp = rows_per_page // hk // CMP_LEN

    def copies(step, slot):
        out = []
        for p in range(P):
            page = pt_ref[step * P + p]
            dst = pl.ds(p * gpp, gpp)
            out.append(pltpu.make_async_copy(ck_hbm.at[page, li], kbuf.at[slot, dst], sem.at[0, slot]))
            out.append(pltpu.make_async_copy(cv_hbm.at[page, li], vbuf.at[slot, dst], sem.at[1, slot]))
        return out

    @pl.when(s == 0)
    def _():
        for c in copies(0, 0):
            c.start()

    @pl.when(s + 1 < ns)
    def _():
        for c in copies(s + 1, (s + 1) % 2):
            c.start()

    slot = s % 2
    for c in copies(s, slot):
        c.wait()

    groups = P * (rows_per_page // hk // CMP_LEN)
    tpv = 8 // hk
    own = [lax.broadcasted_iota(jnp.int32, (groups * 8, LANES), 0) % 8 // hk == u for u in range(tpv)]

    def compress(buf, pe_ref, w1_ref, w2_ref):
        acc = jnp.zeros((groups * 8, LANES), F32)
        for jp in range(CMP_LEN // tpv):
            x = buf[slot, :, pl.ds(jp * 8, 8), :] + pe_ref[jp]
            y = _dot(x.reshape(groups * 8, LANES).astype(BF16), w1_ref[jp])
            part = y[:, 0:LANES]
            for u in range(1, tpv):
                part = jnp.where(own[u], y[:, u * LANES:(u + 1) * LANES], part)
            acc = acc + part
        tot = acc
        for u in range(1, tpv):
            tot = tot + pltpu.roll(acc, u * hk, 0)
        return _dot(_gelu(tot).astype(BF16), w2_ref[...])

    fold_s[0] = _rms(compress(kbuf, pek_ref, w1k_ref, w2k_ref), knorm_ref[0:1, :])
    fold_s[1] = compress(vbuf, pev_ref, w1v_ref, w2v_ref)
    for h in range(hk):
        rows = pl.ds(8 - hk + h, groups, stride=8)
        ko_ref[0, h] = fold_s[0, rows, :]
        vo_ref[0, h] = fold_s[1, rows, :]


def cmp_pages(page_table, cache_k, cache_v, li, knorm, pe_k, pe_v, w1k, w2k, w1v, w2v, P=16):
    nb, n_pages = page_table.shape
    n_pool, n_l, page, hk, dh = cache_k.shape
    rpp = page * hk
    gpp, grows = page // CMP_LEN, CMP_LEN * hk
    ck = cache_k.reshape(n_pool, n_l, gpp, grows, dh)
    cv = cache_v.reshape(n_pool, n_l, gpp, grows, dh)
    tpv = 8 // hk
    slab_pe = lambda pe: jnp.repeat(pe, hk, axis=0).reshape(CMP_LEN // tpv, 8, dh)
    slab_w = lambda w: w.reshape(CMP_LEN // tpv, tpv, dh, dh).transpose(0, 2, 1, 3).reshape(CMP_LEN // tpv, dh, tpv * dh)
    pe_k, pe_v, w1k, w1v = slab_pe(pe_k), slab_pe(pe_v), slab_w(w1k), slab_w(w1v)
    steps_per_b = n_pages // P
    blocks = P * gpp
    nblk = n_pages * gpp
    full = lambda a: pl.BlockSpec(a.shape, lambda s, pt: (0,) * a.ndim)
    out_spec = pl.BlockSpec((1, hk, blocks, dh), lambda s, pt: (s // steps_per_b, 0, s % steps_per_b, 0))
    grid_spec = pltpu.PrefetchScalarGridSpec(
        num_scalar_prefetch=1,
        grid=(nb * steps_per_b,),
        in_specs=[pl.BlockSpec(memory_space=pl.ANY), pl.BlockSpec(memory_space=pl.ANY),
                  full(knorm), full(pe_k), full(pe_v), full(w1k), full(w2k), full(w1v), full(w2v)],
        out_specs=[out_spec, out_spec],
        scratch_shapes=[pltpu.VMEM((2, blocks, grows, dh), F32), pltpu.VMEM((2, blocks, grows, dh), F32),
                        pltpu.VMEM((2, blocks * 8, dh), F32), pltpu.SemaphoreType.DMA((2, 2))])
    return pl.pallas_call(
        functools.partial(_cmp_pages_body, li=li, P=P, hk=hk, rows_per_page=rpp),
        grid_spec=grid_spec,
        out_shape=[jax.ShapeDtypeStruct((nb, hk, nblk, dh), F32)] * 2,
        compiler_params=_params("arbitrary"),
        name="cmp_pages",
    )(page_table.reshape(-1), ck, cv, knorm, pe_k, pe_v, w1k, w2k, w1v, w2v)


def _nsa_s_select_body(q_ref, kc_ref, vc_ref, ks_ref, qn_ref, knorm_ref, pek_ref, pev_ref,
                       w1k_ref, w2k_ref, w1v_ref, w2v_ref, kcmp_ref, vcmp_ref,
                       qo_ref, ocmp_ref, sel_ref, kso_ref, *, hk, qpos, nblk):
    G = C_GROUP
    row = slice(None)
    qw = qn_ref[...]
    nslc = (nblk + 1 + 1) // 2
    lanes_blk = lax.broadcasted_iota(jnp.int32, (8, nblk), 1)
    valid = (lanes_blk + 1) * CMP_LEN - 1 <= qpos
    valid_x = (jnp.full((8, 1), (nblk + 1) * CMP_LEN - 1, jnp.int32) <= qpos)
    pair = (lax.broadcasted_iota(jnp.int32, (nblk, nblk // 2), 0) // 2
            == lax.broadcasted_iota(jnp.int32, (nblk, nblk // 2), 1)).astype(F32)
    sub8 = lax.broadcasted_iota(jnp.int32, (8, 1), 0)
    pe_rest_k = jnp.zeros((8, LANES), F32)
    pe_rest_v = jnp.zeros((8, LANES), F32)
    for j in range(1, CMP_LEN):
        pe_rest_k = pe_rest_k + _dot(jnp.broadcast_to(pek_ref[j:j + 1, :], (8, LANES)).astype(BF16), w1k_ref[j])
        pe_rest_v = pe_rest_v + _dot(jnp.broadcast_to(pev_ref[j:j + 1, :], (8, LANES)).astype(BF16), w1v_ref[j])
    for h in range(hk):
        qs = [_rms(q_ref[row, (h * G + g) * LANES:(h * G + g + 1) * LANES], qw) * SCALE for g in range(G)]
        q8 = jnp.concatenate(qs + [jnp.zeros((8 - G, LANES), F32)], axis=0)
        qo_ref[0, h] = q8
        kso_ref[0, h] = jnp.broadcast_to(_rms(ks_ref[row, h * LANES:(h + 1) * LANES], knorm_ref[1:2, :]), (8, LANES))
        xk = jnp.broadcast_to(kc_ref[row, h * LANES:(h + 1) * LANES] + pek_ref[0:1, :], (8, LANES))
        xv = jnp.broadcast_to(vc_ref[row, h * LANES:(h + 1) * LANES] + pev_ref[0:1, :], (8, LANES))
        k_x = _dot(_gelu(_dot(xk.astype(BF16), w1k_ref[0]) + pe_rest_k).astype(BF16), w2k_ref[...])
        k_x = _rms(k_x, knorm_ref[0:1, :])
        v_x = _dot(_gelu(_dot(xv.astype(BF16), w1v_ref[0]) + pe_rest_v).astype(BF16), w2v_ref[...])
        s = jnp.where(valid, _dot_nt(q8.astype(BF16), kcmp_ref[0, h].astype(BF16)), NEG_INF)
        s_x = jnp.where(valid_x, jnp.sum(q8 * k_x, axis=1, keepdims=True), NEG_INF)
        m = jnp.maximum(jnp.max(s, axis=1, keepdims=True), s_x)
        p = jnp.where(valid, jnp.exp(s - m), 0.0)
        p_x = jnp.where(valid_x, jnp.exp(s_x - m), 0.0)
        den = jnp.maximum(jnp.sum(p, axis=1, keepdims=True) + p_x, 1.0)
        p = jnp.where(sub8 < G, p / den, 0.0)
        p_x = jnp.where(sub8 < G, p_x / den, 0.0)
        ocmp_ref[0, h] = _dot(p.astype(BF16), vcmp_ref[0, h].astype(BF16)) + p_x * v_x
        pb = jnp.sum(p, axis=0, keepdims=True)
        pb_x = jnp.sum(p_x, axis=0, keepdims=True)
        imp = _dot(jnp.broadcast_to(pb, (8, nblk)), pair, precision=HIGHEST)[0:1]
        lane = lax.broadcasted_iota(jnp.int32, (1, LANES), 1)
        tail = jnp.where(lane == 0, pb_x, -jnp.inf)
        imp = jnp.concatenate([imp, tail], axis=1)
        width = imp.shape[1]
        jb = lax.broadcasted_iota(jnp.int32, (1, width), 1)
        cur = qpos // SLC_BLOCK
        forced = (jb == 0) | (jb == cur) | (jb == cur - 1)
        score = jnp.where(jb * SLC_BLOCK <= qpos, jnp.where(forced, FORCE_SCORE, imp), -FORCE_SCORE)
        score = jnp.where(jb < nslc, score, -jnp.inf)
        sel = jnp.zeros((1, LANES), jnp.int32)
        for r in range(SLC_TOPN):
            best = jnp.max(score, axis=1, keepdims=True)
            idx = jnp.min(jnp.where(score == best, jb, width), axis=1, keepdims=True)
            sel = jnp.where(lane == r, idx, sel)
            score = jnp.where(jb == idx, -jnp.inf, score)
        sel_ref[0, h] = jnp.broadcast_to(sel, (8, LANES))


def nsa_sample_select(proj, qn, knorm, pe_k, pe_v, w1k, w2k, w1v, w2v, kcmp, vcmp, hk, qpos):
    nb = proj.shape[0]
    nblk = kcmp.shape[2]
    heads = hk * C_GROUP
    cq, ckv = heads * LANES, hk * LANES
    full = lambda a: pl.BlockSpec(a.shape, lambda b: (0,) * a.ndim)
    per = lambda *s: pl.BlockSpec((1,) + s, lambda b: (b,) + (0,) * len(s))
    colspec = lambda c0, w: pl.BlockSpec((None, 1, w), lambda b: (b, 0, c0 // w))
    proj = proj.reshape(nb, 1, -1)
    out8 = jax.ShapeDtypeStruct((nb, hk, 8, LANES), F32)
    return pl.pallas_call(
        functools.partial(_nsa_s_select_body, hk=hk, qpos=qpos, nblk=nblk),
        grid=(nb,),
        in_specs=[colspec(0, cq), colspec(cq, ckv), colspec(cq + ckv, ckv), colspec(cq + 2 * ckv, ckv),
                  pl.BlockSpec((1, LANES), lambda b: (0, 0)),
                  full(knorm), full(pe_k), full(pe_v), full(w1k), full(w2k), full(w1v), full(w2v),
                  per(hk, nblk, LANES), per(hk, nblk, LANES)],
        out_specs=[per(hk, 8, LANES)] * 4,
        out_shape=[out8, out8, jax.ShapeDtypeStruct((nb, hk, 8, LANES), jnp.int32), out8],
        compiler_params=_params("arbitrary"),
        name="nsa_sample_select",
    )(proj, proj, proj, proj, qn.reshape(1, LANES), knorm, pe_k, pe_v, w1k, w2k, w1v, w2v, kcmp, vcmp)


def _nsa_s_slc_body(sel_ref, pt_ref, q_ref, ksn_ref, vsn_ref, ck_hbm, cv_hbm, o_ref, kbuf, vbuf, sem,
                    *, hk, li, n_past_blocks, per_page, n_pages):
    s = pl.program_id(0)
    ns = pl.num_programs(0)
    rows = SLC_BLOCK * hk

    def copies(step, slot):
        out = []
        for r in range(SLC_TOPN):
            blk = jnp.minimum(sel_ref[step * SLC_TOPN + r], n_past_blocks - 1)
            page = pt_ref[(step // hk) * n_pages + blk // per_page]
            src = pl.ds(pl.multiple_of((blk % per_page) * rows, rows), rows)
            dst = pl.ds(r * rows, rows)
            out.append(pltpu.make_async_copy(ck_hbm.at[page, li, src], kbuf.at[slot, dst], sem.at[0, slot]))
            out.append(pltpu.make_async_copy(cv_hbm.at[page, li, src], vbuf.at[slot, dst], sem.at[1, slot]))
        return out

    @pl.when(s == 0)
    def _():
        for c in copies(0, 0):
            c.start()

    @pl.when(s + 1 < ns)
    def _():
        for c in copies(s + 1, (s + 1) % 2):
            c.start()

    slot = s % 2
    for c in copies(s, slot):
        c.wait()

    h = s % hk
    nkeys = SLC_TOPN * SLC_BLOCK
    first = lax.broadcasted_iota(jnp.int32, (SLC_BLOCK, LANES), 0) == 0
    lane = lax.broadcasted_iota(jnp.int32, (1, nkeys), 1)
    k_new, v_new = ksn_ref[0, 0, 0:1, :], vsn_ref[...]
    ks, vs = [], []
    okf = jnp.ones((1, nkeys), F32)
    for r in range(SLC_TOPN):
        is_new = sel_ref[s * SLC_TOPN + r] >= n_past_blocks
        rws = pl.ds(r * rows + h, SLC_BLOCK, stride=hk)
        ks.append(jnp.where(is_new, jnp.where(first, k_new, 0.0), kbuf[slot, rws, :]))
        vs.append(jnp.where(is_new, jnp.where(first, v_new, 0.0), vbuf[slot, rws, :]))
        okf = jnp.where((lane // SLC_BLOCK == r) & is_new, jnp.where(lane == r * SLC_BLOCK, 1.0, 0.0), okf)
    ok = okf > 0.5
    k = jnp.concatenate(ks, axis=0).astype(BF16)
    v = jnp.concatenate(vs, axis=0).astype(BF16)
    sc = jnp.where(ok, _dot_nt(q_ref[0, 0].astype(BF16), k), NEG_INF)
    p = jnp.where(ok, jnp.exp(sc - jnp.max(sc, axis=1, keepdims=True)), 0.0)
    o_ref[0, 0] = _dot(p.astype(BF16), v) / jnp.sum(p, axis=1, keepdims=True)


def nsa_sample_selected(sel, page_table, q8, ksn, proj, vs_col, cache_k, cache_v, li, hk):
    nb, n_pages = page_table.shape
    n_pool, n_l, page, _, dh = cache_k.shape
    rpp = page * hk
    per_page = page // SLC_BLOCK
    n_past_blocks = n_pages * per_page
    ck = cache_k.reshape(n_pool, n_l, rpp, dh)
    cv = cache_v.reshape(n_pool, n_l, rpp, dh)
    sel_flat = sel[:, :, 0, :SLC_TOPN].reshape(-1)
    proj = proj.reshape(nb, 1, -1)
    per = pl.BlockSpec((1, 1, 8, LANES), lambda s, s_, p_: (s // hk, s % hk, 0, 0))
    buf = pltpu.VMEM((2, SLC_TOPN * SLC_BLOCK * hk, dh), F32)
    grid_spec = pltpu.PrefetchScalarGridSpec(
        num_scalar_prefetch=2,
        grid=(nb * hk,),
        in_specs=[per, per,
                  pl.BlockSpec((None, 1, LANES), lambda s, s_, p_: (s // hk, 0, vs_col // LANES + s % hk)),
                  pl.BlockSpec(memory_space=pl.ANY), pl.BlockSpec(memory_space=pl.ANY)],
        out_specs=per,
        scratch_shapes=[buf, buf, pltpu.SemaphoreType.DMA((2, 2))])
    return pl.pallas_call(
        functools.partial(_nsa_s_slc_body, hk=hk, li=li, n_past_blocks=n_past_blocks, per_page=per_page,
                          n_pages=n_pages),
        grid_spec=grid_spec,
        out_shape=jax.ShapeDtypeStruct((nb, hk, 8, LANES), F32),
        compiler_params=_params("arbitrary"),
        name="nsa_sample_selected",
    )(sel_flat, page_table.reshape(-1), q8, ksn, proj, ck, cv)


def _nsa_s_win_body(q_ref, ocmp_ref, oslc_ref, kw_ref, vw_ref, gate_ref, knorm_ref, wk_ref, wv_ref,
                    o_ref, wko_ref, wvo_ref, *, hk):
    G = C_GROUP
    row = slice(None)
    eye = (lax.broadcasted_iota(jnp.int32, (LANES, LANES), 0)
           == lax.broadcasted_iota(jnp.int32, (LANES, LANES), 1)).astype(F32)
    gcol = _col(eye, _sigmoid(gate_ref[row, :]))
    nh = G * hk
    for h in range(hk):
        lanes = slice(h * LANES, (h + 1) * LANES)
        q8 = q_ref[0, h]
        kn = _rms(kw_ref[row, lanes], knorm_ref[2:3, :])
        v = vw_ref[row, lanes]
        kb, vb = wk_ref[0, 0, :, lanes], wv_ref[0, 0, :, lanes]
        s = _dot_nt(q8.astype(BF16), kb.astype(BF16))
        s_new = jnp.sum(q8 * kn, axis=1, keepdims=True)
        m = jnp.maximum(jnp.max(s, axis=1, keepdims=True), s_new)
        p = jnp.exp(s - m)
        p_new = jnp.exp(s_new - m)
        l = jnp.sum(p, axis=1, keepdims=True) + p_new
        o_win = (_dot(p.astype(BF16), vb.astype(BF16)) + p_new * v) / l
        g0 = gcol[h * G:h * G + 8]
        g1 = gcol[nh + h * G:nh + h * G + 8]
        g2 = gcol[2 * nh + h * G:2 * nh + h * G + 8]
        o = g0 * ocmp_ref[0, h] + g1 * oslc_ref[0, h] + g2 * o_win
        for g in range(G):
            o_ref[row, (h * G + g) * LANES:(h * G + g + 1) * LANES] = o[g:g + 1]
        wko_ref[0, :, lanes] = _shift_in(kb, kn)
        wvo_ref[0, :, lanes] = _shift_in(vb, v)


def nsa_sample_window(q8, ocmp, oslc, proj, kw_col, gate_col, knorm, win_k, win_v, li, hk):
    nb = proj.shape[0]
    Wb = win_k.shape[2]
    ckv = hk * LANES
    wk = win_k.reshape(nb, win_k.shape[1], Wb, ckv)
    wv = win_v.reshape(nb, win_v.shape[1], Wb, ckv)
    per = lambda *s: pl.BlockSpec((1,) + s, lambda b: (b,) + (0,) * len(s))
    proj = proj.reshape(nb, 1, -1)
    cw = hk * C_GROUP * LANES
    o, wko, wvo = pl.pallas_call(
        functools.partial(_nsa_s_win_body, hk=hk),
        grid=(nb,),
        in_specs=[per(hk, 8, LANES), per(hk, 8, LANES), per(hk, 8, LANES),
                  pl.BlockSpec((None, 1, ckv), lambda b: (b, 0, kw_col // ckv)),
                  pl.BlockSpec((None, 1, ckv), lambda b: (b, 0, kw_col // ckv + 1)),
                  pl.BlockSpec((None, 1, LANES), lambda b: (b, 0, gate_col // LANES)),
                  pl.BlockSpec(knorm.shape, lambda b: (0, 0)),
                  pl.BlockSpec((1, 1, Wb, ckv), lambda b: (b, li, 0, 0)),
                  pl.BlockSpec((1, 1, Wb, ckv), lambda b: (b, li, 0, 0))],
        out_specs=[pl.BlockSpec((None, 1, cw), lambda b: (b, 0, 0)),
                   per(Wb, ckv), per(Wb, ckv)],
        out_shape=[jax.ShapeDtypeStruct((nb, 1, cw), F32),
                   jax.ShapeDtypeStruct((nb, Wb, ckv), F32),
                   jax.ShapeDtypeStruct((nb, Wb, ckv), F32)],
        compiler_params=_params("parallel"),
        name="nsa_sample_window",
    )(q8, ocmp, oslc, proj, proj, proj, knorm, wk, wv)
    return o.reshape(nb, cw), wko, wvo


def _pad_cols(w, mult):
    pad = (-w.shape[-1]) % mult
    return jnp.pad(w, ((0, 0),) * (w.ndim - 1) + ((0, pad),)) if pad else w


def kernel(x_prompt, x_sample, state_hgrn, cache_dil_k, cache_dil_v, cache_cmp_k, cache_cmp_v, cache_slc_k, cache_slc_v, cache_win_k, cache_win_v, page_table, norm_mix, norm_mlp, w_in_even, w_out_even, hgrn_lb_logits, hgrn_out_norm, dil_q_norm, dil_k_norm, w_in_odd, w_out_odd, nsa_q_norm, nsa_k_norm, nsa_pe_k, nsa_pe_v, nsa_phi_k1, nsa_phi_k2, nsa_phi_v1, nsa_phi_v2, w_mlp_up, w_mlp_down):
    n, T, D = x_prompt.shape
    nb = x_sample.shape[0]
    assert x_sample.shape[1] == 1
    depth = norm_mix.shape[0]
    a_heads = hgrn_lb_logits.shape[1] // LANES
    b_heads = cache_dil_k.shape[3]
    hk = cache_win_k.shape[3]
    c_heads = hk * C_GROUP
    past_len = page_table.shape[1] * cache_cmp_k.shape[2]
    a_w = a_heads * LANES
    TN = 896
    TM = 512
    TM_IN = 1024 if (n * T) % 1024 == 0 else TM

    lb_cum = jnp.cumsum(jax.nn.softmax(hgrn_lb_logits.astype(F32), axis=0), axis=0)
    lower_bounds = lb_cum - lb_cum[0:1]

    cq, ckv = c_heads * LANES, hk * LANES
    gate_col = cq + 6 * ckv
    gate_w = w_in_odd[:, :, gate_col:].reshape(-1, D, hk, C_GROUP, 3).transpose(0, 1, 4, 2, 3).reshape(-1, D, 3 * c_heads)
    w_in_odd_p = jnp.concatenate([w_in_odd[:, :, :gate_col], _pad_cols(gate_w, LANES)], axis=-1)
    w_in_odd_p = _pad_cols(w_in_odd_p, TN).astype(BF16)
    w_in_even_b = w_in_even.astype(BF16)
    w_out_even_b = w_out_even.astype(BF16)
    w_out_odd_b = w_out_odd.astype(BF16)
    w_up_b = w_mlp_up.astype(BF16)
    w_down_b = w_mlp_down.astype(BF16)
    phi_k1 = nsa_phi_k1.reshape(-1, CMP_LEN, LANES, LANES).astype(BF16)
    phi_v1 = nsa_phi_v1.reshape(-1, CMP_LEN, LANES, LANES).astype(BF16)
    phi_k2 = nsa_phi_k2.astype(BF16)
    phi_v2 = nsa_phi_v2.astype(BF16)

    xp = x_prompt.reshape(n * T, D)
    xs = x_sample.reshape(nb, D)
    outs = {k: [] for k in ("hg_p", "hg_s", "dk_p", "dv_p", "dk_s", "dv_s", "ck_p", "cv_p", "sk_p", "sv_p",
                            "wk_p", "wv_p", "ck_s", "cv_s", "sk_s", "sv_s", "wk_s", "wv_s")}
    dil_bufs = None
    for layer in range(depth):
        li = layer // 2
        if layer % 2 == 0:
            w_in, w_out = w_in_even_b[li], w_out_even_b[li]
            lb, on = lower_bounds[li], hgrn_out_norm[li]
            qn, kn = dil_q_norm[li], dil_k_norm[li]
            pp = rms_matmul(xp, norm_mix[layer], w_in, TM_IN, TN)
            ps = rms_matmul(xs, norm_mix[layer], w_in, nb, TN)
            oa_p, st_p = hgrn_prompt(pp, lb, on, n, T, a_heads)
            ob_p, dk, dv = dil_prompt(pp, qn, kn, n, T, b_heads, 4 * a_w)
            oa_s, st_s = hgrn_sample(ps, lb, on, state_hgrn, li, a_heads)
            ob_s, *dil_bufs = dil_sample(ps, qn, kn, cache_dil_k, cache_dil_v, li, b_heads, 4 * a_w, prev=dil_bufs)
            w_halves = [w_out[:a_w], w_out[a_w:]]
            xp = proj_residual([oa_p, ob_p], w_halves, xp, TM)
            xs = proj_residual([oa_s, ob_s], w_halves, xs, nb)
            outs["hg_p"].append(st_p); outs["hg_s"].append(st_s)
            outs["dk_p"].append(dk.reshape(n, -1, b_heads, LANES)); outs["dv_p"].append(dv.reshape(n, -1, b_heads, LANES))
        else:
            w_in, w_out = w_in_odd_p[li], w_out_odd_b[li]
            knorm = nsa_k_norm[li]
            cmp_w = (nsa_pe_k[li], nsa_pe_v[li], phi_k1[li], phi_k2[li], phi_v1[li], phi_v2[li])
            pp = rms_matmul(xp, norm_mix[layer], w_in, TM_IN, TN)
            ps = rms_matmul(xs, norm_mix[layer], w_in, nb, TN)
            kcmp, vcmp, ksb, ksf, vst, kwb, kwf, vwt = nsa_prep(pp, knorm, *cmp_w, n, T, hk)
            o_p = nsa_prompt(pp, nsa_q_norm[li], kcmp, vcmp, ksb, vst, kwb, vwt, n, T, hk, gate_col)
            kcs, vcs = cmp_pages(page_table, cache_cmp_k, cache_cmp_v, li, knorm, *cmp_w)
            q8, ocmp, sel, ksn = nsa_sample_select(ps, nsa_q_norm[li], knorm, *cmp_w, kcs, vcs, hk, past_len)
            oslc = nsa_sample_selected(sel, page_table, q8, ksn, ps, cq + 3 * ckv, cache_slc_k, cache_slc_v, li, hk)
            o_s, wks, wvs = nsa_sample_window(q8, ocmp, oslc, ps, cq + 4 * ckv, gate_col, knorm,
                                              cache_win_k, cache_win_v, li, hk)
            xp = proj_residual([o_p], [w_out], xp, TM)
            xs = proj_residual([o_s], [w_out], xs, nb)
            kvp = lambda k: pp[:, cq + k * ckv:cq + (k + 1) * ckv].reshape(n, T, hk, LANES)
            kvs = lambda k: ps[:, cq + k * ckv:cq + (k + 1) * ckv].reshape(nb, 1, hk, LANES)
            ww = kwf.shape[1]
            outs["ck_p"].append(kvp(0)); outs["cv_p"].append(kvp(1))
            outs["sk_p"].append(ksf.reshape(n, T, hk, LANES)); outs["sv_p"].append(kvp(3))
            outs["wk_p"].append(kwf.reshape(n, ww, hk, LANES)); outs["wv_p"].append(kvp(5)[:, T - ww:])
            outs["ck_s"].append(kvs(0)); outs["cv_s"].append(kvs(1))
            outs["sk_s"].append(ksn[:, :, 0, :].reshape(nb, 1, hk, LANES)); outs["sv_s"].append(kvs(3))
            outs["wk_s"].append(wks.reshape(nb, -1, hk, LANES)); outs["wv_s"].append(wvs.reshape(nb, -1, hk, LANES))
        xp = mlp_residual(xp, norm_mlp[layer], w_up_b[layer], w_down_b[layer], TM, 512)
        xs = mlp_residual(xs, norm_mlp[layer], w_up_b[layer], w_down_b[layer], nb, 512)
    st = lambda k: jnp.stack(outs[k], axis=1)
    return (xp.reshape(n, T, D), xs.reshape(nb, 1, D),
            st("hg_p"), st("hg_s"), st("dk_p"), st("dv_p"),
            dil_bufs[0].reshape(cache_dil_k.shape), dil_bufs[1].reshape(cache_dil_v.shape),
            st("ck_p"), st("cv_p"), st("sk_p"), st("sv_p"), st("wk_p"), st("wv_p"),
            st("ck_s"), st("cv_s"), st("sk_s"), st("sv_s"), st("wk_s"), st("wv_s"))
```

```python
import functools

import numpy as np
import jax
import jax.numpy as jnp
from jax import lax
from jax.experimental import pallas as pl
from jax.experimental.pallas import tpu as pltpu

F32 = jnp.float32
BF16 = jnp.bfloat16
HIGHEST = lax.Precision.HIGHEST

HEAD_DIM = 128
LANES = 128
RMS_EPS = 1e-6
NEG_INF = -1e30
FORCE_SCORE = 1e6
SCALE = HEAD_DIM ** -0.5
HGRN_CHUNK = 64
HGRN_SUB = 8
DIL_PATTERNS = ((128, 1), (512, 4), (2048, 16))
DIL_MAX_WINDOW = 2048
CMP_LEN = 32
SLC_BLOCK = 64
SLC_TOPN = 16
WIN = 512
C_GROUP = 4
SLC_CHUNK = 512
VMEM_LIMIT = 56 * 1024 * 1024


def _params(*sem):
    return pltpu.CompilerParams(dimension_semantics=sem, vmem_limit_bytes=VMEM_LIMIT)


def _rms(x, w):
    return x * lax.rsqrt(jnp.mean(x * x, axis=-1, keepdims=True) + RMS_EPS) * w


def _sigmoid(x):
    return 1.0 / (1.0 + jnp.exp(-x))


def _dot_nt(a, b):
    return lax.dot_general(a, b, (((1,), (1,)), ((), ())), preferred_element_type=F32)


def _dot(a, b, precision=None):
    return jnp.dot(a, b, preferred_element_type=F32, precision=precision)


def _rms_mm_body(x_ref, g_ref, w_ref, o_ref, h_ref):
    @pl.when(pl.program_id(1) == 0)
    def _():
        h_ref[...] = _rms(x_ref[...], g_ref[...]).astype(BF16)

    o_ref[...] = _dot(h_ref[...], w_ref[...].astype(BF16))


def rms_matmul(x, g, w, tm, tn):
    M, D = x.shape
    N = w.shape[1]
    return pl.pallas_call(
        _rms_mm_body,
        grid=(M // tm, N // tn),
        in_specs=[pl.BlockSpec((tm, D), lambda i, j: (i, 0)),
                  pl.BlockSpec((1, D), lambda i, j: (0, 0)),
                  pl.BlockSpec((D, tn), lambda i, j: (0, j))],
        out_specs=pl.BlockSpec((tm, tn), lambda i, j: (i, j)),
        out_shape=jax.ShapeDtypeStruct((M, N), F32),
        scratch_shapes=[pltpu.VMEM((tm, D), BF16)],
        compiler_params=_params("parallel", "arbitrary"),
        name="rms_matmul",
    )(x, g.reshape(1, D), w)


def _proj_res_body(*refs, n_in):
    res_ref, o_ref = refs[2 * n_in], refs[2 * n_in + 1]
    acc = res_ref[...]
    for a_ref, w_ref in zip(refs[:n_in], refs[n_in:2 * n_in]):
        acc = acc + _dot(a_ref[...].astype(BF16), w_ref[...])
    o_ref[...] = acc


def proj_residual(lhs, ws, res, tm):
    M, D = res.shape
    n = len(lhs)
    in_specs = [pl.BlockSpec((tm, a.shape[1]), lambda i: (i, 0)) for a in lhs]
    in_specs += [pl.BlockSpec(w.shape, lambda i: (0, 0)) for w in ws]
    in_specs += [pl.BlockSpec((tm, D), lambda i: (i, 0))]
    return pl.pallas_call(
        functools.partial(_proj_res_body, n_in=n),
        grid=(M // tm,),
        in_specs=in_specs,
        out_specs=pl.BlockSpec((tm, D), lambda i: (i, 0)),
        out_shape=jax.ShapeDtypeStruct((M, D), F32),
        compiler_params=_params("parallel"),
        name="proj_residual",
    )(*lhs, *ws, res)


def _mlp_body(x_ref, g_ref, wu_ref, wd_ref, o_ref, h_ref):
    @pl.when(pl.program_id(1) == 0)
    def _():
        x = x_ref[...]
        h_ref[...] = _rms(x, g_ref[...]).astype(BF16)
        o_ref[...] = x

    u = jnp.maximum(_dot(h_ref[...], wu_ref[...]), 0.0)
    o_ref[...] += _dot((u * u).astype(BF16), wd_ref[...])


def mlp_residual(x, g, wu, wd, tm, tf):
    M, D = x.shape
    Fd = wu.shape[1]
    return pl.pallas_call(
        _mlp_body,
        grid=(M // tm, Fd // tf),
        in_specs=[pl.BlockSpec((tm, D), lambda i, j: (i, 0)),
                  pl.BlockSpec((1, D), lambda i, j: (0, 0)),
                  pl.BlockSpec((D, tf), lambda i, j: (0, j)),
                  pl.BlockSpec((tf, D), lambda i, j: (j, 0))],
        out_specs=pl.BlockSpec((tm, D), lambda i, j: (i, 0)),
        out_shape=jax.ShapeDtypeStruct((M, D), F32),
        scratch_shapes=[pltpu.VMEM((tm, D), BF16)],
        compiler_params=_params("parallel", "arbitrary"),
        name="mlp_residual",
    )(x, g.reshape(1, D), wu, wd)


def _hgrn_gates(z, lb):
    log_sig = jnp.minimum(z, 0.0) - jnp.log1p(jnp.exp(-jnp.abs(z)))
    a = jnp.log(lb)
    b = jnp.log1p(-lb) + log_sig
    log_f = jnp.maximum(a, b) + jnp.log1p(jnp.exp(-jnp.abs(a - b)))
    series = -(log_f + 0.5 * log_f * log_f + log_f * log_f * log_f * (1.0 / 6.0))
    k = jnp.where(log_f > -0.01, series, 1.0 - jnp.exp(log_f))
    return log_f, k


def _hgrn_out(o, on, g_raw):
    return _rms(o, on) * (g_raw * _sigmoid(g_raw))


def _split3(x):
    hi = x.astype(BF16)
    r1 = x - hi.astype(F32)
    mid = r1.astype(BF16)
    lo = (r1 - mid.astype(F32)).astype(BF16)
    return jnp.concatenate([hi, mid, lo], axis=1)


def _hgrn_body(q_ref, f_ref, i_ref, g_ref, lb_ref, on_ref, o_ref, s_ref, st_ref, *, tb, nh):
    C, SC = HGRN_CHUNK, HGRN_SUB
    t = pl.program_id(2)
    heads = range(nh)

    @pl.when(t == 0)
    def _():
        st_ref[...] = jnp.zeros_like(st_ref)

    on = on_ref[...]
    r_i = lax.broadcasted_iota(jnp.int32, (C, C), 0)
    c_i = lax.broadcasted_iota(jnp.int32, (C, C), 1)
    tril = (r_i >= c_i).astype(BF16)
    row = lax.broadcasted_iota(jnp.int32, (C, LANES), 0)
    levels = []
    bs = C // 2
    while bs >= SC:
        levels.append((bs, (row // bs) % 2 == 1, ((r_i // bs) % 2 == 1) & (c_i // bs == r_i // bs - 1)))
        bs //= 2
    lane_c = lax.broadcasted_iota(jnp.int32, (SC, C), 1)
    sub_c = lax.broadcasted_iota(jnp.int32, (SC, C), 0)

    def chunk(c, carry):
        rows = pl.ds(pl.multiple_of(c * C, C), C)
        sl = [slice(h * LANES, (h + 1) * LANES) for h in heads]
        qr = [q_ref[rows, sl[h]] for h in heads]
        q = [x * _sigmoid(x) for x in qr]
        gates = [_hgrn_gates(f_ref[rows, sl[h]], lb_ref[h]) for h in heads]
        log_f, kk = [g[0] for g in gates], [g[1] for g in gates]
        v = [i_ref[rows, sl[h]] for h in heads]
        vb = [x.astype(BF16) for x in v]
        g3 = [_dot(tril, _split3(log_f[h])) for h in heads]
        G = [x[:, 0:LANES] + x[:, LANES:2 * LANES] + x[:, 2 * LANES:3 * LANES] for x in g3]
        st = [st_ref[h] for h in heads]
        inter = [_dot_nt((q[h] * jnp.exp(G[h])).astype(BF16), st[h].astype(BF16)) for h in heads]
        a_off = [jnp.zeros((C, C), F32) for _ in heads]
        for bs, odd, blk in levels:
            refs = [jnp.concatenate([jnp.broadcast_to(G[h][p + bs - 1:p + bs], (2 * bs, LANES))
                                     for p in range(0, C, 2 * bs)], axis=0) for h in heads]
            d = [G[h] - refs[h] for h in heads]
            qp = [(q[h] * jnp.exp(jnp.where(odd, d[h], NEG_INF))).astype(BF16) for h in heads]
            kp = [(kk[h] * jnp.exp(jnp.where(odd, NEG_INF, -d[h]))).astype(BF16) for h in heads]
            a_off = [a_off[h] + jnp.where(blk, _dot_nt(qp[h], kp[h]), 0.0) for h in heads]
        a_rows = [[] for _ in heads]
        for I in range(C // SC):
            lo = I * SC
            for h in heads:
                GI, qI = G[h][lo:lo + SC], q[h][lo:lo + SC]
                dg = jnp.zeros((SC, C), F32)
                for j in range(SC):
                    e = jnp.exp(jnp.minimum(GI - GI[j:j + 1], 0.0))
                    colv = jnp.sum(qI * e * kk[h][lo + j:lo + j + 1], axis=1, keepdims=True)
                    dg = jnp.where(lane_c == lo + j, colv, dg)
                a_rows[h].append(jnp.where(lane_c <= lo + sub_c, dg, 0.0))
        a = [(a_off[h] + jnp.concatenate(a_rows[h], axis=0)).astype(BF16) for h in heads]
        o = [inter[h] + _dot(a[h], vb[h]) for h in heads]
        Gl = [G[h][C - 1:C] for h in heads]
        kd = [(kk[h] * jnp.exp(Gl[h] - G[h])).astype(BF16) for h in heads]
        upd = [_dot(v[h].T.astype(BF16), kd[h]) for h in heads]
        for h in heads:
            st_ref[h] = jnp.exp(Gl[h]) * st[h] + upd[h]
            o_ref[rows, sl[h]] = _hgrn_out(o[h], on, g_ref[rows, sl[h]]).astype(o_ref.dtype)
        return carry

    lax.fori_loop(0, tb // C, chunk, 0)

    @pl.when(t == pl.num_programs(2) - 1)
    def _():
        for h in heads:
            s_ref[0, h] = st_ref[h].T


def hgrn_prompt(proj, lb, on, n, T, heads, tb=256, nh=4):
    nt = T // tb
    hg = heads // nh
    col = lambda k: pl.BlockSpec((tb, nh * LANES), lambda b, h, t, k=k: (b * nt + t, k * hg + h))
    return pl.pallas_call(
        functools.partial(_hgrn_body, tb=tb, nh=nh),
        grid=(n, hg, nt),
        in_specs=[col(0), col(1), col(2), col(3),
                  pl.BlockSpec((nh, 1, LANES), lambda b, h, t: (h, 0, 0)),
                  pl.BlockSpec((1, LANES), lambda b, h, t: (0, 0))],
        out_specs=[pl.BlockSpec((tb, nh * LANES), lambda b, h, t: (b * nt + t, h)),
                   pl.BlockSpec((1, nh, LANES, LANES), lambda b, h, t: (b, h, 0, 0))],
        out_shape=[jax.ShapeDtypeStruct((n * T, heads * LANES), BF16),
                   jax.ShapeDtypeStruct((n, heads, LANES, LANES), F32)],
        scratch_shapes=[pltpu.VMEM((nh, LANES, LANES), F32)],
        compiler_params=_params("parallel", "parallel", "arbitrary"),
        name="hgrn_prompt",
    )(proj, proj, proj, proj, lb.reshape(heads, 1, LANES), on.reshape(1, LANES))


def _col(eye, row):
    return jnp.sum(eye * row, axis=1, keepdims=True)


def _hgrn_s_body(q_ref, f_ref, i_ref, g_ref, lb_ref, on_ref, s_ref, o_ref, so_ref, *, nb):
    qr = q_ref[...]
    q = qr * _sigmoid(qr)
    log_f, kk = _hgrn_gates(f_ref[...], lb_ref[0])
    v = i_ref[...]
    f = jnp.exp(log_f)
    eye = (lax.broadcasted_iota(jnp.int32, (LANES, LANES), 0)
           == lax.broadcasted_iota(jnp.int32, (LANES, LANES), 1)).astype(F32)
    qf = (q * f).astype(BF16)
    a = jnp.sum(q * kk, axis=1, keepdims=True)
    rows = []
    for b in range(nb):
        S = s_ref[b, 0, 0]
        so_ref[b, 0] = _col(eye, f[b:b + 1]) * S + _col(eye, kk[b:b + 1]) * v[b:b + 1]
        rows.append(_dot(qf, S.astype(BF16))[b:b + 1])
    o = jnp.concatenate(rows, axis=0) + a * v
    o_ref[...] = _hgrn_out(o, on_ref[...], g_ref[...])


def hgrn_sample(proj, lb, on, state, li, heads):
    nb = proj.shape[0]
    col = lambda k: pl.BlockSpec((nb, LANES), lambda h, k=k: (0, k * heads + h))
    return pl.pallas_call(
        functools.partial(_hgrn_s_body, nb=nb),
        grid=(heads,),
        in_specs=[col(0), col(1), col(2), col(3),
                  pl.BlockSpec((1, 1, LANES), lambda h: (h, 0, 0)),
                  pl.BlockSpec((1, LANES), lambda h: (0, 0)),
                  pl.BlockSpec((nb, 1, 1, LANES, LANES), lambda h: (0, li, h, 0, 0))],
        out_specs=[pl.BlockSpec((nb, LANES), lambda h: (0, h)),
                   pl.BlockSpec((nb, 1, LANES, LANES), lambda h: (0, h, 0, 0))],
        out_shape=[jax.ShapeDtypeStruct((nb, heads * LANES), F32),
                   jax.ShapeDtypeStruct((nb, heads, LANES, LANES), F32)],
        compiler_params=_params("parallel"),
        name="hgrn_sample",
    )(proj, proj, proj, proj, lb.reshape(heads, 1, LANES), on.reshape(1, LANES), state)


def _dil_multiplicity(delta):
    c = np.zeros(delta.shape, np.float32)
    for window, dil in DIL_PATTERNS:
        c += ((delta >= 0) & (delta <= window) & (delta % dil == 0)).astype(np.float32)
    return c


def _dil_body(q_ref, k_ref, v_ref, qn_ref, kn_ref, c_ref, o_ref, ko_ref, vo_ref, kn_s, vb_s,
              *, T, tq, W, pad, nh):
    qi = pl.program_id(2)
    span = pad + tq
    sl = [slice(h * LANES, (h + 1) * LANES) for h in range(nh)]

    @pl.when(qi == 0)
    def _():
        kw = kn_ref[...]
        kn_s[:, 0:pad, :] = jnp.zeros((nh, pad, LANES), BF16)
        vb_s[:, 0:pad, :] = jnp.zeros((nh, pad, LANES), BF16)

        def norm(c, carry):
            rows = pl.ds(pl.multiple_of(c * 512, 512), 512)
            dst = pl.ds(pl.multiple_of(pad + c * 512, 512), 512)
            for h in range(nh):
                kn_s[h, dst, :] = _rms(k_ref[rows, sl[h]], kw).astype(BF16)
                vb_s[h, dst, :] = v_ref[rows, sl[h]].astype(BF16)
            return carry

        lax.fori_loop(0, T // 512, norm, 0)
        for h in range(nh):
            ko_ref[0, :, sl[h]] = _rms(k_ref[T - W:, sl[h]], kw)
        vo_ref[0] = v_ref[T - W:, :]

    rows = pl.ds(pl.multiple_of(qi * tq, tq), span)
    q = [(_rms(q_ref[:, sl[h]], qn_ref[...]) * SCALE).astype(BF16) for h in range(nh)]
    sc = [_dot_nt(q[h], kn_s[h, rows, :]) for h in range(nh)]
    c = c_ref[...]
    valid = (c > 0.0) & (lax.broadcasted_iota(jnp.int32, (tq, span), 1) >= pad - qi * tq)
    sc = [jnp.where(valid, x, NEG_INF) for x in sc]
    p = [c * jnp.exp(x - jnp.max(x, axis=1, keepdims=True)) for x in sc]
    o = [_dot(p[h].astype(BF16), vb_s[h, rows, :]) for h in range(nh)]
    for h in range(nh):
        o_ref[:, sl[h]] = (o[h] / jnp.sum(p[h], axis=1, keepdims=True)).astype(o_ref.dtype)


def dil_prompt(proj, qn, kn, n, T, heads, col0, tq=256, nh=2):
    W = min(DIL_MAX_WINDOW, T)
    pad = DIL_MAX_WINDOW
    nt = T // tq
    ctab = jnp.asarray(_dil_multiplicity(np.arange(tq)[:, None] + pad - np.arange(pad + tq)[None, :]))
    hw = nh * LANES
    cb, hg = col0 // hw, heads // nh
    return pl.pallas_call(
        functools.partial(_dil_body, T=T, tq=tq, W=W, pad=pad, nh=nh),
        grid=(n, hg, nt),
        in_specs=[pl.BlockSpec((tq, hw), lambda b, h, t: (b * nt + t, cb + h)),
                  pl.BlockSpec((T, hw), lambda b, h, t: (b, cb + hg + h)),
                  pl.BlockSpec((T, hw), lambda b, h, t: (b, cb + 2 * hg + h)),
                  pl.BlockSpec((1, LANES), lambda b, h, t: (0, 0)),
                  pl.BlockSpec((1, LANES), lambda b, h, t: (0, 0)),
                  pl.BlockSpec((tq, pad + tq), lambda b, h, t: (0, 0))],
        out_specs=[pl.BlockSpec((tq, hw), lambda b, h, t: (b * nt + t, h)),
                   pl.BlockSpec((1, W, hw), lambda b, h, t: (b, 0, h)),
                   pl.BlockSpec((1, W, hw), lambda b, h, t: (b, 0, h))],
        out_shape=[jax.ShapeDtypeStruct((n * T, heads * LANES), BF16),
                   jax.ShapeDtypeStruct((n, W, heads * LANES), F32),
                   jax.ShapeDtypeStruct((n, W, heads * LANES), F32)],
        scratch_shapes=[pltpu.VMEM((nh, T + pad, LANES), BF16), pltpu.VMEM((nh, T + pad, LANES), BF16)],
        compiler_params=_params("parallel", "parallel", "arbitrary"),
        name="dil_prompt",
    )(proj, proj, proj, qn.reshape(1, LANES), kn.reshape(1, LANES), ctab)


def _shift_in(buf, new_row):
    n = buf.shape[0]
    rolled = pltpu.roll(buf, n - 1, 0)
    return jnp.where(lax.broadcasted_iota(jnp.int32, buf.shape, 0) == n - 1, new_row, rolled)


def _dil_s_body(q_ref, k_ref, v_ref, qn_ref, kn_ref, c_ref, *rest, heads, c_new, li, n_l, has_prev):
    n_src = 1 if has_prev else n_l
    ck_refs, cv_refs = rest[:n_src], rest[n_src:2 * n_src]
    o_ref, ko_hbm, vo_hbm, m_s, l_s, acc_s, new_s, sem = rest[2 * n_src + (2 if has_prev else 0):]
    ck_ref, cv_ref = (ck_refs[0], cv_refs[0]) if has_prev else (ck_refs[li], cv_refs[li])
    H = heads
    b, c = pl.program_id(0), pl.program_id(1)
    nrows, cr = ko_hbm.shape[2], ck_ref.shape[2]
    sub = lax.broadcasted_iota(jnp.int32, (H, LANES), 0)

    def heads_on_rows(ref):
        out = jnp.zeros((H, LANES), F32)
        for h in range(H):
            out = jnp.where(sub == h, ref[:, h * LANES:(h + 1) * LANES], out)
        return out

    q8 = _rms(heads_on_rows(q_ref), qn_ref[...]) * SCALE
    kn8 = _rms(heads_on_rows(k_ref), kn_ref[...])
    v8 = heads_on_rows(v_ref)

    def chunk_copies(first):
        if first:
            src, dst = pl.ds(H, cr - H), pl.ds(0, cr - H)
        else:
            src, dst = pl.ds(0, cr), pl.ds(pl.multiple_of(c * cr - H, H), cr)
        out = []
        for l in range(0 if has_prev else n_l):
            out.append(pltpu.make_async_copy(ck_refs[l].at[0, 0, src], ko_hbm.at[b, l, dst], sem.at[0]))
            out.append(pltpu.make_async_copy(cv_refs[l].at[0, 0, src], vo_hbm.at[b, l, dst], sem.at[1]))
        return out

    def tail_copies():
        tail = pl.ds(nrows - H, H)
        out = [pltpu.make_async_copy(new_s.at[0], ko_hbm.at[b, li, tail], sem.at[2]),
               pltpu.make_async_copy(new_s.at[1], vo_hbm.at[b, li, tail], sem.at[3])]
        for l in range(0 if has_prev else n_l):
            if l != li:
                out.append(pltpu.make_async_copy(new_s.at[2], ko_hbm.at[b, l, tail], sem.at[2]))
                out.append(pltpu.make_async_copy(new_s.at[2], vo_hbm.at[b, l, tail], sem.at[3]))
        return out

    @pl.when(c == 0)
    def _():
        m_s[...] = jnp.full_like(m_s, NEG_INF)
        l_s[...] = jnp.zeros_like(l_s)
        acc_s[...] = jnp.zeros_like(acc_s)
        for cp in chunk_copies(True):
            cp.start()

    @pl.when(c > 0)
    def _():
        for cp in chunk_copies(False):
            cp.start()

    kc, vc = ck_ref[0, 0], cv_ref[0, 0]
    s = _dot_nt(q8.astype(BF16), kc.astype(BF16))
    own = lax.broadcasted_iota(jnp.int32, s.shape, 1) % H == lax.broadcasted_iota(jnp.int32, s.shape, 0)
    cm = jnp.where(own, c_ref[0], 0.0)
    s = jnp.where(cm > 0.0, s, NEG_INF)
    m_old = m_s[...]
    m_new = jnp.maximum(m_old, jnp.max(s, axis=1, keepdims=True))
    alpha = jnp.exp(m_old - m_new)
    p = cm * jnp.exp(s - m_new[:, 0:1])
    l_s[...] = alpha * l_s[...] + jnp.sum(p, axis=1, keepdims=True)
    acc_s[...] = alpha * acc_s[...] + _dot(p.astype(BF16), vc.astype(BF16))
    m_s[...] = m_new

    @pl.when(c == 0)
    def _():
        for cp in chunk_copies(True):
            cp.wait()

    @pl.when(c > 0)
    def _():
        for cp in chunk_copies(False):
            cp.wait()

    @pl.when(c == pl.num_programs(1) - 1)
    def _():
        s_new = jnp.sum(q8 * kn8, axis=1, keepdims=True)
        m_f = jnp.maximum(m_s[...], s_new)
        a = jnp.exp(m_s[...] - m_f)
        p_new = c_new * jnp.exp(s_new - m_f)
        o = (a * acc_s[...] + p_new * v8) / (a * l_s[...] + p_new)
        for h in range(H):
            o_ref[:, h * LANES:(h + 1) * LANES] = o[h:h + 1]
        new_s[0] = kn8
        new_s[1] = v8
        new_s[2] = jnp.zeros((H, LANES), F32)
        for cp in tail_copies():
            cp.start()
        for cp in tail_copies():
            cp.wait()


def dil_sample(proj, qn, kn, cache_k, cache_v, li, heads, col0, prev=None, chunk=512):
    nb, n_l, Wb = cache_k.shape[0], cache_k.shape[1], cache_k.shape[2]
    H = heads
    ck = cache_k.reshape(nb, n_l, Wb * H, LANES)
    cv = cache_v.reshape(nb, n_l, Wb * H, LANES)
    nch = Wb // chunk
    c_buf = jnp.asarray(np.repeat(_dil_multiplicity(Wb - np.arange(Wb)), H).reshape(nch, 1, chunk * H))
    c_new = float(_dil_multiplicity(np.zeros((1,), np.int64))[0])
    hw = H * LANES
    proj = proj.reshape(nb, 1, -1)
    col = lambda k: pl.BlockSpec((None, 1, hw), lambda b, c, k=k: (b, 0, col0 // hw + k))
    vec = pl.BlockSpec((1, LANES), lambda b, c: (0, 0))
    cache = lambda l: pl.BlockSpec((1, 1, chunk * H, LANES), lambda b, c, l=l: (b, l, c, 0))
    anyspec = pl.BlockSpec(memory_space=pl.ANY)
    full = jax.ShapeDtypeStruct((nb, n_l, Wb * H, LANES), F32)
    layers = [li] if prev is not None else list(range(n_l))
    args = [proj, proj, proj, qn.reshape(1, LANES), kn.reshape(1, LANES), c_buf]
    args += [ck] * len(layers) + [cv] * len(layers)
    in_specs = [col(0), col(1), col(2), vec, vec, pl.BlockSpec((1, 1, chunk * H), lambda b, c: (c, 0, 0))]
    in_specs += [cache(l) for l in layers] * 2
    aliases = {}
    if prev is not None:
        aliases = {len(args): 1, len(args) + 1: 2}
        args += list(prev)
        in_specs += [anyspec, anyspec]
    o, ok, ov = pl.pallas_call(
        functools.partial(_dil_s_body, heads=H, c_new=c_new, li=li, n_l=n_l, has_prev=prev is not None),
        grid=(nb, nch),
        in_specs=in_specs,
        out_specs=[pl.BlockSpec((None, 1, hw), lambda b, c: (b, 0, 0)), anyspec, anyspec],
        out_shape=[jax.ShapeDtypeStruct((nb, 1, hw), F32), full, full],
        scratch_shapes=[pltpu.VMEM((H, LANES), F32), pltpu.VMEM((H, LANES), F32), pltpu.VMEM((H, LANES), F32),
                        pltpu.VMEM((3, H, LANES), F32), pltpu.SemaphoreType.DMA((4,))],
        input_output_aliases=aliases,
        compiler_params=_params("arbitrary", "arbitrary"),
        name="dil_sample",
    )(*args)
    return o.reshape(nb, hw), ok, ov


def _gelu(x):
    return 0.5 * x * (1.0 + jnp.tanh(0.7978845608028654 * (x + 0.044715 * x * x * x)))


def _compress_rows(load_j, pe_ref, w1_ref, w2_ref, rows):
    acc = jnp.zeros((rows, LANES), F32)
    for j in range(CMP_LEN):
        acc = acc + _dot((load_j(j) + pe_ref[j:j + 1, :]).astype(BF16), w1_ref[j])
    return _dot(_gelu(acc).astype(BF16), w2_ref[...])


def _nsa_prep_body(kc_ref, vc_ref, ks_ref, vs_ref, kw_ref, vw_ref, knorm_ref, pek_ref, pev_ref,
                   w1k_ref, w2k_ref, w1v_ref, w2v_ref,
                   kcmp_ref, vcmp_ref, ksb_ref, ksf_ref, vst_ref, kwb_ref, kwf_ref, vwt_ref, *, T):
    nblk = T // CMP_LEN
    kcmp = _compress_rows(lambda j: kc_ref[pl.ds(j, nblk, stride=CMP_LEN), :], pek_ref, w1k_ref, w2k_ref, nblk)
    kcmp_ref[0, 0] = _rms(kcmp, knorm_ref[0:1, :])
    vcmp_ref[0, 0] = _compress_rows(lambda j: vc_ref[pl.ds(j, nblk, stride=CMP_LEN), :], pev_ref, w1v_ref, w2v_ref, nblk)
    ks_w, kw_w = knorm_ref[1:2, :], knorm_ref[2:3, :]

    kwb_ref[0, 0, 0:WIN, :] = jnp.zeros((WIN, LANES), BF16)
    for i in range(WIN // LANES):
        vwt_ref[0, 0, i] = jnp.zeros((LANES, LANES), BF16)

    def tile(c, carry):
        rows = pl.ds(pl.multiple_of(c * LANES, LANES), LANES)
        ksn = _rms(ks_ref[rows, :], ks_w)
        ksf_ref[0, rows, :] = ksn
        ksb_ref[0, 0, rows, 0:LANES] = ksn.astype(BF16)
        blk_of_row = c * (LANES // SLC_BLOCK) + lax.broadcasted_iota(jnp.int32, (LANES, LANES), 0) // SLC_BLOCK
        ksb_ref[0, 0, rows, LANES:2 * LANES] = (lax.broadcasted_iota(jnp.int32, (LANES, LANES), 1) == blk_of_row).astype(BF16)
        kwb_ref[0, 0, pl.ds(pl.multiple_of(WIN + c * LANES, LANES), LANES), :] = _rms(kw_ref[rows, :], kw_w).astype(BF16)
        vwt_ref[0, 0, WIN // LANES + c] = vw_ref[rows, :].T.astype(BF16)
        return carry

    lax.fori_loop(0, T // LANES, tile, 0)

    def chunk(c, carry):
        rows = pl.ds(pl.multiple_of(c * SLC_CHUNK, SLC_CHUNK), SLC_CHUNK)
        vst_ref[0, 0, c] = vs_ref[rows, :].T.astype(BF16)
        return carry

    lax.fori_loop(0, T // SLC_CHUNK, chunk, 0)
    ww = min(WIN, T)
    kwf_ref[0] = _rms(kw_ref[T - ww:, :], kw_w)


def nsa_prep(proj, knorm, pe_k, pe_v, w1k, w2k, w1v, w2v, n, T, hk):
    cb = (hk * C_GROUP * HEAD_DIM) // LANES
    nblk = T // CMP_LEN
    nt = T // LANES
    ww = min(WIN, T)
    col = lambda k: pl.BlockSpec((T, LANES), lambda b, h, k=k: (b, cb + k * hk + h))
    full = lambda a: pl.BlockSpec(a.shape, lambda b, h: (0,) * a.ndim)
    per = lambda *s: pl.BlockSpec((1, 1) + s, lambda b, h: (b, h) + (0,) * len(s))
    return pl.pallas_call(
        functools.partial(_nsa_prep_body, T=T),
        grid=(n, hk),
        in_specs=[col(0), col(1), col(2), col(3), col(4), col(5),
                  full(knorm), full(pe_k), full(pe_v), full(w1k), full(w2k), full(w1v), full(w2v)],
        out_specs=[per(nblk, LANES), per(nblk, LANES), per(T, 2 * LANES),
                   pl.BlockSpec((1, T, LANES), lambda b, h: (b, 0, h)),
                   per(T // SLC_CHUNK, LANES, SLC_CHUNK), per(T + WIN, LANES),
                   pl.BlockSpec((1, ww, LANES), lambda b, h: (b, 0, h)),
                   per(nt + WIN // LANES, LANES, LANES)],
        out_shape=[jax.ShapeDtypeStruct((n, hk, nblk, LANES), F32),
                   jax.ShapeDtypeStruct((n, hk, nblk, LANES), F32),
                   jax.ShapeDtypeStruct((n, hk, T, 2 * LANES), BF16),
                   jax.ShapeDtypeStruct((n, T, hk * LANES), F32),
                   jax.ShapeDtypeStruct((n, hk, T // SLC_CHUNK, LANES, SLC_CHUNK), BF16),
                   jax.ShapeDtypeStruct((n, hk, T + WIN, LANES), BF16),
                   jax.ShapeDtypeStruct((n, ww, hk * LANES), F32),
                   jax.ShapeDtypeStruct((n, hk, nt + WIN // LANES, LANES, LANES), BF16)],
        compiler_params=_params("parallel", "parallel"),
        name="nsa_prep",
    )(proj, proj, proj, proj, proj, proj, knorm, pe_k, pe_v, w1k, w2k, w1v, w2v)


def _tile4(x):
    return jnp.concatenate([x] * C_GROUP, axis=1)


def _nsa_body(q_ref, gate_ref, qn_ref, kcmp_ref, vcmp_ref, ks_ref, vst_ref, kw_ref, vwt_ref, o_ref,
              vct_s, pb_s, gt_s, *, tq, nblk, nslc, hk_n, hpb):
    G = C_GROUP
    HB = range(hpb)
    gw = G * LANES
    hk0 = pl.program_id(1) * hpb
    qi = pl.program_id(2)
    t0 = qi * tq

    @pl.when(qi == 0)
    def _():
        for h in HB:
            vct_s[h] = vcmp_ref[0, h].T.astype(BF16)

    qw = qn_ref[...]
    q4 = [jnp.concatenate([(_rms(q_ref[:, h * gw + g * LANES:h * gw + (g + 1) * LANES], qw) * SCALE).astype(BF16)
                           for g in range(G)], axis=0) for h in HB]

    st = [_dot_nt(kcmp_ref[0, h].astype(BF16), q4[h]) for h in HB]
    blk = lax.broadcasted_iota(jnp.int32, (nblk, G * tq), 0)
    tpos = t0 + (lax.broadcasted_iota(jnp.int32, (nblk, G * tq), 1) & (tq - 1))
    valid = (blk + 1) * CMP_LEN - 1 <= tpos
    st = [jnp.where(valid, x, NEG_INF) for x in st]
    p = [jnp.where(valid, jnp.exp(x - jnp.max(x, axis=0, keepdims=True)), 0.0) for x in st]
    p = [x / jnp.maximum(jnp.sum(x, axis=0, keepdims=True), 1.0) for x in p]
    o_cmp = [_dot(vct_s[h], p[h].astype(BF16)) for h in HB]
    ratio = SLC_BLOCK // CMP_LEN
    imp = []
    for h in HB:
        pb = p[h][:, 0:tq]
        for g in range(1, G):
            pb = pb + p[h][:, g * tq:(g + 1) * tq]
        parts = []
        for i in range(tq // LANES):
            pb_s[h, i] = pb[:, i * LANES:(i + 1) * LANES]
            part = pb_s[h, i, pl.ds(0, nslc, stride=ratio), :]
            for r in range(1, ratio):
                part = part + pb_s[h, i, pl.ds(r, nslc, stride=ratio), :]
            parts.append(part)
        imp.append(jnp.concatenate(parts, axis=1))

    jb = lax.broadcasted_iota(jnp.int32, (nslc, tq), 0)
    tp = t0 + lax.broadcasted_iota(jnp.int32, (nslc, tq), 1)
    cur = tp // SLC_BLOCK
    forced = (jb == 0) | (jb == cur) | (jb == cur - 1)
    in_past = jb * SLC_BLOCK <= tp
    score = [jnp.where(in_past, jnp.where(forced, FORCE_SCORE, x), -FORCE_SCORE) for x in imp]
    rank = [jnp.zeros((nslc, tq), F32) for _ in HB]
    for jp in range(nslc):
        later = jb > jp
        for h in HB:
            row = score[h][jp:jp + 1, :]
            rank[h] = rank[h] + ((row > score[h]) | ((row == score[h]) & later)).astype(F32)
    q_aug = []
    for h in HB:
        bias = jnp.where(rank[h] < float(min(SLC_TOPN, nslc)), 0.0, NEG_INF)
        bias = jnp.concatenate([bias, jnp.zeros((LANES - nslc, tq), F32)], axis=0).T.astype(BF16)
        q_aug.append(jnp.concatenate([q4[h], jnp.concatenate([bias] * G, axis=0)], axis=1))

    KC = SLC_CHUNK

    def slc_step(kc, carry, causal):
        m, l, acc = carry
        rows = pl.ds(pl.multiple_of(kc * KC, KC), KC)
        s = [_dot_nt(ks_ref[0, h, rows, :], q_aug[h]) for h in HB]
        if causal:
            kpos = kc * KC + lax.broadcasted_iota(jnp.int32, (KC, tq), 0)
            hide = _tile4(jnp.where(kpos <= t0 + lax.broadcasted_iota(jnp.int32, (KC, tq), 1), 0.0, NEG_INF))
            s = [x + hide for x in s]
        m_new = [jnp.maximum(m[h], jnp.max(s[h], axis=0, keepdims=True)) for h in HB]
        alpha = [jnp.exp(m[h] - m_new[h]) for h in HB]
        pp = [jnp.exp(s[h] - m_new[h]) for h in HB]
        l = [alpha[h] * l[h] + jnp.sum(pp[h], axis=0, keepdims=True) for h in HB]
        pv = [_dot(vst_ref[0, h, kc], pp[h].astype(BF16)) for h in HB]
        acc = [alpha[h] * acc[h] + pv[h] for h in HB]
        return m_new, l, acc

    init = ([jnp.full((1, G * tq), NEG_INF, F32) for _ in HB], [jnp.zeros((1, G * tq), F32) for _ in HB],
            [jnp.zeros((LANES, G * tq), F32) for _ in HB])
    last = (t0 + tq - 1) // KC
    carry = lax.fori_loop(0, last, functools.partial(slc_step, causal=False), init)
    _, l_s, acc_s = slc_step(last, carry, causal=True)
    o_slc = [acc_s[h] / l_s[h] for h in HB]

    wspan = WIN + tq
    wsub = lax.broadcasted_iota(jnp.int32, (wspan, tq), 0)
    dist = lax.broadcasted_iota(jnp.int32, (wspan, tq), 1) + WIN - wsub
    wbias = _tile4(jnp.where((dist >= 0) & (dist <= WIN) & (wsub >= WIN - t0), 0.0, NEG_INF))
    wrows = pl.ds(pl.multiple_of(t0, tq), wspan)
    sw = [_dot_nt(kw_ref[0, h, wrows, :], q4[h]) + wbias for h in HB]
    pw = [jnp.exp(x - jnp.max(x, axis=0, keepdims=True)) for x in sw]
    vw_t = [jnp.concatenate([vwt_ref[0, h, qi * (tq // LANES) + i] for i in range(wspan // LANES)], axis=1)
            for h in HB]
    o_win = [_dot(vw_t[h], pw[h].astype(BF16)) / jnp.sum(pw[h], axis=0, keepdims=True) for h in HB]

    gt_s[...] = _sigmoid(gate_ref[...]).T
    nh = G * hk_n
    for h in HB:
        for g in range(G):
            sl = slice(g * tq, (g + 1) * tq)
            head = (hk0 + h) * G + g
            g0 = gt_s[pl.ds(head, 1), :]
            g1 = gt_s[pl.ds(nh + head, 1), :]
            g2 = gt_s[pl.ds(2 * nh + head, 1), :]
            o = g0 * o_cmp[h][:, sl] + g1 * o_slc[h][:, sl] + g2 * o_win[h][:, sl]
            o_ref[:, h * gw + g * LANES:h * gw + (g + 1) * LANES] = o.T.astype(o_ref.dtype)


def nsa_prompt(proj, qn, kcmp, vcmp, ksb, vst, kwb, vwt, n, T, hk, gate_col, tq=256, hpb=2):
    nblk, nslc, nt = T // CMP_LEN, T // SLC_BLOCK, T // tq
    gw = hpb * C_GROUP * LANES
    per = lambda *s: pl.BlockSpec((1, hpb) + s, lambda b, h, t: (b, h) + (0,) * len(s))
    return pl.pallas_call(
        functools.partial(_nsa_body, tq=tq, nblk=nblk, nslc=nslc, hk_n=hk, hpb=hpb),
        grid=(n, hk // hpb, nt),
        in_specs=[pl.BlockSpec((tq, gw), lambda b, h, t: (b * nt + t, h)),
                  pl.BlockSpec((tq, LANES), lambda b, h, t: (b * nt + t, gate_col // LANES)),
                  pl.BlockSpec((1, LANES), lambda b, h, t: (0, 0)),
                  per(nblk, LANES), per(nblk, LANES), per(T, 2 * LANES), per(T // SLC_CHUNK, LANES, SLC_CHUNK),
                  per(T + WIN, LANES), per((T + WIN) // LANES, LANES, LANES)],
        out_specs=pl.BlockSpec((tq, gw), lambda b, h, t: (b * nt + t, h)),
        out_shape=jax.ShapeDtypeStruct((n * T, hk * C_GROUP * LANES), BF16),
        scratch_shapes=[pltpu.VMEM((hpb, LANES, nblk), BF16), pltpu.VMEM((hpb, tq // LANES, nblk, LANES), F32),
                        pltpu.VMEM((LANES, tq), F32)],
        compiler_params=_params("parallel", "parallel", "arbitrary"),
        name="nsa_prompt",
    )(proj, proj, qn.reshape(1, LANES), kcmp, vcmp, ksb, vst, kwb, vwt)


def _cmp_pages_body(pt_ref, ck_hbm, cv_hbm, knorm_ref, pek_ref, pev_ref, w1k_ref, w2k_ref, w1v_ref, w2v_ref,
                    ko_ref, vo_ref, kbuf, vbuf, fold_s, sem, *, li, P, hk, rows_per_page):
    s = pl.program_id(0)
    ns = pl.num_programs(0)
    gpp = rows_per_page // hk // CMP_LEN

    def copies(step, slot):
        out = []
        for p in range(P):
            page = pt_ref[step * P + p]
            dst = pl.ds(p * gpp, gpp)
            out.append(pltpu.make_async_copy(ck_hbm.at[page, li], kbuf.at[slot, dst], sem.at[0, slot]))
            out.append(pltpu.make_async_copy(cv_hbm.at[page, li], vbuf.at[slot, dst], sem.at[1, slot]))
        return out

    @pl.when(s == 0)
    def _():
        for c in copies(0, 0):
            c.start()

    @pl.when(s + 1 < ns)
    def _():
        for c in copies(s + 1, (s + 1) % 2):
            c.start()

    slot = s % 2
    for c in copies(s, slot):
        c.wait()

    groups = P * (rows_per_page // hk // CMP_LEN)
    tpv = 8 // hk
    own = [lax.broadcasted_iota(jnp.int32, (groups * 8, LANES), 0) % 8 // hk == u for u in range(tpv)]

    def compress(buf, pe_ref, w1_ref, w2_ref):
        acc = jnp.zeros((groups * 8, LANES), F32)
        for jp in range(CMP_LEN // tpv):
            x = buf[slot, :, pl.ds(jp * 8, 8), :] + pe_ref[jp]
            y = _dot(x.reshape(groups * 8, LANES).astype(BF16), w1_ref[jp])
            part = y[:, 0:LANES]
            for u in range(1, tpv):
                part = jnp.where(own[u], y[:, u * LANES:(u + 1) * LANES], part)
            acc = acc + part
        tot = acc
        for u in range(1, tpv):
            tot = tot + pltpu.roll(acc, u * hk, 0)
        return _dot(_gelu(tot).astype(BF16), w2_ref[...])

    fold_s[0] = _rms(compress(kbuf, pek_ref, w1k_ref, w2k_ref), knorm_ref[0:1, :])
    fold_s[1] = compress(vbuf, pev_ref, w1v_ref, w2v_ref)
    for h in range(hk):
        rows = pl.ds(8 - hk + h, groups, stride=8)
        ko_ref[0, h] = fold_s[0, rows, :]
        vo_ref[0, h] = fold_s[1, rows, :]


def cmp_pages(page_table, cache_k, cache_v, li, knorm, pe_k, pe_v, w1k, w2k, w1v, w2v, P=16):
    nb, n_pages = page_table.shape
    n_pool, n_l, page, hk, dh = cache_k.shape
    rpp = page * hk
    gpp, grows = page // CMP_LEN, CMP_LEN * hk
    ck = cache_k.reshape(n_pool, n_l, gpp, grows, dh)
    cv = cache_v.reshape(n_pool, n_l, gpp, grows, dh)
    tpv = 8 // hk
    slab_pe = lambda pe: jnp.repeat(pe, hk, axis=0).reshape(CMP_LEN // tpv, 8, dh)
    slab_w = lambda w: w.reshape(CMP_LEN // tpv, tpv, dh, dh).transpose(0, 2, 1, 3).reshape(CMP_LEN // tpv, dh, tpv * dh)
    pe_k, pe_v, w1k, w1v = slab_pe(pe_k), slab_pe(pe_v), slab_w(w1k), slab_w(w1v)
    steps_per_b = n_pages // P
    blocks = P * gpp
    nblk = n_pages * gpp
    full = lambda a: pl.BlockSpec(a.shape, lambda s, pt: (0,) * a.ndim)
    out_spec = pl.BlockSpec((1, hk, blocks, dh), lambda s, pt: (s // steps_per_b, 0, s % steps_per_b, 0))
    grid_spec = pltpu.PrefetchScalarGridSpec(
        num_scalar_prefetch=1,
        grid=(nb * steps_per_b,),
        in_specs=[pl.BlockSpec(memory_space=pl.ANY), pl.BlockSpec(memory_space=pl.ANY),
                  full(knorm), full(pe_k), full(pe_v), full(w1k), full(w2k), full(w1v), full(w2v)],
        out_specs=[out_spec, out_spec],
        scratch_shapes=[pltpu.VMEM((2, blocks, grows, dh), F32), pltpu.VMEM((2, blocks, grows, dh), F32),
                        pltpu.VMEM((2, blocks * 8, dh), F32), pltpu.SemaphoreType.DMA((2, 2))])
    return pl.pallas_call(
        functools.partial(_cmp_pages_body, li=li, P=P, hk=hk, rows_per_page=rpp),
        grid_spec=grid_spec,
        out_shape=[jax.ShapeDtypeStruct((nb, hk, nblk, dh), F32)] * 2,
        compiler_params=_params("arbitrary"),
        name="cmp_pages",
    )(page_table.reshape(-1), ck, cv, knorm, pe_k, pe_v, w1k, w2k, w1v, w2v)


def _nsa_s_select_body(q_ref, kc_ref, vc_ref, ks_ref, qn_ref, knorm_ref, pek_ref, pev_ref,
                       w1k_ref, w2k_ref, w1v_ref, w2v_ref, kcmp_ref, vcmp_ref,
                       qo_ref, ocmp_ref, sel_ref, kso_ref, *, hk, qpos, nblk):
    G = C_GROUP
    row = slice(None)
    qw = qn_ref[...]
    nslc = (nblk + 1 + 1) // 2
    lanes_blk = lax.broadcasted_iota(jnp.int32, (8, nblk), 1)
    valid = (lanes_blk + 1) * CMP_LEN - 1 <= qpos
    valid_x = (jnp.full((8, 1), (nblk + 1) * CMP_LEN - 1, jnp.int32) <= qpos)
    pair = (lax.broadcasted_iota(jnp.int32, (nblk, nblk // 2), 0) // 2
            == lax.broadcasted_iota(jnp.int32, (nblk, nblk // 2), 1)).astype(F32)
    sub8 = lax.broadcasted_iota(jnp.int32, (8, 1), 0)
    pe_rest_k = jnp.zeros((8, LANES), F32)
    pe_rest_v = jnp.zeros((8, LANES), F32)
    for j in range(1, CMP_LEN):
        pe_rest_k = pe_rest_k + _dot(jnp.broadcast_to(pek_ref[j:j + 1, :], (8, LANES)).astype(BF16), w1k_ref[j])
        pe_rest_v = pe_rest_v + _dot(jnp.broadcast_to(pev_ref[j:j + 1, :], (8, LANES)).astype(BF16), w1v_ref[j])
    for h in range(hk):
        qs = [_rms(q_ref[row, (h * G + g) * LANES:(h * G + g + 1) * LANES], qw) * SCALE for g in range(G)]
        q8 = jnp.concatenate(qs + [jnp.zeros((8 - G, LANES), F32)], axis=0)
        qo_ref[0, h] = q8
        kso_ref[0, h] = jnp.broadcast_to(_rms(ks_ref[row, h * LANES:(h + 1) * LANES], knorm_ref[1:2, :]), (8, LANES))
        xk = jnp.broadcast_to(kc_ref[row, h * LANES:(h + 1) * LANES] + pek_ref[0:1, :], (8, LANES))
        xv = jnp.broadcast_to(vc_ref[row, h * LANES:(h + 1) * LANES] + pev_ref[0:1, :], (8, LANES))
        k_x = _dot(_gelu(_dot(xk.astype(BF16), w1k_ref[0]) + pe_rest_k).astype(BF16), w2k_ref[...])
        k_x = _rms(k_x, knorm_ref[0:1, :])
        v_x = _dot(_gelu(_dot(xv.astype(BF16), w1v_ref[0]) + pe_rest_v).astype(BF16), w2v_ref[...])
        s = jnp.where(valid, _dot_nt(q8.astype(BF16), kcmp_ref[0, h].astype(BF16)), NEG_INF)
        s_x = jnp.where(valid_x, jnp.sum(q8 * k_x, axis=1, keepdims=True), NEG_INF)
        m = jnp.maximum(jnp.max(s, axis=1, keepdims=True), s_x)
        p = jnp.where(valid, jnp.exp(s - m), 0.0)
        p_x = jnp.where(valid_x, jnp.exp(s_x - m), 0.0)
        den = jnp.maximum(jnp.sum(p, axis=1, keepdims=True) + p_x, 1.0)
        p = jnp.where(sub8 < G, p / den, 0.0)
        p_x = jnp.where(sub8 < G, p_x / den, 0.0)
        ocmp_ref[0, h] = _dot(p.astype(BF16), vcmp_ref[0, h].astype(BF16)) + p_x * v_x
        pb = jnp.sum(p, axis=0, keepdims=True)
        pb_x = jnp.sum(p_x, axis=0, keepdims=True)
        imp = _dot(jnp.broadcast_to(pb, (8, nblk)), pair, precision=HIGHEST)[0:1]
        lane = lax.broadcasted_iota(jnp.int32, (1, LANES), 1)
        tail = jnp.where(lane == 0, pb_x, -jnp.inf)
        imp = jnp.concatenate([imp, tail], axis=1)
        width = imp.shape[1]
        jb = lax.broadcasted_iota(jnp.int32, (1, width), 1)
        cur = qpos // SLC_BLOCK
        forced = (jb == 0) | (jb == cur) | (jb == cur - 1)
        score = jnp.where(jb * SLC_BLOCK <= qpos, jnp.where(forced, FORCE_SCORE, imp), -FORCE_SCORE)
        score = jnp.where(jb < nslc, score, -jnp.inf)
        sel = jnp.zeros((1, LANES), jnp.int32)
        for r in range(SLC_TOPN):
            best = jnp.max(score, axis=1, keepdims=True)
            idx = jnp.min(jnp.where(score == best, jb, width), axis=1, keepdims=True)
            sel = jnp.where(lane == r, idx, sel)
            score = jnp.where(jb == idx, -jnp.inf, score)
        sel_ref[0, h] = jnp.broadcast_to(sel, (8, LANES))


def nsa_sample_select(proj, qn, knorm, pe_k, pe_v, w1k, w2k, w1v, w2v, kcmp, vcmp, hk, qpos):
    nb = proj.shape[0]
    nblk = kcmp.shape[2]
    heads = hk * C_GROUP
    cq, ckv = heads * LANES, hk * LANES
    full = lambda a: pl.BlockSpec(a.shape, lambda b: (0,) * a.ndim)
    per = lambda *s: pl.BlockSpec((1,) + s, lambda b: (b,) + (0,) * len(s))
    colspec = lambda c0, w: pl.BlockSpec((None, 1, w), lambda b: (b, 0, c0 // w))
    proj = proj.reshape(nb, 1, -1)
    out8 = jax.ShapeDtypeStruct((nb, hk, 8, LANES), F32)
    return pl.pallas_call(
        functools.partial(_nsa_s_select_body, hk=hk, qpos=qpos, nblk=nblk),
        grid=(nb,),
        in_specs=[colspec(0, cq), colspec(cq, ckv), colspec(cq + ckv, ckv), colspec(cq + 2 * ckv, ckv),
                  pl.BlockSpec((1, LANES), lambda b: (0, 0)),
                  full(knorm), full(pe_k), full(pe_v), full(w1k), full(w2k), full(w1v), full(w2v),
                  per(hk, nblk, LANES), per(hk, nblk, LANES)],
        out_specs=[per(hk, 8, LANES)] * 4,
        out_shape=[out8, out8, jax.ShapeDtypeStruct((nb, hk, 8, LANES), jnp.int32), out8],
        compiler_params=_params("arbitrary"),
        name="nsa_sample_select",
    )(proj, proj, proj, proj, qn.reshape(1, LANES), knorm, pe_k, pe_v, w1k, w2k, w1v, w2v, kcmp, vcmp)


def _nsa_s_slc_body(sel_ref, pt_ref, q_ref, ksn_ref, vsn_ref, ck_hbm, cv_hbm, o_ref, kbuf, vbuf, sem,
                    *, hk, li, n_past_blocks, per_page, n_pages):
    s = pl.program_id(0)
    ns = pl.num_programs(0)
    rows = SLC_BLOCK * hk

    def copies(step, slot):
        out = []
        for r in range(SLC_TOPN):
            blk = jnp.minimum(sel_ref[step * SLC_TOPN + r], n_past_blocks - 1)
            page = pt_ref[(step // hk) * n_pages + blk // per_page]
            src = pl.ds(pl.multiple_of((blk % per_page) * rows, rows), rows)
            dst = pl.ds(r * rows, rows)
            out.append(pltpu.make_async_copy(ck_hbm.at[page, li, src], kbuf.at[slot, dst], sem.at[0, slot]))
            out.append(pltpu.make_async_copy(cv_hbm.at[page, li, src], vbuf.at[slot, dst], sem.at[1, slot]))
        return out

    @pl.when(s == 0)
    def _():
        for c in copies(0, 0):
            c.start()

    @pl.when(s + 1 < ns)
    def _():
        for c in copies(s + 1, (s + 1) % 2):
            c.start()

    slot = s % 2
    for c in copies(s, slot):
        c.wait()

    h = s % hk
    nkeys = SLC_TOPN * SLC_BLOCK
    first = lax.broadcasted_iota(jnp.int32, (SLC_BLOCK, LANES), 0) == 0
    lane = lax.broadcasted_iota(jnp.int32, (1, nkeys), 1)
    k_new, v_new = ksn_ref[0, 0, 0:1, :], vsn_ref[...]
    ks, vs = [], []
    okf = jnp.ones((1, nkeys), F32)
    for r in range(SLC_TOPN):
        is_new = sel_ref[s * SLC_TOPN + r] >= n_past_blocks
        rws = pl.ds(r * rows + h, SLC_BLOCK, stride=hk)
        ks.append(jnp.where(is_new, jnp.where(first, k_new, 0.0), kbuf[slot, rws, :]))
        vs.append(jnp.where(is_new, jnp.where(first, v_new, 0.0), vbuf[slot, rws, :]))
        okf = jnp.where((lane // SLC_BLOCK == r) & is_new, jnp.where(lane == r * SLC_BLOCK, 1.0, 0.0), okf)
    ok = okf > 0.5
    k = jnp.concatenate(ks, axis=0).astype(BF16)
    v = jnp.concatenate(vs, axis=0).astype(BF16)
    sc = jnp.where(ok, _dot_nt(q_ref[0, 0].astype(BF16), k), NEG_INF)
    p = jnp.where(ok, jnp.exp(sc - jnp.max(sc, axis=1, keepdims=True)), 0.0)
    o_ref[0, 0] = _dot(p.astype(BF16), v) / jnp.sum(p, axis=1, keepdims=True)


def nsa_sample_selected(sel, page_table, q8, ksn, proj, vs_col, cache_k, cache_v, li, hk):
    nb, n_pages = page_table.shape
    n_pool, n_l, page, _, dh = cache_k.shape
    rpp = page * hk
    per_page = page // SLC_BLOCK
    n_past_blocks = n_pages * per_page
    ck = cache_k.reshape(n_pool, n_l, rpp, dh)
    cv = cache_v.reshape(n_pool, n_l, rpp, dh)
    sel_flat = sel[:, :, 0, :SLC_TOPN].reshape(-1)
    proj = proj.reshape(nb, 1, -1)
    per = pl.BlockSpec((1, 1, 8, LANES), lambda s, s_, p_: (s // hk, s % hk, 0, 0))
    buf = pltpu.VMEM((2, SLC_TOPN * SLC_BLOCK * hk, dh), F32)
    grid_spec = pltpu.PrefetchScalarGridSpec(
        num_scalar_prefetch=2,
        grid=(nb * hk,),
        in_specs=[per, per,
                  pl.BlockSpec((None, 1, LANES), lambda s, s_, p_: (s // hk, 0, vs_col // LANES + s % hk)),
                  pl.BlockSpec(memory_space=pl.ANY), pl.BlockSpec(memory_space=pl.ANY)],
        out_specs=per,
        scratch_shapes=[buf, buf, pltpu.SemaphoreType.DMA((2, 2))])
    return pl.pallas_call(
        functools.partial(_nsa_s_slc_body, hk=hk, li=li, n_past_blocks=n_past_blocks, per_page=per_page,
                          n_pages=n_pages),
        grid_spec=grid_spec,
        out_shape=jax.ShapeDtypeStruct((nb, hk, 8, LANES), F32),
        compiler_params=_params("arbitrary"),
        name="nsa_sample_selected",
    )(sel_flat, page_table.reshape(-1), q8, ksn, proj, ck, cv)


def _nsa_s_win_body(q_ref, ocmp_ref, oslc_ref, kw_ref, vw_ref, gate_ref, knorm_ref, wk_ref, wv_ref,
                    o_ref, wko_ref, wvo_ref, *, hk):
    G = C_GROUP
    row = slice(None)
    eye = (lax.broadcasted_iota(jnp.int32, (LANES, LANES), 0)
           == lax.broadcasted_iota(jnp.int32, (LANES, LANES), 1)).astype(F32)
    gcol = _col(eye, _sigmoid(gate_ref[row, :]))
    nh = G * hk
    for h in range(hk):
        lanes = slice(h * LANES, (h + 1) * LANES)
        q8 = q_ref[0, h]
        kn = _rms(kw_ref[row, lanes], knorm_ref[2:3, :])
        v = vw_ref[row, lanes]
        kb, vb = wk_ref[0, 0, :, lanes], wv_ref[0, 0, :, lanes]
        s = _dot_nt(q8.astype(BF16), kb.astype(BF16))
        s_new = jnp.sum(q8 * kn, axis=1, keepdims=True)
        m = jnp.maximum(jnp.max(s, axis=1, keepdims=True), s_new)
        p = jnp.exp(s - m)
        p_new = jnp.exp(s_new - m)
        l = jnp.sum(p, axis=1, keepdims=True) + p_new
        o_win = (_dot(p.astype(BF16), vb.astype(BF16)) + p_new * v) / l
        g0 = gcol[h * G:h * G + 8]
        g1 = gcol[nh + h * G:nh + h * G + 8]
        g2 = gcol[2 * nh + h * G:2 * nh + h * G + 8]
        o = g0 * ocmp_ref[0, h] + g1 * oslc_ref[0, h] + g2 * o_win
        for g in range(G):
            o_ref[row, (h * G + g) * LANES:(h * G + g + 1) * LANES] = o[g:g + 1]
        wko_ref[0, :, lanes] = _shift_in(kb, kn)
        wvo_ref[0, :, lanes] = _shift_in(vb, v)


def nsa_sample_window(q8, ocmp, oslc, proj, kw_col, gate_col, knorm, win_k, win_v, li, hk):
    nb = proj.shape[0]
    Wb = win_k.shape[2]
    ckv = hk * LANES
    wk = win_k.reshape(nb, win_k.shape[1], Wb, ckv)
    wv = win_v.reshape(nb, win_v.shape[1], Wb, ckv)
    per = lambda *s: pl.BlockSpec((1,) + s, lambda b: (b,) + (0,) * len(s))
    proj = proj.reshape(nb, 1, -1)
    cw = hk * C_GROUP * LANES
    o, wko, wvo = pl.pallas_call(
        functools.partial(_nsa_s_win_body, hk=hk),
        grid=(nb,),
        in_specs=[per(hk, 8, LANES), per(hk, 8, LANES), per(hk, 8, LANES),
                  pl.BlockSpec((None, 1, ckv), lambda b: (b, 0, kw_col // ckv)),
                  pl.BlockSpec((None, 1, ckv), lambda b: (b, 0, kw_col // ckv + 1)),
                  pl.BlockSpec((None, 1, LANES), lambda b: (b, 0, gate_col // LANES)),
                  pl.BlockSpec(knorm.shape, lambda b: (0, 0)),
                  pl.BlockSpec((1, 1, Wb, ckv), lambda b: (b, li, 0, 0)),
                  pl.BlockSpec((1, 1, Wb, ckv), lambda b: (b, li, 0, 0))],
        out_specs=[pl.BlockSpec((None, 1, cw), lambda b: (b, 0, 0)),
                   per(Wb, ckv), per(Wb, ckv)],
        out_shape=[jax.ShapeDtypeStruct((nb, 1, cw), F32),
                   jax.ShapeDtypeStruct((nb, Wb, ckv), F32),
                   jax.ShapeDtypeStruct((nb, Wb, ckv), F32)],
        compiler_params=_params("parallel"),
        name="nsa_sample_window",
    )(q8, ocmp, oslc, proj, proj, proj, knorm, wk, wv)
    return o.reshape(nb, cw), wko, wvo


def _pad_cols(w, mult):
    pad = (-w.shape[-1]) % mult
    return jnp.pad(w, ((0, 0),) * (w.ndim - 1) + ((0, pad),)) if pad else w


def kernel(x_prompt, x_sample, state_hgrn, cache_dil_k, cache_dil_v, cache_cmp_k, cache_cmp_v, cache_slc_k, cache_slc_v, cache_win_k, cache_win_v, page_table, norm_mix, norm_mlp, w_in_even, w_out_even, hgrn_lb_logits, hgrn_out_norm, dil_q_norm, dil_k_norm, w_in_odd, w_out_odd, nsa_q_norm, nsa_k_norm, nsa_pe_k, nsa_pe_v, nsa_phi_k1, nsa_phi_k2, nsa_phi_v1, nsa_phi_v2, w_mlp_up, w_mlp_down):
    n, T, D = x_prompt.shape
    nb = x_sample.shape[0]
    assert x_sample.shape[1] == 1
    depth = norm_mix.shape[0]
    a_heads = hgrn_lb_logits.shape[1] // LANES
    b_heads = cache_dil_k.shape[3]
    hk = cache_win_k.shape[3]
    c_heads = hk * C_GROUP
    past_len = page_table.shape[1] * cache_cmp_k.shape[2]
    a_w = a_heads * LANES
    TN = 896
    TM = 512
    TM_IN = 1024 if (n * T) % 1024 == 0 else TM

    lb_cum = jnp.cumsum(jax.nn.softmax(hgrn_lb_logits.astype(F32), axis=0), axis=0)
    lower_bounds = lb_cum - lb_cum[0:1]

    cq, ckv = c_heads * LANES, hk * LANES
    gate_col = cq + 6 * ckv
    gate_w = w_in_odd[:, :, gate_col:].reshape(-1, D, hk, C_GROUP, 3).transpose(0, 1, 4, 2, 3).reshape(-1, D, 3 * c_heads)
    w_in_odd_p = jnp.concatenate([w_in_odd[:, :, :gate_col], _pad_cols(gate_w, LANES)], axis=-1)
    w_in_odd_p = _pad_cols(w_in_odd_p, TN)
    w_in_even_b = w_in_even
    w_out_even_b = w_out_even.astype(BF16)
    w_out_odd_b = w_out_odd.astype(BF16)
    w_up_b = w_mlp_up.astype(BF16)
    w_down_b = w_mlp_down.astype(BF16)
    phi_k1 = nsa_phi_k1.reshape(-1, CMP_LEN, LANES, LANES).astype(BF16)
    phi_v1 = nsa_phi_v1.reshape(-1, CMP_LEN, LANES, LANES).astype(BF16)
    phi_k2 = nsa_phi_k2.astype(BF16)
    phi_v2 = nsa_phi_v2.astype(BF16)

    xp = x_prompt.reshape(n * T, D)
    xs = x_sample.reshape(nb, D)
    outs = {k: [] for k in ("hg_p", "hg_s", "dk_p", "dv_p", "dk_s", "dv_s", "ck_p", "cv_p", "sk_p", "sv_p",
                            "wk_p", "wv_p", "ck_s", "cv_s", "sk_s", "sv_s", "wk_s", "wv_s")}
    dil_bufs = None
    for layer in range(depth):
        li = layer // 2
        if layer % 2 == 0:
            w_in, w_out = w_in_even_b[li], w_out_even_b[li]
            lb, on = lower_bounds[li], hgrn_out_norm[li]
            qn, kn = dil_q_norm[li], dil_k_norm[li]
            pp = rms_matmul(xp, norm_mix[layer], w_in, TM_IN, TN)
            ps = rms_matmul(xs, norm_mix[layer], w_in, nb, TN)
            oa_p, st_p = hgrn_prompt(pp, lb, on, n, T, a_heads)
            ob_p, dk, dv = dil_prompt(pp, qn, kn, n, T, b_heads, 4 * a_w)
            oa_s, st_s = hgrn_sample(ps, lb, on, state_hgrn, li, a_heads)
            ob_s, *dil_bufs = dil_sample(ps, qn, kn, cache_dil_k, cache_dil_v, li, b_heads, 4 * a_w, prev=dil_bufs)
            w_halves = [w_out[:a_w], w_out[a_w:]]
            xp = proj_residual([oa_p, ob_p], w_halves, xp, TM)
            xs = proj_residual([oa_s, ob_s], w_halves, xs, nb)
            outs["hg_p"].append(st_p); outs["hg_s"].append(st_s)
            outs["dk_p"].append(dk.reshape(n, -1, b_heads, LANES)); outs["dv_p"].append(dv.reshape(n, -1, b_heads, LANES))
        else:
            w_in, w_out = w_in_odd_p[li], w_out_odd_b[li]
            knorm = nsa_k_norm[li]
            cmp_w = (nsa_pe_k[li], nsa_pe_v[li], phi_k1[li], phi_k2[li], phi_v1[li], phi_v2[li])
            pp = rms_matmul(xp, norm_mix[layer], w_in, TM_IN, TN)
            ps = rms_matmul(xs, norm_mix[layer], w_in, nb, TN)
            kcmp, vcmp, ksb, ksf, vst, kwb, kwf, vwt = nsa_prep(pp, knorm, *cmp_w, n, T, hk)
            o_p = nsa_prompt(pp, nsa_q_norm[li], kcmp, vcmp, ksb, vst, kwb, vwt, n, T, hk, gate_col)
            kcs, vcs = cmp_pages(page_table, cache_cmp_k, cache_cmp_v, li, knorm, *cmp_w)
            q8, ocmp, sel, ksn = nsa_sample_select(ps, nsa_q_norm[li], knorm, *cmp_w, kcs, vcs, hk, past_len)
            oslc = nsa_sample_selected(sel, page_table, q8, ksn, ps, cq + 3 * ckv, cache_slc_k, cache_slc_v, li, hk)
            o_s, wks, wvs = nsa_sample_window(q8, ocmp, oslc, ps, cq + 4 * ckv, gate_col, knorm,
                                              cache_win_k, cache_win_v, li, hk)
            xp = proj_residual([o_p], [w_out], xp, TM)
            xs = proj_residual([o_s], [w_out], xs, nb)
            kvp = lambda k: pp[:, cq + k * ckv:cq + (k + 1) * ckv].reshape(n, T, hk, LANES)
            kvs = lambda k: ps[:, cq + k * ckv:cq + (k + 1) * ckv].reshape(nb, 1, hk, LANES)
            ww = kwf.shape[1]
            outs["ck_p"].append(kvp(0)); outs["cv_p"].append(kvp(1))
            outs["sk_p"].append(ksf.reshape(n, T, hk, LANES)); outs["sv_p"].append(kvp(3))
            outs["wk_p"].append(kwf.reshape(n, ww, hk, LANES)); outs["wv_p"].append(kvp(5)[:, T - ww:])
            outs["ck_s"].append(kvs(0)); outs["cv_s"].append(kvs(1))
            outs["sk_s"].append(ksn[:, :, 0, :].reshape(nb, 1, hk, LANES)); outs["sv_s"].append(kvs(3))
            outs["wk_s"].append(wks.reshape(nb, -1, hk, LANES)); outs["wv_s"].append(wvs.reshape(nb, -1, hk, LANES))
        xp = mlp_residual(xp, norm_mlp[layer], w_up_b[layer], w_down_b[layer], TM_IN, 512)
        xs = mlp_residual(xs, norm_mlp[layer], w_up_b[layer], w_down_b[layer], nb, 512)
    st = lambda k: jnp.stack(outs[k], axis=1)
    return (xp.reshape(n, T, D), xs.reshape(nb, 1, D),
            st("hg_p"), st("hg_s"), st("dk_p"), st("dv_p"),
            dil_bufs[0].reshape(cache_dil_k.shape), dil_bufs[1].reshape(cache_dil_v.shape),
            st("ck_p"), st("cv_p"), st("sk_p"), st("sv_p"), st("wk_p"), st("wv_p"),
            st("ck_s"), st("cv_s"), st("sk_s"), st("sv_s"), st("wk_s"), st("wv_s"))
```

```python
import functools

import numpy as np
import jax
import jax.numpy as jnp
from jax import lax
from jax.experimental import pallas as pl
from jax.experimental.pallas import tpu as pltpu

F32 = jnp.float32
BF16 = jnp.bfloat16
HIGHEST = lax.Precision.HIGHEST

HEAD_DIM = 128
LANES = 128
RMS_EPS = 1e-6
NEG_INF = -1e30
FORCE_SCORE = 1e6
SCALE = HEAD_DIM ** -0.5
HGRN_CHUNK = 64
HGRN_SUB = 8
DIL_PATTERNS = ((128, 1), (512, 4), (2048, 16))
DIL_MAX_WINDOW = 2048
CMP_LEN = 32
SLC_BLOCK = 64
SLC_TOPN = 16
WIN = 512
C_GROUP = 4
SLC_CHUNK = 512
VMEM_LIMIT = 56 * 1024 * 1024


def _params(*sem):
    return pltpu.CompilerParams(dimension_semantics=sem, vmem_limit_bytes=VMEM_LIMIT)


def _rms(x, w):
    return x * lax.rsqrt(jnp.mean(x * x, axis=-1, keepdims=True) + RMS_EPS) * w


def _sigmoid(x):
    return 1.0 / (1.0 + jnp.exp(-x))


def _dot_nt(a, b):
    return lax.dot_general(a, b, (((1,), (1,)), ((), ())), preferred_element_type=F32)


def _dot(a, b, precision=None):
    return jnp.dot(a, b, preferred_element_type=F32, precision=precision)


def _rms_mm_body(x_ref, g_ref, w_ref, o_ref, h_ref):
    @pl.when(pl.program_id(1) == 0)
    def _():
        h_ref[...] = _rms(x_ref[...], g_ref[...]).astype(BF16)

    o_ref[...] = _dot(h_ref[...], w_ref[...].astype(BF16))


def rms_matmul(x, g, w, li, tm, tn):
    M, D = x.shape
    N = w.shape[2]
    return pl.pallas_call(
        _rms_mm_body,
        grid=(M // tm, N // tn),
        in_specs=[pl.BlockSpec((tm, D), lambda i, j: (i, 0)),
                  pl.BlockSpec((1, D), lambda i, j: (0, 0)),
                  pl.BlockSpec((None, D, tn), lambda i, j: (li, 0, j))],
        out_specs=pl.BlockSpec((tm, tn), lambda i, j: (i, j)),
        out_shape=jax.ShapeDtypeStruct((M, N), F32),
        scratch_shapes=[pltpu.VMEM((tm, D), BF16)],
        compiler_params=_params("parallel", "arbitrary"),
        name="rms_matmul",
    )(x, g.reshape(1, D), w)


def _proj_res_body(*refs, n_in):
    res_ref, o_ref = refs[2 * n_in], refs[2 * n_in + 1]
    acc = res_ref[...]
    for a_ref, w_ref in zip(refs[:n_in], refs[n_in:2 * n_in]):
        acc = acc + _dot(a_ref[...].astype(BF16), w_ref[...])
    o_ref[...] = acc


def proj_residual(lhs, w, li, res, tm):
    M, D = res.shape
    n = len(lhs)
    kp = w.shape[1] // n
    assert all(a.shape[1] == kp for a in lhs)
    ws = [w] * n
    in_specs = [pl.BlockSpec((tm, kp), lambda i: (i, 0)) for _ in lhs]
    in_specs += [pl.BlockSpec((None, kp, D), lambda i, k=k: (li, k, 0)) for k in range(n)]
    in_specs += [pl.BlockSpec((tm, D), lambda i: (i, 0))]
    return pl.pallas_call(
        functools.partial(_proj_res_body, n_in=n),
        grid=(M // tm,),
        in_specs=in_specs,
        out_specs=pl.BlockSpec((tm, D), lambda i: (i, 0)),
        out_shape=jax.ShapeDtypeStruct((M, D), F32),
        compiler_params=_params("parallel"),
        name="proj_residual",
    )(*lhs, *ws, res)


def _mlp_body(x_ref, g_ref, wu_ref, wd_ref, o_ref, h_ref):
    @pl.when(pl.program_id(1) == 0)
    def _():
        x = x_ref[...]
        h_ref[...] = _rms(x, g_ref[...]).astype(BF16)
        o_ref[...] = x

    u = jnp.maximum(_dot(h_ref[...], wu_ref[...]), 0.0)
    o_ref[...] += _dot((u * u).astype(BF16), wd_ref[...])


def mlp_residual(x, g, wu, wd, layer, tm, tf):
    M, D = x.shape
    Fd = wu.shape[2]
    return pl.pallas_call(
        _mlp_body,
        grid=(M // tm, Fd // tf),
        in_specs=[pl.BlockSpec((tm, D), lambda i, j: (i, 0)),
                  pl.BlockSpec((1, D), lambda i, j: (0, 0)),
                  pl.BlockSpec((None, D, tf), lambda i, j: (layer, 0, j)),
                  pl.BlockSpec((None, tf, D), lambda i, j: (layer, j, 0))],
        out_specs=pl.BlockSpec((tm, D), lambda i, j: (i, 0)),
        out_shape=jax.ShapeDtypeStruct((M, D), F32),
        scratch_shapes=[pltpu.VMEM((tm, D), BF16)],
        compiler_params=_params("parallel", "arbitrary"),
        name="mlp_residual",
    )(x, g.reshape(1, D), wu, wd)


def _hgrn_gates(z, lb):
    log_sig = jnp.minimum(z, 0.0) - jnp.log1p(jnp.exp(-jnp.abs(z)))
    a = jnp.log(lb)
    b = jnp.log1p(-lb) + log_sig
    log_f = jnp.maximum(a, b) + jnp.log1p(jnp.exp(-jnp.abs(a - b)))
    series = -(log_f + 0.5 * log_f * log_f + log_f * log_f * log_f * (1.0 / 6.0))
    k = jnp.where(log_f > -0.01, series, 1.0 - jnp.exp(log_f))
    return log_f, k


def _hgrn_out(o, on, g_raw):
    return _rms(o, on) * (g_raw * _sigmoid(g_raw))


def _split3(x):
    hi = x.astype(BF16)
    r1 = x - hi.astype(F32)
    mid = r1.astype(BF16)
    lo = (r1 - mid.astype(F32)).astype(BF16)
    return jnp.concatenate([hi, mid, lo], axis=1)


def _hgrn_body(q_ref, f_ref, i_ref, g_ref, lb_ref, on_ref, o_ref, s_ref, st_ref, *, tb, nh):
    C, SC = HGRN_CHUNK, HGRN_SUB
    t = pl.program_id(2)
    heads = range(nh)

    @pl.when(t == 0)
    def _():
        st_ref[...] = jnp.zeros_like(st_ref)

    on = on_ref[...]
    r_i = lax.broadcasted_iota(jnp.int32, (C, C), 0)
    c_i = lax.broadcasted_iota(jnp.int32, (C, C), 1)
    tril = (r_i >= c_i).astype(BF16)
    row = lax.broadcasted_iota(jnp.int32, (C, LANES), 0)
    levels = []
    bs = C // 2
    while bs >= SC:
        levels.append((bs, (row // bs) % 2 == 1, ((r_i // bs) % 2 == 1) & (c_i // bs == r_i // bs - 1)))
        bs //= 2
    lane_c = lax.broadcasted_iota(jnp.int32, (SC, C), 1)
    sub_c = lax.broadcasted_iota(jnp.int32, (SC, C), 0)

    def chunk(c, carry):
        rows = pl.ds(pl.multiple_of(c * C, C), C)
        sl = [slice(h * LANES, (h + 1) * LANES) for h in heads]
        qr = [q_ref[rows, sl[h]] for h in heads]
        q = [x * _sigmoid(x) for x in qr]
        gates = [_hgrn_gates(f_ref[rows, sl[h]], lb_ref[h]) for h in heads]
        log_f, kk = [g[0] for g in gates], [g[1] for g in gates]
        v = [i_ref[rows, sl[h]] for h in heads]
        vb = [x.astype(BF16) for x in v]
        g3 = [_dot(tril, _split3(log_f[h])) for h in heads]
        G = [x[:, 0:LANES] + x[:, LANES:2 * LANES] + x[:, 2 * LANES:3 * LANES] for x in g3]
        st = [st_ref[h] for h in heads]
        inter = [_dot_nt((q[h] * jnp.exp(G[h])).astype(BF16), st[h].astype(BF16)) for h in heads]
        a_off = [jnp.zeros((C, C), F32) for _ in heads]
        for bs, odd, blk in levels:
            refs = [jnp.concatenate([jnp.broadcast_to(G[h][p + bs - 1:p + bs], (2 * bs, LANES))
                                     for p in range(0, C, 2 * bs)], axis=0) for h in heads]
            d = [G[h] - refs[h] for h in heads]
            qp = [(q[h] * jnp.exp(jnp.where(odd, d[h], NEG_INF))).astype(BF16) for h in heads]
            kp = [(kk[h] * jnp.exp(jnp.where(odd, NEG_INF, -d[h]))).astype(BF16) for h in heads]
            a_off = [a_off[h] + jnp.where(blk, _dot_nt(qp[h], kp[h]), 0.0) for h in heads]
        a_rows = [[] for _ in heads]
        for I in range(C // SC):
            lo = I * SC
            for h in heads:
                GI, qI = G[h][lo:lo + SC], q[h][lo:lo + SC]
                dg = jnp.zeros((SC, C), F32)
                for j in range(SC):
                    e = jnp.exp(jnp.minimum(GI - GI[j:j + 1], 0.0))
                    colv = jnp.sum(qI * e * kk[h][lo + j:lo + j + 1], axis=1, keepdims=True)
                    dg = jnp.where(lane_c == lo + j, colv, dg)
                a_rows[h].append(jnp.where(lane_c <= lo + sub_c, dg, 0.0))
        a = [(a_off[h] + jnp.concatenate(a_rows[h], axis=0)).astype(BF16) for h in heads]
        o = [inter[h] + _dot(a[h], vb[h]) for h in heads]
        Gl = [G[h][C - 1:C] for h in heads]
        kd = [(kk[h] * jnp.exp(Gl[h] - G[h])).astype(BF16) for h in heads]
        upd = [_dot(v[h].T.astype(BF16), kd[h]) for h in heads]
        for h in heads:
            st_ref[h] = jnp.exp(Gl[h]) * st[h] + upd[h]
            o_ref[rows, sl[h]] = _hgrn_out(o[h], on, g_ref[rows, sl[h]]).astype(o_ref.dtype)
        return carry

    lax.fori_loop(0, tb // C, chunk, 0)

    @pl.when(t == pl.num_programs(2) - 1)
    def _():
        for h in heads:
            s_ref[0, h] = st_ref[h].T


def hgrn_prompt(proj, lb, on, n, T, heads, tb=256, nh=4):
    nt = T // tb
    hg = heads // nh
    col = lambda k: pl.BlockSpec((tb, nh * LANES), lambda b, h, t, k=k: (b * nt + t, k * hg + h))
    return pl.pallas_call(
        functools.partial(_hgrn_body, tb=tb, nh=nh),
        grid=(n, hg, nt),
        in_specs=[col(0), col(1), col(2), col(3),
                  pl.BlockSpec((nh, 1, LANES), lambda b, h, t: (h, 0, 0)),
                  pl.BlockSpec((1, LANES), lambda b, h, t: (0, 0))],
        out_specs=[pl.BlockSpec((tb, nh * LANES), lambda b, h, t: (b * nt + t, h)),
                   pl.BlockSpec((1, nh, LANES, LANES), lambda b, h, t: (b, h, 0, 0))],
        out_shape=[jax.ShapeDtypeStruct((n * T, heads * LANES), BF16),
                   jax.ShapeDtypeStruct((n, heads, LANES, LANES), F32)],
        scratch_shapes=[pltpu.VMEM((nh, LANES, LANES), F32)],
        compiler_params=_params("parallel", "parallel", "arbitrary"),
        name="hgrn_prompt",
    )(proj, proj, proj, proj, lb.reshape(heads, 1, LANES), on.reshape(1, LANES))


def _col(eye, row):
    return jnp.sum(eye * row, axis=1, keepdims=True)


def _hgrn_s_body(q_ref, f_ref, i_ref, g_ref, lb_ref, on_ref, s_ref, o_ref, so_ref, *, nb):
    qr = q_ref[...]
    q = qr * _sigmoid(qr)
    log_f, kk = _hgrn_gates(f_ref[...], lb_ref[0])
    v = i_ref[...]
    f = jnp.exp(log_f)
    eye = (lax.broadcasted_iota(jnp.int32, (LANES, LANES), 0)
           == lax.broadcasted_iota(jnp.int32, (LANES, LANES), 1)).astype(F32)
    qf = (q * f).astype(BF16)
    a = jnp.sum(q * kk, axis=1, keepdims=True)
    rows = []
    for b in range(nb):
        S = s_ref[b, 0, 0]
        so_ref[b, 0] = _col(eye, f[b:b + 1]) * S + _col(eye, kk[b:b + 1]) * v[b:b + 1]
        rows.append(_dot(qf, S.astype(BF16))[b:b + 1])
    o = jnp.concatenate(rows, axis=0) + a * v
    o_ref[...] = _hgrn_out(o, on_ref[...], g_ref[...])


def hgrn_sample(proj, lb, on, state, li, heads):
    nb = proj.shape[0]
    col = lambda k: pl.BlockSpec((nb, LANES), lambda h, k=k: (0, k * heads + h))
    return pl.pallas_call(
        functools.partial(_hgrn_s_body, nb=nb),
        grid=(heads,),
        in_specs=[col(0), col(1), col(2), col(3),
                  pl.BlockSpec((1, 1, LANES), lambda h: (h, 0, 0)),
                  pl.BlockSpec((1, LANES), lambda h: (0, 0)),
                  pl.BlockSpec((nb, 1, 1, LANES, LANES), lambda h: (0, li, h, 0, 0))],
        out_specs=[pl.BlockSpec((nb, LANES), lambda h: (0, h)),
                   pl.BlockSpec((nb, 1, LANES, LANES), lambda h: (0, h, 0, 0))],
        out_shape=[jax.ShapeDtypeStruct((nb, heads * LANES), F32),
                   jax.ShapeDtypeStruct((nb, heads, LANES, LANES), F32)],
        compiler_params=_params("parallel"),
        name="hgrn_sample",
    )(proj, proj, proj, proj, lb.reshape(heads, 1, LANES), on.reshape(1, LANES), state)


def _dil_multiplicity(delta):
    c = np.zeros(delta.shape, np.float32)
    for window, dil in DIL_PATTERNS:
        c += ((delta >= 0) & (delta <= window) & (delta % dil == 0)).astype(np.float32)
    return c


def _dil_body(q_ref, k_ref, v_ref, qn_ref, kn_ref, c_ref, o_ref, ko_ref, vo_ref, kn_s, vb_s,
              *, T, tq, W, pad, nh):
    qi = pl.program_id(2)
    span = pad + tq
    sl = [slice(h * LANES, (h + 1) * LANES) for h in range(nh)]

    @pl.when(qi == 0)
    def _():
        kw = kn_ref[...]
        kn_s[:, 0:pad, :] = jnp.zeros((nh, pad, LANES), BF16)
        vb_s[:, 0:pad, :] = jnp.zeros((nh, pad, LANES), BF16)

        def norm(c, carry):
            rows = pl.ds(pl.multiple_of(c * 512, 512), 512)
            dst = pl.ds(pl.multiple_of(pad + c * 512, 512), 512)
            for h in range(nh):
                kn_s[h, dst, :] = _rms(k_ref[rows, sl[h]], kw).astype(BF16)
                vb_s[h, dst, :] = v_ref[rows, sl[h]].astype(BF16)
            return carry

        lax.fori_loop(0, T // 512, norm, 0)
        for h in range(nh):
            ko_ref[0, :, sl[h]] = _rms(k_ref[T - W:, sl[h]], kw)
        vo_ref[0] = v_ref[T - W:, :]

    rows = pl.ds(pl.multiple_of(qi * tq, tq), span)
    q = [(_rms(q_ref[:, sl[h]], qn_ref[...]) * SCALE).astype(BF16) for h in range(nh)]
    sc = [_dot_nt(q[h], kn_s[h, rows, :]) for h in range(nh)]
    c = c_ref[...]
    valid = (c > 0.0) & (lax.broadcasted_iota(jnp.int32, (tq, span), 1) >= pad - qi * tq)
    sc = [jnp.where(valid, x, NEG_INF) for x in sc]
    p = [c * jnp.exp(x - jnp.max(x, axis=1, keepdims=True)) for x in sc]
    o = [_dot(p[h].astype(BF16), vb_s[h, rows, :]) for h in range(nh)]
    for h in range(nh):
        o_ref[:, sl[h]] = (o[h] / jnp.sum(p[h], axis=1, keepdims=True)).astype(o_ref.dtype)


def dil_prompt(proj, qn, kn, n, T, heads, col0, tq=256, nh=2):
    W = min(DIL_MAX_WINDOW, T)
    pad = DIL_MAX_WINDOW
    nt = T // tq
    ctab = jnp.asarray(_dil_multiplicity(np.arange(tq)[:, None] + pad - np.arange(pad + tq)[None, :]))
    hw = nh * LANES
    cb, hg = col0 // hw, heads // nh
    return pl.pallas_call(
        functools.partial(_dil_body, T=T, tq=tq, W=W, pad=pad, nh=nh),
        grid=(n, hg, nt),
        in_specs=[pl.BlockSpec((tq, hw), lambda b, h, t: (b * nt + t, cb + h)),
                  pl.BlockSpec((T, hw), lambda b, h, t: (b, cb + hg + h)),
                  pl.BlockSpec((T, hw), lambda b, h, t: (b, cb + 2 * hg + h)),
                  pl.BlockSpec((1, LANES), lambda b, h, t: (0, 0)),
                  pl.BlockSpec((1, LANES), lambda b, h, t: (0, 0)),
                  pl.BlockSpec((tq, pad + tq), lambda b, h, t: (0, 0))],
        out_specs=[pl.BlockSpec((tq, hw), lambda b, h, t: (b * nt + t, h)),
                   pl.BlockSpec((1, W, hw), lambda b, h, t: (b, 0, h)),
                   pl.BlockSpec((1, W, hw), lambda b, h, t: (b, 0, h))],
        out_shape=[jax.ShapeDtypeStruct((n * T, heads * LANES), BF16),
                   jax.ShapeDtypeStruct((n, W, heads * LANES), F32),
                   jax.ShapeDtypeStruct((n, W, heads * LANES), F32)],
        scratch_shapes=[pltpu.VMEM((nh, T + pad, LANES), BF16), pltpu.VMEM((nh, T + pad, LANES), BF16)],
        compiler_params=_params("parallel", "parallel", "arbitrary"),
        name="dil_prompt",
    )(proj, proj, proj, qn.reshape(1, LANES), kn.reshape(1, LANES), ctab)


def _shift_in(buf, new_row):
    n = buf.shape[0]
    rolled = pltpu.roll(buf, n - 1, 0)
    return jnp.where(lax.broadcasted_iota(jnp.int32, buf.shape, 0) == n - 1, new_row, rolled)


def _dil_s_body(q_ref, k_ref, v_ref, qn_ref, kn_ref, c_ref, *rest, heads, c_new, li, n_l, has_prev):
    n_src = 1 if has_prev else n_l
    ck_refs, cv_refs = rest[:n_src], rest[n_src:2 * n_src]
    o_ref, ko_hbm, vo_hbm, m_s, l_s, acc_s, new_s, sem = rest[2 * n_src + (2 if has_prev else 0):]
    ck_ref, cv_ref = (ck_refs[0], cv_refs[0]) if has_prev else (ck_refs[li], cv_refs[li])
    H = heads
    b, c = pl.program_id(0), pl.program_id(1)
    nrows, cr = ko_hbm.shape[2], ck_ref.shape[2]
    sub = lax.broadcasted_iota(jnp.int32, (H, LANES), 0)

    def heads_on_rows(ref):
        out = jnp.zeros((H, LANES), F32)
        for h in range(H):
            out = jnp.where(sub == h, ref[:, h * LANES:(h + 1) * LANES], out)
        return out

    q8 = _rms(heads_on_rows(q_ref), qn_ref[...]) * SCALE
    kn8 = _rms(heads_on_rows(k_ref), kn_ref[...])
    v8 = heads_on_rows(v_ref)

    def chunk_copies(first):
        if first:
            src, dst = pl.ds(H, cr - H), pl.ds(0, cr - H)
        else:
            src, dst = pl.ds(0, cr), pl.ds(pl.multiple_of(c * cr - H, H), cr)
        out = []
        for l in range(0 if has_prev else n_l):
            out.append(pltpu.make_async_copy(ck_refs[l].at[0, 0, src], ko_hbm.at[b, l, dst], sem.at[0]))
            out.append(pltpu.make_async_copy(cv_refs[l].at[0, 0, src], vo_hbm.at[b, l, dst], sem.at[1]))
        return out

    def tail_copies():
        tail = pl.ds(nrows - H, H)
        out = [pltpu.make_async_copy(new_s.at[0], ko_hbm.at[b, li, tail], sem.at[2]),
               pltpu.make_async_copy(new_s.at[1], vo_hbm.at[b, li, tail], sem.at[3])]
        for l in range(0 if has_prev else n_l):
            if l != li:
                out.append(pltpu.make_async_copy(new_s.at[2], ko_hbm.at[b, l, tail], sem.at[2]))
                out.append(pltpu.make_async_copy(new_s.at[2], vo_hbm.at[b, l, tail], sem.at[3]))
        return out

    @pl.when(c == 0)
    def _():
        m_s[...] = jnp.full_like(m_s, NEG_INF)
        l_s[...] = jnp.zeros_like(l_s)
        acc_s[...] = jnp.zeros_like(acc_s)
        for cp in chunk_copies(True):
            cp.start()

    @pl.when(c > 0)
    def _():
        for cp in chunk_copies(False):
            cp.start()

    kc, vc = ck_ref[0, 0], cv_ref[0, 0]
    s = _dot_nt(q8.astype(BF16), kc.astype(BF16))
    own = lax.broadcasted_iota(jnp.int32, s.shape, 1) % H == lax.broadcasted_iota(jnp.int32, s.shape, 0)
    cm = jnp.where(own, c_ref[0], 0.0)
    s = jnp.where(cm > 0.0, s, NEG_INF)
    m_old = m_s[...]
    m_new = jnp.maximum(m_old, jnp.max(s, axis=1, keepdims=True))
    alpha = jnp.exp(m_old - m_new)
    p = cm * jnp.exp(s - m_new[:, 0:1])
    l_s[...] = alpha * l_s[...] + jnp.sum(p, axis=1, keepdims=True)
    acc_s[...] = alpha * acc_s[...] + _dot(p.astype(BF16), vc.astype(BF16))
    m_s[...] = m_new

    @pl.when(c == 0)
    def _():
        for cp in chunk_copies(True):
            cp.wait()

    @pl.when(c > 0)
    def _():
        for cp in chunk_copies(False):
            cp.wait()

    @pl.when(c == pl.num_programs(1) - 1)
    def _():
        s_new = jnp.sum(q8 * kn8, axis=1, keepdims=True)
        m_f = jnp.maximum(m_s[...], s_new)
        a = jnp.exp(m_s[...] - m_f)
        p_new = c_new * jnp.exp(s_new - m_f)
        o = (a * acc_s[...] + p_new * v8) / (a * l_s[...] + p_new)
        for h in range(H):
            o_ref[:, h * LANES:(h + 1) * LANES] = o[h:h + 1]
        new_s[0] = kn8
        new_s[1] = v8
        new_s[2] = jnp.zeros((H, LANES), F32)
        for cp in tail_copies():
            cp.start()
        for cp in tail_copies():
            cp.wait()


def dil_sample(proj, qn, kn, cache_k, cache_v, li, heads, col0, prev=None, chunk=512):
    nb, n_l, Wb = cache_k.shape[0], cache_k.shape[1], cache_k.shape[2]
    H = heads
    ck = cache_k.reshape(nb, n_l, Wb * H, LANES)
    cv = cache_v.reshape(nb, n_l, Wb * H, LANES)
    nch = Wb // chunk
    c_buf = jnp.asarray(np.repeat(_dil_multiplicity(Wb - np.arange(Wb)), H).reshape(nch, 1, chunk * H))
    c_new = float(_dil_multiplicity(np.zeros((1,), np.int64))[0])
    hw = H * LANES
    proj = proj.reshape(nb, 1, -1)
    col = lambda k: pl.BlockSpec((None, 1, hw), lambda b, c, k=k: (b, 0, col0 // hw + k))
    vec = pl.BlockSpec((1, LANES), lambda b, c: (0, 0))
    cache = lambda l: pl.BlockSpec((1, 1, chunk * H, LANES), lambda b, c, l=l: (b, l, c, 0))
    anyspec = pl.BlockSpec(memory_space=pl.ANY)
    full = jax.ShapeDtypeStruct((nb, n_l, Wb * H, LANES), F32)
    layers = [li] if prev is not None else list(range(n_l))
    args = [proj, proj, proj, qn.reshape(1, LANES), kn.reshape(1, LANES), c_buf]
    args += [ck] * len(layers) + [cv] * len(layers)
    in_specs = [col(0), col(1), col(2), vec, vec, pl.BlockSpec((1, 1, chunk * H), lambda b, c: (c, 0, 0))]
    in_specs += [cache(l) for l in layers] * 2
    aliases = {}
    if prev is not None:
        aliases = {len(args): 1, len(args) + 1: 2}
        args += list(prev)
        in_specs += [anyspec, anyspec]
    o, ok, ov = pl.pallas_call(
        functools.partial(_dil_s_body, heads=H, c_new=c_new, li=li, n_l=n_l, has_prev=prev is not None),
        grid=(nb, nch),
        in_specs=in_specs,
        out_specs=[pl.BlockSpec((None, 1, hw), lambda b, c: (b, 0, 0)), anyspec, anyspec],
        out_shape=[jax.ShapeDtypeStruct((nb, 1, hw), F32), full, full],
        scratch_shapes=[pltpu.VMEM((H, LANES), F32), pltpu.VMEM((H, LANES), F32), pltpu.VMEM((H, LANES), F32),
                        pltpu.VMEM((3, H, LANES), F32), pltpu.SemaphoreType.DMA((4,))],
        input_output_aliases=aliases,
        compiler_params=_params("arbitrary", "arbitrary"),
        name="dil_sample",
    )(*args)
    return o.reshape(nb, hw), ok, ov


def _gelu(x):
    return 0.5 * x * (1.0 + jnp.tanh(0.7978845608028654 * (x + 0.044715 * x * x * x)))


def _compress_rows(load_j, pe_ref, w1_ref, w2_ref, rows):
    acc = jnp.zeros((rows, LANES), F32)
    for j in range(CMP_LEN):
        acc = acc + _dot((load_j(j) + pe_ref[j:j + 1, :]).astype(BF16), w1_ref[j])
    return _dot(_gelu(acc).astype(BF16), w2_ref[...])


def _nsa_prep_body(kc_ref, vc_ref, ks_ref, vs_ref, kw_ref, vw_ref, knorm_ref, pek_ref, pev_ref,
                   w1k_ref, w2k_ref, w1v_ref, w2v_ref,
                   kcmp_ref, vcmp_ref, ksb_ref, ksf_ref, vst_ref, kwb_ref, kwf_ref, vwt_ref, *, T):
    nblk = T // CMP_LEN
    kcmp = _compress_rows(lambda j: kc_ref[pl.ds(j, nblk, stride=CMP_LEN), :], pek_ref, w1k_ref, w2k_ref, nblk)
    kcmp_ref[0, 0] = _rms(kcmp, knorm_ref[0:1, :])
    vcmp_ref[0, 0] = _compress_rows(lambda j: vc_ref[pl.ds(j, nblk, stride=CMP_LEN), :], pev_ref, w1v_ref, w2v_ref, nblk)
    ks_w, kw_w = knorm_ref[1:2, :], knorm_ref[2:3, :]

    kwb_ref[0, 0, 0:WIN, :] = jnp.zeros((WIN, LANES), BF16)
    for i in range(WIN // LANES):
        vwt_ref[0, 0, i] = jnp.zeros((LANES, LANES), BF16)

    def tile(c, carry):
        rows = pl.ds(pl.multiple_of(c * LANES, LANES), LANES)
        ksn = _rms(ks_ref[rows, :], ks_w)
        ksf_ref[0, rows, :] = ksn
        ksb_ref[0, 0, rows, 0:LANES] = ksn.astype(BF16)
        blk_of_row = c * (LANES // SLC_BLOCK) + lax.broadcasted_iota(jnp.int32, (LANES, LANES), 0) // SLC_BLOCK
        ksb_ref[0, 0, rows, LANES:2 * LANES] = (lax.broadcasted_iota(jnp.int32, (LANES, LANES), 1) == blk_of_row).astype(BF16)
        kwb_ref[0, 0, pl.ds(pl.multiple_of(WIN + c * LANES, LANES), LANES), :] = _rms(kw_ref[rows, :], kw_w).astype(BF16)
        vwt_ref[0, 0, WIN // LANES + c] = vw_ref[rows, :].T.astype(BF16)
        return carry

    lax.fori_loop(0, T // LANES, tile, 0)

    def chunk(c, carry):
        rows = pl.ds(pl.multiple_of(c * SLC_CHUNK, SLC_CHUNK), SLC_CHUNK)
        vst_ref[0, 0, c] = vs_ref[rows, :].T.astype(BF16)
        return carry

    lax.fori_loop(0, T // SLC_CHUNK, chunk, 0)
    ww = min(WIN, T)
    kwf_ref[0] = _rms(kw_ref[T - ww:, :], kw_w)


def nsa_prep(proj, knorm, pe_k, pe_v, w1k, w2k, w1v, w2v, n, T, hk):
    cb = (hk * C_GROUP * HEAD_DIM) // LANES
    nblk = T // CMP_LEN
    nt = T // LANES
    ww = min(WIN, T)
    col = lambda k: pl.BlockSpec((T, LANES), lambda b, h, k=k: (b, cb + k * hk + h))
    full = lambda a: pl.BlockSpec(a.shape, lambda b, h: (0,) * a.ndim)
    per = lambda *s: pl.BlockSpec((1, 1) + s, lambda b, h: (b, h) + (0,) * len(s))
    return pl.pallas_call(
        functools.partial(_nsa_prep_body, T=T),
        grid=(n, hk),
        in_specs=[col(0), col(1), col(2), col(3), col(4), col(5),
                  full(knorm), full(pe_k), full(pe_v), full(w1k), full(w2k), full(w1v), full(w2v)],
        out_specs=[per(nblk, LANES), per(nblk, LANES), per(T, 2 * LANES),
                   pl.BlockSpec((1, T, LANES), lambda b, h: (b, 0, h)),
                   per(T // SLC_CHUNK, LANES, SLC_CHUNK), per(T + WIN, LANES),
                   pl.BlockSpec((1, ww, LANES), lambda b, h: (b, 0, h)),
                   per(nt + WIN // LANES, LANES, LANES)],
        out_shape=[jax.ShapeDtypeStruct((n, hk, nblk, LANES), F32),
                   jax.ShapeDtypeStruct((n, hk, nblk, LANES), F32),
                   jax.ShapeDtypeStruct((n, hk, T, 2 * LANES), BF16),
                   jax.ShapeDtypeStruct((n, T, hk * LANES), F32),
                   jax.ShapeDtypeStruct((n, hk, T // SLC_CHUNK, LANES, SLC_CHUNK), BF16),
                   jax.ShapeDtypeStruct((n, hk, T + WIN, LANES), BF16),
                   jax.ShapeDtypeStruct((n, ww, hk * LANES), F32),
                   jax.ShapeDtypeStruct((n, hk, nt + WIN // LANES, LANES, LANES), BF16)],
        compiler_params=_params("parallel", "parallel"),
        name="nsa_prep",
    )(proj, proj, proj, proj, proj, proj, knorm, pe_k, pe_v, w1k, w2k, w1v, w2v)


def _tile4(x):
    return jnp.concatenate([x] * C_GROUP, axis=1)


def _nsa_body(q_ref, gate_ref, qn_ref, kcmp_ref, vcmp_ref, ks_ref, vst_ref, kw_ref, vwt_ref, o_ref,
              vct_s, pb_s, gt_s, *, tq, nblk, nslc, hk_n, hpb):
    G = C_GROUP
    HB = range(hpb)
    gw = G * LANES
    hk0 = pl.program_id(1) * hpb
    qi = pl.program_id(2)
    t0 = qi * tq

    @pl.when(qi == 0)
    def _():
        for h in HB:
            vct_s[h] = vcmp_ref[0, h].T.astype(BF16)

    qw = qn_ref[...]
    q4 = [jnp.concatenate([(_rms(q_ref[:, h * gw + g * LANES:h * gw + (g + 1) * LANES], qw) * SCALE).astype(BF16)
                           for g in range(G)], axis=0) for h in HB]

    st = [_dot_nt(kcmp_ref[0, h].astype(BF16), q4[h]) for h in HB]
    blk = lax.broadcasted_iota(jnp.int32, (nblk, G * tq), 0)
    tpos = t0 + (lax.broadcasted_iota(jnp.int32, (nblk, G * tq), 1) & (tq - 1))
    valid = (blk + 1) * CMP_LEN - 1 <= tpos
    st = [jnp.where(valid, x, NEG_INF) for x in st]
    p = [jnp.where(valid, jnp.exp(x - jnp.max(x, axis=0, keepdims=True)), 0.0) for x in st]
    p = [x / jnp.maximum(jnp.sum(x, axis=0, keepdims=True), 1.0) for x in p]
    o_cmp = [_dot(vct_s[h], p[h].astype(BF16)) for h in HB]
    ratio = SLC_BLOCK // CMP_LEN
    imp = []
    for h in HB:
        pb = p[h][:, 0:tq]
        for g in range(1, G):
            pb = pb + p[h][:, g * tq:(g + 1) * tq]
        parts = []
        for i in range(tq // LANES):
            pb_s[h, i] = pb[:, i * LANES:(i + 1) * LANES]
            part = pb_s[h, i, pl.ds(0, nslc, stride=ratio), :]
            for r in range(1, ratio):
                part = part + pb_s[h, i, pl.ds(r, nslc, stride=ratio), :]
            parts.append(part)
        imp.append(jnp.concatenate(parts, axis=1))

    jb = lax.broadcasted_iota(jnp.int32, (nslc, tq), 0)
    tp = t0 + lax.broadcasted_iota(jnp.int32, (nslc, tq), 1)
    cur = tp // SLC_BLOCK
    forced = (jb == 0) | (jb == cur) | (jb == cur - 1)
    in_past = jb * SLC_BLOCK <= tp
    score = [jnp.where(in_past, jnp.where(forced, FORCE_SCORE, x), -FORCE_SCORE) for x in imp]
    rank = [jnp.zeros((nslc, tq), F32) for _ in HB]
    for jp in range(nslc):
        later = jb > jp
        for h in HB:
            row = score[h][jp:jp + 1, :]
            rank[h] = rank[h] + ((row > score[h]) | ((row == score[h]) & later)).astype(F32)
    q_aug = []
    for h in HB:
        bias = jnp.where(rank[h] < float(min(SLC_TOPN, nslc)), 0.0, NEG_INF)
        bias = jnp.concatenate([bias, jnp.zeros((LANES - nslc, tq), F32)], axis=0).T.astype(BF16)
        q_aug.append(jnp.concatenate([q4[h], jnp.concatenate([bias] * G, axis=0)], axis=1))

    KC = SLC_CHUNK

    def slc_step(kc, carry, causal):
        m, l, acc = carry
        rows = pl.ds(pl.multiple_of(kc * KC, KC), KC)
        s = [_dot_nt(ks_ref[0, h, rows, :], q_aug[h]) for h in HB]
        if causal:
            kpos = kc * KC + lax.broadcasted_iota(jnp.int32, (KC, tq), 0)
            hide = _tile4(jnp.where(kpos <= t0 + lax.broadcasted_iota(jnp.int32, (KC, tq), 1), 0.0, NEG_INF))
            s = [x + hide for x in s]
        m_new = [jnp.maximum(m[h], jnp.max(s[h], axis=0, keepdims=True)) for h in HB]
        alpha = [jnp.exp(m[h] - m_new[h]) for h in HB]
        pp = [jnp.exp(s[h] - m_new[h]) for h in HB]
        l = [alpha[h] * l[h] + jnp.sum(pp[h], axis=0, keepdims=True) for h in HB]
        pv = [_dot(vst_ref[0, h, kc], pp[h].astype(BF16)) for h in HB]
        acc = [alpha[h] * acc[h] + pv[h] for h in HB]
        return m_new, l, acc

    init = ([jnp.full((1, G * tq), NEG_INF, F32) for _ in HB], [jnp.zeros((1, G * tq), F32) for _ in HB],
            [jnp.zeros((LANES, G * tq), F32) for _ in HB])
    last = (t0 + tq - 1) // KC
    carry = lax.fori_loop(0, last, functools.partial(slc_step, causal=False), init)
    _, l_s, acc_s = slc_step(last, carry, causal=True)
    o_slc = [acc_s[h] / l_s[h] for h in HB]

    wspan = WIN + tq
    wsub = lax.broadcasted_iota(jnp.int32, (wspan, tq), 0)
    dist = lax.broadcasted_iota(jnp.int32, (wspan, tq), 1) + WIN - wsub
    wbias = _tile4(jnp.where((dist >= 0) & (dist <= WIN) & (wsub >= WIN - t0), 0.0, NEG_INF))
    wrows = pl.ds(pl.multiple_of(t0, tq), wspan)
    sw = [_dot_nt(kw_ref[0, h, wrows, :], q4[h]) + wbias for h in HB]
    pw = [jnp.exp(x - jnp.max(x, axis=0, keepdims=True)) for x in sw]
    vw_t = [jnp.concatenate([vwt_ref[0, h, qi * (tq // LANES) + i] for i in range(wspan // LANES)], axis=1)
            for h in HB]
    o_win = [_dot(vw_t[h], pw[h].astype(BF16)) / jnp.sum(pw[h], axis=0, keepdims=True) for h in HB]

    gt_s[...] = _sigmoid(gate_ref[...]).T
    nh = G * hk_n
    for h in HB:
        for g in range(G):
            sl = slice(g * tq, (g + 1) * tq)
            head = (hk0 + h) * G + g
            g0 = gt_s[pl.ds(head, 1), :]
            g1 = gt_s[pl.ds(nh + head, 1), :]
            g2 = gt_s[pl.ds(2 * nh + head, 1), :]
            o = g0 * o_cmp[h][:, sl] + g1 * o_slc[h][:, sl] + g2 * o_win[h][:, sl]
            o_ref[:, h * gw + g * LANES:h * gw + (g + 1) * LANES] = o.T.astype(o_ref.dtype)


def nsa_prompt(proj, qn, kcmp, vcmp, ksb, vst, kwb, vwt, n, T, hk, gate_col, tq=256, hpb=2):
    nblk, nslc, nt = T // CMP_LEN, T // SLC_BLOCK, T // tq
    gw = hpb * C_GROUP * LANES
    per = lambda *s: pl.BlockSpec((1, hpb) + s, lambda b, h, t: (b, h) + (0,) * len(s))
    return pl.pallas_call(
        functools.partial(_nsa_body, tq=tq, nblk=nblk, nslc=nslc, hk_n=hk, hpb=hpb),
        grid=(n, hk // hpb, nt),
        in_specs=[pl.BlockSpec((tq, gw), lambda b, h, t: (b * nt + t, h)),
                  pl.BlockSpec((tq, LANES), lambda b, h, t: (b * nt + t, gate_col // LANES)),
                  pl.BlockSpec((1, LANES), lambda b, h, t: (0, 0)),
                  per(nblk, LANES), per(nblk, LANES), per(T, 2 * LANES), per(T // SLC_CHUNK, LANES, SLC_CHUNK),
                  per(T + WIN, LANES), per((T + WIN) // LANES, LANES, LANES)],
        out_specs=pl.BlockSpec((tq, gw), lambda b, h, t: (b * nt + t, h)),
        out_shape=jax.ShapeDtypeStruct((n * T, hk * C_GROUP * LANES), BF16),
        scratch_shapes=[pltpu.VMEM((hpb, LANES, nblk), BF16), pltpu.VMEM((hpb, tq // LANES, nblk, LANES), F32),
                        pltpu.VMEM((LANES, tq), F32)],
        compiler_params=_params("parallel", "parallel", "arbitrary"),
        name="nsa_prompt",
    )(proj, proj, qn.reshape(1, LANES), kcmp, vcmp, ksb, vst, kwb, vwt)


def _cmp_pages_body(pt_ref, ck_hbm, cv_hbm, knorm_ref, pek_ref, pev_ref, w1k_ref, w2k_ref, w1v_ref, w2v_ref,
                    ko_ref, vo_ref, kbuf, vbuf, fold_s, sem, *, li, P, hk, rows_per_page):
    s = pl.program_id(0)
    ns = pl.num_programs(0)
    gpp = rows_per_page // hk // CMP_LEN

    def copies(step, slot):
        out = []
        for p in range(P):
            page = pt_ref[step * P + p]
            dst = pl.ds(p * gpp, gpp)
            out.append(pltpu.make_async_copy(ck_hbm.at[page, li], kbuf.at[slot, dst], sem.at[0, slot]))
            out.append(pltpu.make_async_copy(cv_hbm.at[page, li], vbuf.at[slot, dst], sem.at[1, slot]))
        return out

    @pl.when(s == 0)
    def _():
        for c in copies(0, 0):
            c.start()

    @pl.when(s + 1 < ns)
    def _():
        for c in copies(s + 1, (s + 1) % 2):
            c.start()

    slot = s % 2
    for c in copies(s, slot):
        c.wait()

    groups = P * (rows_per_page // hk // CMP_LEN)
    tpv = 8 // hk
    own = [lax.broadcasted_iota(jnp.int32, (groups * 8, LANES), 0) % 8 // hk == u for u in range(tpv)]

    def compress(buf, pe_ref, w1_ref, w2_ref):
        acc = jnp.zeros((groups * 8, LANES), F32)
        for jp in range(CMP_LEN // tpv):
            x = buf[slot, :, pl.ds(jp * 8, 8), :] + pe_ref[jp]
            y = _dot(x.reshape(groups * 8, LANES).astype(BF16), w1_ref[jp])
            part = y[:, 0:LANES]
            for u in range(1, tpv):
                part = jnp.where(own[u], y[:, u * LANES:(u + 1) * LANES], part)
            acc = acc + part
        tot = acc
        for u in range(1, tpv):
            tot = tot + pltpu.roll(acc, u * hk, 0)
        return _dot(_gelu(tot).astype(BF16), w2_ref[...])

    fold_s[0] = _rms(compress(kbuf, pek_ref, w1k_ref, w2k_ref), knorm_ref[0:1, :])
    fold_s[1] = compress(vbuf, pev_ref, w1v_ref, w2v_ref)
    for h in range(hk):
        rows = pl.ds(8 - hk + h, groups, stride=8)
        ko_ref[0, h] = fold_s[0, rows, :]
        vo_ref[0, h] = fold_s[1, rows, :]


def cmp_pages(page_table, cache_k, cache_v, li, knorm, pe_k, pe_v, w1k, w2k, w1v, w2v, P=16):
    nb, n_pages = page_table.shape
    n_pool, n_l, page, hk, dh = cache_k.shape
    rpp = page * hk
    gpp, grows = page // CMP_LEN, CMP_LEN * hk
    ck = cache_k.reshape(n_pool, n_l, gpp, grows, dh)
    cv = cache_v.reshape(n_pool, n_l, gpp, grows, dh)
    tpv = 8 // hk
    slab_pe = lambda pe: jnp.repeat(pe, hk, axis=0).reshape(CMP_LEN // tpv, 8, dh)
    slab_w = lambda w: w.reshape(CMP_LEN // tpv, tpv, dh, dh).transpose(0, 2, 1, 3).reshape(CMP_LEN // tpv, dh, tpv * dh)
    pe_k, pe_v, w1k, w1v = slab_pe(pe_k), slab_pe(pe_v), slab_w(w1k), slab_w(w1v)
    steps_per_b = n_pages // P
    blocks = P * gpp
    nblk = n_pages * gpp
    full = lambda a: pl.BlockSpec(a.shape, lambda s, pt: (0,) * a.ndim)
    out_spec = pl.BlockSpec((1, hk, blocks, dh), lambda s, pt: (s // steps_per_b, 0, s % steps_per_b, 0))
    grid_spec = pltpu.PrefetchScalarGridSpec(
        num_scalar_prefetch=1,
        grid=(nb * steps_per_b,),
        in_specs=[pl.BlockSpec(memory_space=pl.ANY), pl.BlockSpec(memory_space=pl.ANY),
                  full(knorm), full(pe_k), full(pe_v), full(w1k), full(w2k), full(w1v), full(w2v)],
        out_specs=[out_spec, out_spec],
        scratch_shapes=[pltpu.VMEM((2, blocks, grows, dh), F32), pltpu.VMEM((2, blocks, grows, dh), F32),
                        pltpu.VMEM((2, blocks * 8, dh), F32), pltpu.SemaphoreType.DMA((2, 2))])
    return pl.pallas_call(
        functools.partial(_cmp_pages_body, li=li, P=P, hk=hk, rows_per_page=rpp),
        grid_spec=grid_spec,
        out_shape=[jax.ShapeDtypeStruct((nb, hk, nblk, dh), F32)] * 2,
        compiler_params=_params("arbitrary"),
        name="cmp_pages",
    )(page_table.reshape(-1), ck, cv, knorm, pe_k, pe_v, w1k, w2k, w1v, w2v)


def _nsa_s_select_body(q_ref, kc_ref, vc_ref, ks_ref, qn_ref, knorm_ref, pek_ref, pev_ref,
                       w1k_ref, w2k_ref, w1v_ref, w2v_ref, kcmp_ref, vcmp_ref,
                       qo_ref, ocmp_ref, sel_ref, kso_ref, *, hk, qpos, nblk):
    G = C_GROUP
    row = slice(None)
    qw = qn_ref[...]
    nslc = (nblk + 1 + 1) // 2
    lanes_blk = lax.broadcasted_iota(jnp.int32, (8, nblk), 1)
    valid = (lanes_blk + 1) * CMP_LEN - 1 <= qpos
    valid_x = (jnp.full((8, 1), (nblk + 1) * CMP_LEN - 1, jnp.int32) <= qpos)
    pair = (lax.broadcasted_iota(jnp.int32, (nblk, nblk // 2), 0) // 2
            == lax.broadcasted_iota(jnp.int32, (nblk, nblk // 2), 1)).astype(F32)
    sub8 = lax.broadcasted_iota(jnp.int32, (8, 1), 0)
    pe_rest_k = jnp.zeros((8, LANES), F32)
    pe_rest_v = jnp.zeros((8, LANES), F32)
    for j in range(1, CMP_LEN):
        pe_rest_k = pe_rest_k + _dot(jnp.broadcast_to(pek_ref[j:j + 1, :], (8, LANES)).astype(BF16), w1k_ref[j])
        pe_rest_v = pe_rest_v + _dot(jnp.broadcast_to(pev_ref[j:j + 1, :], (8, LANES)).astype(BF16), w1v_ref[j])
    for h in range(hk):
        qs = [_rms(q_ref[row, (h * G + g) * LANES:(h * G + g + 1) * LANES], qw) * SCALE for g in range(G)]
        q8 = jnp.concatenate(qs + [jnp.zeros((8 - G, LANES), F32)], axis=0)
        qo_ref[0, h] = q8
        kso_ref[0, h] = jnp.broadcast_to(_rms(ks_ref[row, h * LANES:(h + 1) * LANES], knorm_ref[1:2, :]), (8, LANES))
        xk = jnp.broadcast_to(kc_ref[row, h * LANES:(h + 1) * LANES] + pek_ref[0:1, :], (8, LANES))
        xv = jnp.broadcast_to(vc_ref[row, h * LANES:(h + 1) * LANES] + pev_ref[0:1, :], (8, LANES))
        k_x = _dot(_gelu(_dot(xk.astype(BF16), w1k_ref[0]) + pe_rest_k).astype(BF16), w2k_ref[...])
        k_x = _rms(k_x, knorm_ref[0:1, :])
        v_x = _dot(_gelu(_dot(xv.astype(BF16), w1v_ref[0]) + pe_rest_v).astype(BF16), w2v_ref[...])
        s = jnp.where(valid, _dot_nt(q8.astype(BF16), kcmp_ref[0, h].astype(BF16)), NEG_INF)
        s_x = jnp.where(valid_x, jnp.sum(q8 * k_x, axis=1, keepdims=True), NEG_INF)
        m = jnp.maximum(jnp.max(s, axis=1, keepdims=True), s_x)
        p = jnp.where(valid, jnp.exp(s - m), 0.0)
        p_x = jnp.where(valid_x, jnp.exp(s_x - m), 0.0)
        den = jnp.maximum(jnp.sum(p, axis=1, keepdims=True) + p_x, 1.0)
        p = jnp.where(sub8 < G, p / den, 0.0)
        p_x = jnp.where(sub8 < G, p_x / den, 0.0)
        ocmp_ref[0, h] = _dot(p.astype(BF16), vcmp_ref[0, h].astype(BF16)) + p_x * v_x
        pb = jnp.sum(p, axis=0, keepdims=True)
        pb_x = jnp.sum(p_x, axis=0, keepdims=True)
        imp = _dot(jnp.broadcast_to(pb, (8, nblk)), pair, precision=HIGHEST)[0:1]
        lane = lax.broadcasted_iota(jnp.int32, (1, LANES), 1)
        tail = jnp.where(lane == 0, pb_x, -jnp.inf)
        imp = jnp.concatenate([imp, tail], axis=1)
        width = imp.shape[1]
        jb = lax.broadcasted_iota(jnp.int32, (1, width), 1)
        cur = qpos // SLC_BLOCK
        forced = (jb == 0) | (jb == cur) | (jb == cur - 1)
        score = jnp.where(jb * SLC_BLOCK <= qpos, jnp.where(forced, FORCE_SCORE, imp), -FORCE_SCORE)
        score = jnp.where(jb < nslc, score, -jnp.inf)
        sel = jnp.zeros((1, LANES), jnp.int32)
        for r in range(SLC_TOPN):
            best = jnp.max(score, axis=1, keepdims=True)
            idx = jnp.min(jnp.where(score == best, jb, width), axis=1, keepdims=True)
            sel = jnp.where(lane == r, idx, sel)
            score = jnp.where(jb == idx, -jnp.inf, score)
        sel_ref[0, h] = jnp.broadcast_to(sel, (8, LANES))


def nsa_sample_select(proj, qn, knorm, pe_k, pe_v, w1k, w2k, w1v, w2v, kcmp, vcmp, hk, qpos):
    nb = proj.shape[0]
    nblk = kcmp.shape[2]
    heads = hk * C_GROUP
    cq, ckv = heads * LANES, hk * LANES
    full = lambda a: pl.BlockSpec(a.shape, lambda b: (0,) * a.ndim)
    per = lambda *s: pl.BlockSpec((1,) + s, lambda b: (b,) + (0,) * len(s))
    colspec = lambda c0, w: pl.BlockSpec((None, 1, w), lambda b: (b, 0, c0 // w))
    proj = proj.reshape(nb, 1, -1)
    out8 = jax.ShapeDtypeStruct((nb, hk, 8, LANES), F32)
    return pl.pallas_call(
        functools.partial(_nsa_s_select_body, hk=hk, qpos=qpos, nblk=nblk),
        grid=(nb,),
        in_specs=[colspec(0, cq), colspec(cq, ckv), colspec(cq + ckv, ckv), colspec(cq + 2 * ckv, ckv),
                  pl.BlockSpec((1, LANES), lambda b: (0, 0)),
                  full(knorm), full(pe_k), full(pe_v), full(w1k), full(w2k), full(w1v), full(w2v),
                  per(hk, nblk, LANES), per(hk, nblk, LANES)],
        out_specs=[per(hk, 8, LANES)] * 4,
        out_shape=[out8, out8, jax.ShapeDtypeStruct((nb, hk, 8, LANES), jnp.int32), out8],
        compiler_params=_params("arbitrary"),
        name="nsa_sample_select",
    )(proj, proj, proj, proj, qn.reshape(1, LANES), knorm, pe_k, pe_v, w1k, w2k, w1v, w2v, kcmp, vcmp)


def _nsa_s_slc_body(sel_ref, pt_ref, q_ref, ksn_ref, vsn_ref, ck_hbm, cv_hbm, o_ref, kbuf, vbuf, sem,
                    *, hk, li, n_past_blocks, per_page, n_pages):
    s = pl.program_id(0)
    ns = pl.num_programs(0)
    rows = SLC_BLOCK * hk

    def copies(step, slot):
        out = []
        for r in range(SLC_TOPN):
            blk = jnp.minimum(sel_ref[step * SLC_TOPN + r], n_past_blocks - 1)
            page = pt_ref[(step // hk) * n_pages + blk // per_page]
            src = pl.ds(pl.multiple_of((blk % per_page) * rows, rows), rows)
            dst = pl.ds(r * rows, rows)
            out.append(pltpu.make_async_copy(ck_hbm.at[page, li, src], kbuf.at[slot, dst], sem.at[0, slot]))
            out.append(pltpu.make_async_copy(cv_hbm.at[page, li, src], vbuf.at[slot, dst], sem.at[1, slot]))
        return out

    @pl.when(s == 0)
    def _():
        for c in copies(0, 0):
            c.start()

    @pl.when(s + 1 < ns)
    def _():
        for c in copies(s + 1, (s + 1) % 2):
            c.start()

    slot = s % 2
    for c in copies(s, slot):
        c.wait()

    h = s % hk
    nkeys = SLC_TOPN * SLC_BLOCK
    first = lax.broadcasted_iota(jnp.int32, (SLC_BLOCK, LANES), 0) == 0
    lane = lax.broadcasted_iota(jnp.int32, (1, nkeys), 1)
    k_new, v_new = ksn_ref[0, 0, 0:1, :], vsn_ref[...]
    ks, vs = [], []
    okf = jnp.ones((1, nkeys), F32)
    for r in range(SLC_TOPN):
        is_new = sel_ref[s * SLC_TOPN + r] >= n_past_blocks
        rws = pl.ds(r * rows + h, SLC_BLOCK, stride=hk)
        ks.append(jnp.where(is_new, jnp.where(first, k_new, 0.0), kbuf[slot, rws, :]))
        vs.append(jnp.where(is_new, jnp.where(first, v_new, 0.0), vbuf[slot, rws, :]))
        okf = jnp.where((lane // SLC_BLOCK == r) & is_new, jnp.where(lane == r * SLC_BLOCK, 1.0, 0.0), okf)
    ok = okf > 0.5
    k = jnp.concatenate(ks, axis=0).astype(BF16)
    v = jnp.concatenate(vs, axis=0).astype(BF16)
    sc = jnp.where(ok, _dot_nt(q_ref[0, 0].astype(BF16), k), NEG_INF)
    p = jnp.where(ok, jnp.exp(sc - jnp.max(sc, axis=1, keepdims=True)), 0.0)
    o_ref[0, 0] = _dot(p.astype(BF16), v) / jnp.sum(p, axis=1, keepdims=True)


def nsa_sample_selected(sel, page_table, q8, ksn, proj, vs_col, cache_k, cache_v, li, hk):
    nb, n_pages = page_table.shape
    n_pool, n_l, page, _, dh = cache_k.shape
    rpp = page * hk
    per_page = page // SLC_BLOCK
    n_past_blocks = n_pages * per_page
    ck = cache_k.reshape(n_pool, n_l, rpp, dh)
    cv = cache_v.reshape(n_pool, n_l, rpp, dh)
    sel_flat = sel[:, :, 0, :SLC_TOPN].reshape(-1)
    proj = proj.reshape(nb, 1, -1)
    per = pl.BlockSpec((1, 1, 8, LANES), lambda s, s_, p_: (s // hk, s % hk, 0, 0))
    buf = pltpu.VMEM((2, SLC_TOPN * SLC_BLOCK * hk, dh), F32)
    grid_spec = pltpu.PrefetchScalarGridSpec(
        num_scalar_prefetch=2,
        grid=(nb * hk,),
        in_specs=[per, per,
                  pl.BlockSpec((None, 1, LANES), lambda s, s_, p_: (s // hk, 0, vs_col // LANES + s % hk)),
                  pl.BlockSpec(memory_space=pl.ANY), pl.BlockSpec(memory_space=pl.ANY)],
        out_specs=per,
        scratch_shapes=[buf, buf, pltpu.SemaphoreType.DMA((2, 2))])
    return pl.pallas_call(
        functools.partial(_nsa_s_slc_body, hk=hk, li=li, n_past_blocks=n_past_blocks, per_page=per_page,
                          n_pages=n_pages),
        grid_spec=grid_spec,
        out_shape=jax.ShapeDtypeStruct((nb, hk, 8, LANES), F32),
        compiler_params=_params("arbitrary"),
        name="nsa_sample_selected",
    )(sel_flat, page_table.reshape(-1), q8, ksn, proj, ck, cv)


def _nsa_s_win_body(q_ref, ocmp_ref, oslc_ref, kw_ref, vw_ref, gate_ref, knorm_ref, wk_ref, wv_ref,
                    o_ref, wko_ref, wvo_ref, *, hk):
    G = C_GROUP
    row = slice(None)
    eye = (lax.broadcasted_iota(jnp.int32, (LANES, LANES), 0)
           == lax.broadcasted_iota(jnp.int32, (LANES, LANES), 1)).astype(F32)
    gcol = _col(eye, _sigmoid(gate_ref[row, :]))
    nh = G * hk
    for h in range(hk):
        lanes = slice(h * LANES, (h + 1) * LANES)
        q8 = q_ref[0, h]
        kn = _rms(kw_ref[row, lanes], knorm_ref[2:3, :])
        v = vw_ref[row, lanes]
        kb, vb = wk_ref[0, 0, :, lanes], wv_ref[0, 0, :, lanes]
        s = _dot_nt(q8.astype(BF16), kb.astype(BF16))
        s_new = jnp.sum(q8 * kn, axis=1, keepdims=True)
        m = jnp.maximum(jnp.max(s, axis=1, keepdims=True), s_new)
        p = jnp.exp(s - m)
        p_new = jnp.exp(s_new - m)
        l = jnp.sum(p, axis=1, keepdims=True) + p_new
        o_win = (_dot(p.astype(BF16), vb.astype(BF16)) + p_new * v) / l
        g0 = gcol[h * G:h * G + 8]
        g1 = gcol[nh + h * G:nh + h * G + 8]
        g2 = gcol[2 * nh + h * G:2 * nh + h * G + 8]
        o = g0 * ocmp_ref[0, h] + g1 * oslc_ref[0, h] + g2 * o_win
        for g in range(G):
            o_ref[row, (h * G + g) * LANES:(h * G + g + 1) * LANES] = o[g:g + 1]
        wko_ref[0, :, lanes] = _shift_in(kb, kn)
        wvo_ref[0, :, lanes] = _shift_in(vb, v)


def nsa_sample_window(q8, ocmp, oslc, proj, kw_col, gate_col, knorm, win_k, win_v, li, hk):
    nb = proj.shape[0]
    Wb = win_k.shape[2]
    ckv = hk * LANES
    wk = win_k.reshape(nb, win_k.shape[1], Wb, ckv)
    wv = win_v.reshape(nb, win_v.shape[1], Wb, ckv)
    per = lambda *s: pl.BlockSpec((1,) + s, lambda b: (b,) + (0,) * len(s))
    proj = proj.reshape(nb, 1, -1)
    cw = hk * C_GROUP * LANES
    o, wko, wvo = pl.pallas_call(
        functools.partial(_nsa_s_win_body, hk=hk),
        grid=(nb,),
        in_specs=[per(hk, 8, LANES), per(hk, 8, LANES), per(hk, 8, LANES),
                  pl.BlockSpec((None, 1, ckv), lambda b: (b, 0, kw_col // ckv)),
                  pl.BlockSpec((None, 1, ckv), lambda b: (b, 0, kw_col // ckv + 1)),
                  pl.BlockSpec((None, 1, LANES), lambda b: (b, 0, gate_col // LANES)),
                  pl.BlockSpec(knorm.shape, lambda b: (0, 0)),
                  pl.BlockSpec((1, 1, Wb, ckv), lambda b: (b, li, 0, 0)),
                  pl.BlockSpec((1, 1, Wb, ckv), lambda b: (b, li, 0, 0))],
        out_specs=[pl.BlockSpec((None, 1, cw), lambda b: (b, 0, 0)),
                   per(Wb, ckv), per(Wb, ckv)],
        out_shape=[jax.ShapeDtypeStruct((nb, 1, cw), F32),
                   jax.ShapeDtypeStruct((nb, Wb, ckv), F32),
                   jax.ShapeDtypeStruct((nb, Wb, ckv), F32)],
        compiler_params=_params("parallel"),
        name="nsa_sample_window",
    )(q8, ocmp, oslc, proj, proj, proj, knorm, wk, wv)
    return o.reshape(nb, cw), wko, wvo


def _pad_cols(w, mult):
    pad = (-w.shape[-1]) % mult
    return jnp.pad(w, ((0, 0),) * (w.ndim - 1) + ((0, pad),)) if pad else w


def kernel(x_prompt, x_sample, state_hgrn, cache_dil_k, cache_dil_v, cache_cmp_k, cache_cmp_v, cache_slc_k, cache_slc_v, cache_win_k, cache_win_v, page_table, norm_mix, norm_mlp, w_in_even, w_out_even, hgrn_lb_logits, hgrn_out_norm, dil_q_norm, dil_k_norm, w_in_odd, w_out_odd, nsa_q_norm, nsa_k_norm, nsa_pe_k, nsa_pe_v, nsa_phi_k1, nsa_phi_k2, nsa_phi_v1, nsa_phi_v2, w_mlp_up, w_mlp_down):
    n, T, D = x_prompt.shape
    nb = x_sample.shape[0]
    assert x_sample.shape[1] == 1
    depth = norm_mix.shape[0]
    a_heads = hgrn_lb_logits.shape[1] // LANES
    b_heads = cache_dil_k.shape[3]
    hk = cache_win_k.shape[3]
    c_heads = hk * C_GROUP
    past_len = page_table.shape[1] * cache_cmp_k.shape[2]
    a_w = a_heads * LANES
    TN = 896
    TM = 512
    TM_IN = 1024 if (n * T) % 1024 == 0 else TM

    lb_cum = jnp.cumsum(jax.nn.softmax(hgrn_lb_logits.astype(F32), axis=0), axis=0)
    lower_bounds = lb_cum - lb_cum[0:1]

    cq, ckv = c_heads * LANES, hk * LANES
    gate_col = cq + 6 * ckv
    gate_w = w_in_odd[:, :, gate_col:].reshape(-1, D, hk, C_GROUP, 3).transpose(0, 1, 4, 2, 3).reshape(-1, D, 3 * c_heads)
    w_in_odd_p = jnp.concatenate([w_in_odd[:, :, :gate_col], _pad_cols(gate_w, LANES)], axis=-1)
    w_in_odd_p = _pad_cols(w_in_odd_p, TN)
    w_in_even_b = w_in_even
    w_out_even_b = w_out_even.astype(BF16)
    w_out_odd_b = w_out_odd.astype(BF16)
    w_up_b = w_mlp_up.astype(BF16)
    w_down_b = w_mlp_down.astype(BF16)
    phi_k1 = nsa_phi_k1.reshape(-1, CMP_LEN, LANES, LANES).astype(BF16)
    phi_v1 = nsa_phi_v1.reshape(-1, CMP_LEN, LANES, LANES).astype(BF16)
    phi_k2 = nsa_phi_k2.astype(BF16)
    phi_v2 = nsa_phi_v2.astype(BF16)

    xp = x_prompt.reshape(n * T, D)
    xs = x_sample.reshape(nb, D)
    outs = {k: [] for k in ("hg_p", "hg_s", "dk_p", "dv_p", "dk_s", "dv_s", "ck_p", "cv_p", "sk_p", "sv_p",
                            "wk_p", "wv_p", "ck_s", "cv_s", "sk_s", "sv_s", "wk_s", "wv_s")}
    dil_bufs = None
    for layer in range(depth):
        li = layer // 2
        if layer % 2 == 0:
            lb, on = lower_bounds[li], hgrn_out_norm[li]
            qn, kn = dil_q_norm[li], dil_k_norm[li]
            pp = rms_matmul(xp, norm_mix[layer], w_in_even_b, li, TM_IN, TN)
            ps = rms_matmul(xs, norm_mix[layer], w_in_even_b, li, nb, TN)
            oa_p, st_p = hgrn_prompt(pp, lb, on, n, T, a_heads)
            ob_p, dk, dv = dil_prompt(pp, qn, kn, n, T, b_heads, 4 * a_w)
            oa_s, st_s = hgrn_sample(ps, lb, on, state_hgrn, li, a_heads)
            ob_s, *dil_bufs = dil_sample(ps, qn, kn, cache_dil_k, cache_dil_v, li, b_heads, 4 * a_w, prev=dil_bufs)
            xp = proj_residual([oa_p, ob_p], w_out_even_b, li, xp, TM)
            xs = proj_residual([oa_s, ob_s], w_out_even_b, li, xs, nb)
            outs["hg_p"].append(st_p); outs["hg_s"].append(st_s)
            outs["dk_p"].append(dk.reshape(n, -1, b_heads, LANES)); outs["dv_p"].append(dv.reshape(n, -1, b_heads, LANES))
        else:
            knorm = nsa_k_norm[li]
            cmp_w = (nsa_pe_k[li], nsa_pe_v[li], phi_k1[li], phi_k2[li], phi_v1[li], phi_v2[li])
            pp = rms_matmul(xp, norm_mix[layer], w_in_odd_p, li, TM_IN, TN)
            ps = rms_matmul(xs, norm_mix[layer], w_in_odd_p, li, nb, TN)
            kcmp, vcmp, ksb, ksf, vst, kwb, kwf, vwt = nsa_prep(pp, knorm, *cmp_w, n, T, hk)
            o_p = nsa_prompt(pp, nsa_q_norm[li], kcmp, vcmp, ksb, vst, kwb, vwt, n, T, hk, gate_col)
            kcs, vcs = cmp_pages(page_table, cache_cmp_k, cache_cmp_v, li, knorm, *cmp_w)
            q8, ocmp, sel, ksn = nsa_sample_select(ps, nsa_q_norm[li], knorm, *cmp_w, kcs, vcs, hk, past_len)
            oslc = nsa_sample_selected(sel, page_table, q8, ksn, ps, cq + 3 * ckv, cache_slc_k, cache_slc_v, li, hk)
            o_s, wks, wvs = nsa_sample_window(q8, ocmp, oslc, ps, cq + 4 * ckv, gate_col, knorm,
                                              cache_win_k, cache_win_v, li, hk)
            xp = proj_residual([o_p], w_out_odd_b, li, xp, TM)
            xs = proj_residual([o_s], w_out_odd_b, li, xs, nb)
            kvp = lambda k: pp[:, cq + k * ckv:cq + (k + 1) * ckv].reshape(n, T, hk, LANES)
            kvs = lambda k: ps[:, cq + k * ckv:cq + (k + 1) * ckv].reshape(nb, 1, hk, LANES)
            ww = kwf.shape[1]
            outs["ck_p"].append(kvp(0)); outs["cv_p"].append(kvp(1))
            outs["sk_p"].append(ksf.reshape(n, T, hk, LANES)); outs["sv_p"].append(kvp(3))
            outs["wk_p"].append(kwf.reshape(n, ww, hk, LANES)); outs["wv_p"].append(kvp(5)[:, T - ww:])
            outs["ck_s"].append(kvs(0)); outs["cv_s"].append(kvs(1))
            outs["sk_s"].append(ksn[:, :, 0, :].reshape(nb, 1, hk, LANES)); outs["sv_s"].append(kvs(3))
            outs["wk_s"].append(wks.reshape(nb, -1, hk, LANES)); outs["wv_s"].append(wvs.reshape(nb, -1, hk, LANES))
        xp = mlp_residual(xp, norm_mlp[layer], w_up_b, w_down_b, layer, TM_IN, 512)
        xs = mlp_residual(xs, norm_mlp[layer], w_up_b, w_down_b, layer, nb, 512)
    st = lambda k: jnp.stack(outs[k], axis=1)
    return (xp.reshape(n, T, D), xs.reshape(nb, 1, D),
            st("hg_p"), st("hg_s"), st("dk_p"), st("dv_p"),
            dil_bufs[0].reshape(cache_dil_k.shape), dil_bufs[1].reshape(cache_dil_v.shape),
            st("ck_p"), st("cv_p"), st("sk_p"), st("sv_p"), st("wk_p"), st("wv_p"),
            st("ck_s"), st("cv_s"), st("sk_s"), st("sv_s"), st("wk_s"), st("wv_s"))
```

```python
import functools

import numpy as np
import jax
import jax.numpy as jnp
from jax import lax
from jax.experimental import pallas as pl
from jax.experimental.pallas import tpu as pltpu

F32 = jnp.float32
BF16 = jnp.bfloat16
HIGHEST = lax.Precision.HIGHEST

HEAD_DIM = 128
LANES = 128
RMS_EPS = 1e-6
NEG_INF = -1e30
FORCE_SCORE = 1e6
SCALE = HEAD_DIM ** -0.5
HGRN_CHUNK = 64
HGRN_SUB = 8
DIL_PATTERNS = ((128, 1), (512, 4), (2048, 16))
DIL_MAX_WINDOW = 2048
CMP_LEN = 32
SLC_BLOCK = 64
SLC_TOPN = 16
WIN = 512
C_GROUP = 4
SLC_CHUNK = 512
VMEM_LIMIT = 56 * 1024 * 1024


def _params(*sem):
    return pltpu.CompilerParams(dimension_semantics=sem, vmem_limit_bytes=VMEM_LIMIT)


def _rms(x, w):
    return x * lax.rsqrt(jnp.mean(x * x, axis=-1, keepdims=True) + RMS_EPS) * w


def _sigmoid(x):
    return 1.0 / (1.0 + jnp.exp(-x))


def _dot_nt(a, b):
    return lax.dot_general(a, b, (((1,), (1,)), ((), ())), preferred_element_type=F32)


def _dot(a, b, precision=None):
    return jnp.dot(a, b, preferred_element_type=F32, precision=precision)


def _rms_mm_body(x_ref, g_ref, w_ref, o_ref, h_ref):
    @pl.when(pl.program_id(1) == 0)
    def _():
        h_ref[...] = _rms(x_ref[...], g_ref[...]).astype(BF16)

    o_ref[...] = _dot(h_ref[...], w_ref[...].astype(BF16))


def rms_matmul(x, g, w, li, tm, tn, n_out=None):
    M, D = x.shape
    N = w.shape[2] if n_out is None else n_out
    assert N % tn == 0
    return pl.pallas_call(
        _rms_mm_body,
        grid=(M // tm, N // tn),
        in_specs=[pl.BlockSpec((tm, D), lambda i, j: (i, 0)),
                  pl.BlockSpec((1, D), lambda i, j: (0, 0)),
                  pl.BlockSpec((None, D, tn), lambda i, j: (li, 0, j))],
        out_specs=pl.BlockSpec((tm, tn), lambda i, j: (i, j)),
        out_shape=jax.ShapeDtypeStruct((M, N), F32),
        scratch_shapes=[pltpu.VMEM((tm, D), BF16)],
        compiler_params=_params("parallel", "arbitrary"),
        name="rms_matmul",
    )(x, g.reshape(1, D), w)


def _proj_res_body(*refs, n_in):
    res_ref, o_ref = refs[2 * n_in], refs[2 * n_in + 1]
    acc = res_ref[...]
    for a_ref, w_ref in zip(refs[:n_in], refs[n_in:2 * n_in]):
        acc = acc + _dot(a_ref[...].astype(BF16), w_ref[...])
    o_ref[...] = acc


def proj_residual(lhs, w, li, res, tm):
    M, D = res.shape
    n = len(lhs)
    kp = w.shape[1] // n
    assert all(a.shape[1] == kp for a in lhs)
    ws = [w] * n
    in_specs = [pl.BlockSpec((tm, kp), lambda i: (i, 0)) for _ in lhs]
    in_specs += [pl.BlockSpec((None, kp, D), lambda i, k=k: (li, k, 0)) for k in range(n)]
    in_specs += [pl.BlockSpec((tm, D), lambda i: (i, 0))]
    return pl.pallas_call(
        functools.partial(_proj_res_body, n_in=n),
        grid=(M // tm,),
        in_specs=in_specs,
        out_specs=pl.BlockSpec((tm, D), lambda i: (i, 0)),
        out_shape=jax.ShapeDtypeStruct((M, D), F32),
        compiler_params=_params("parallel"),
        name="proj_residual",
    )(*lhs, *ws, res)


def _mlp_body(x_ref, g_ref, wu_ref, wd_ref, o_ref, h_ref):
    @pl.when(pl.program_id(1) == 0)
    def _():
        x = x_ref[...]
        h_ref[...] = _rms(x, g_ref[...]).astype(BF16)
        o_ref[...] = x

    u = jnp.maximum(_dot(h_ref[...], wu_ref[...]), 0.0)
    o_ref[...] += _dot((u * u).astype(BF16), wd_ref[...])


def mlp_residual(x, g, wu, wd, layer, tm, tf):
    M, D = x.shape
    Fd = wu.shape[2]
    return pl.pallas_call(
        _mlp_body,
        grid=(M // tm, Fd // tf),
        in_specs=[pl.BlockSpec((tm, D), lambda i, j: (i, 0)),
                  pl.BlockSpec((1, D), lambda i, j: (0, 0)),
                  pl.BlockSpec((None, D, tf), lambda i, j: (layer, 0, j)),
                  pl.BlockSpec((None, tf, D), lambda i, j: (layer, j, 0))],
        out_specs=pl.BlockSpec((tm, D), lambda i, j: (i, 0)),
        out_shape=jax.ShapeDtypeStruct((M, D), F32),
        scratch_shapes=[pltpu.VMEM((tm, D), BF16)],
        compiler_params=_params("parallel", "arbitrary"),
        name="mlp_residual",
    )(x, g.reshape(1, D), wu, wd)


def _hgrn_gates(z, lb):
    log_sig = jnp.minimum(z, 0.0) - jnp.log1p(jnp.exp(-jnp.abs(z)))
    a = jnp.log(lb)
    b = jnp.log1p(-lb) + log_sig
    log_f = jnp.maximum(a, b) + jnp.log1p(jnp.exp(-jnp.abs(a - b)))
    series = -(log_f + 0.5 * log_f * log_f + log_f * log_f * log_f * (1.0 / 6.0))
    k = jnp.where(log_f > -0.01, series, 1.0 - jnp.exp(log_f))
    return log_f, k


def _hgrn_out(o, on, g_raw):
    return _rms(o, on) * (g_raw * _sigmoid(g_raw))


def _split3(x):
    hi = x.astype(BF16)
    r1 = x - hi.astype(F32)
    mid = r1.astype(BF16)
    lo = (r1 - mid.astype(F32)).astype(BF16)
    return jnp.concatenate([hi, mid, lo], axis=1)


def _hgrn_body(q_ref, f_ref, i_ref, g_ref, lb_ref, on_ref, o_ref, s_ref, st_ref, *, tb, nh):
    C, SC = HGRN_CHUNK, HGRN_SUB
    t = pl.program_id(2)
    heads = range(nh)

    @pl.when(t == 0)
    def _():
        st_ref[...] = jnp.zeros_like(st_ref)

    on = on_ref[...]
    r_i = lax.broadcasted_iota(jnp.int32, (C, C), 0)
    c_i = lax.broadcasted_iota(jnp.int32, (C, C), 1)
    tril = (r_i >= c_i).astype(BF16)
    row = lax.broadcasted_iota(jnp.int32, (C, LANES), 0)
    levels = []
    bs = C // 2
    while bs >= SC:
        levels.append((bs, (row // bs) % 2 == 1, ((r_i // bs) % 2 == 1) & (c_i // bs == r_i // bs - 1)))
        bs //= 2
    lane_c = lax.broadcasted_iota(jnp.int32, (SC, C), 1)
    sub_c = lax.broadcasted_iota(jnp.int32, (SC, C), 0)

    def chunk(c, carry):
        rows = pl.ds(pl.multiple_of(c * C, C), C)
        sl = [slice(h * LANES, (h + 1) * LANES) for h in heads]
        qr = [q_ref[rows, sl[h]] for h in heads]
        q = [x * _sigmoid(x) for x in qr]
        gates = [_hgrn_gates(f_ref[rows, sl[h]], lb_ref[h]) for h in heads]
        log_f, kk = [g[0] for g in gates], [g[1] for g in gates]
        v = [i_ref[rows, sl[h]] for h in heads]
        vb = [x.astype(BF16) for x in v]
        g3 = [_dot(tril, _split3(log_f[h])) for h in heads]
        G = [x[:, 0:LANES] + x[:, LANES:2 * LANES] + x[:, 2 * LANES:3 * LANES] for x in g3]
        st = [st_ref[h] for h in heads]
        inter = [_dot_nt((q[h] * jnp.exp(G[h])).astype(BF16), st[h].astype(BF16)) for h in heads]
        a_off = [jnp.zeros((C, C), F32) for _ in heads]
        for bs, odd, blk in levels:
            refs = [jnp.concatenate([jnp.broadcast_to(G[h][p + bs - 1:p + bs], (2 * bs, LANES))
                                     for p in range(0, C, 2 * bs)], axis=0) for h in heads]
            d = [G[h] - refs[h] for h in heads]
            qp = [(q[h] * jnp.exp(jnp.where(odd, d[h], NEG_INF))).astype(BF16) for h in heads]
            kp = [(kk[h] * jnp.exp(jnp.where(odd, NEG_INF, -d[h]))).astype(BF16) for h in heads]
            a_off = [a_off[h] + jnp.where(blk, _dot_nt(qp[h], kp[h]), 0.0) for h in heads]
        a_rows = [[] for _ in heads]
        for I in range(C // SC):
            lo = I * SC
            for h in heads:
                GI, qI = G[h][lo:lo + SC], q[h][lo:lo + SC]
                dg = jnp.zeros((SC, C), F32)
                for j in range(SC):
                    e = jnp.exp(jnp.minimum(GI - GI[j:j + 1], 0.0))
                    colv = jnp.sum(qI * e * kk[h][lo + j:lo + j + 1], axis=1, keepdims=True)
                    dg = jnp.where(lane_c == lo + j, colv, dg)
                a_rows[h].append(jnp.where(lane_c <= lo + sub_c, dg, 0.0))
        a = [(a_off[h] + jnp.concatenate(a_rows[h], axis=0)).astype(BF16) for h in heads]
        o = [inter[h] + _dot(a[h], vb[h]) for h in heads]
        Gl = [G[h][C - 1:C] for h in heads]
        kd = [(kk[h] * jnp.exp(Gl[h] - G[h])).astype(BF16) for h in heads]
        upd = [_dot(v[h].T.astype(BF16), kd[h]) for h in heads]
        for h in heads:
            st_ref[h] = jnp.exp(Gl[h]) * st[h] + upd[h]
            o_ref[rows, sl[h]] = _hgrn_out(o[h], on, g_ref[rows, sl[h]]).astype(o_ref.dtype)
        return carry

    lax.fori_loop(0, tb // C, chunk, 0)

    @pl.when(t == pl.num_programs(2) - 1)
    def _():
        for h in heads:
            s_ref[0, h] = st_ref[h].T


def hgrn_prompt(proj, lb, on, n, T, heads, tb=256, nh=4):
    nt = T // tb
    hg = heads // nh
    col = lambda k: pl.BlockSpec((tb, nh * LANES), lambda b, h, t, k=k: (b * nt + t, k * hg + h))
    return pl.pallas_call(
        functools.partial(_hgrn_body, tb=tb, nh=nh),
        grid=(n, hg, nt),
        in_specs=[col(0), col(1), col(2), col(3),
                  pl.BlockSpec((nh, 1, LANES), lambda b, h, t: (h, 0, 0)),
                  pl.BlockSpec((1, LANES), lambda b, h, t: (0, 0))],
        out_specs=[pl.BlockSpec((tb, nh * LANES), lambda b, h, t: (b * nt + t, h)),
                   pl.BlockSpec((1, nh, LANES, LANES), lambda b, h, t: (b, h, 0, 0))],
        out_shape=[jax.ShapeDtypeStruct((n * T, heads * LANES), BF16),
                   jax.ShapeDtypeStruct((n, heads, LANES, LANES), F32)],
        scratch_shapes=[pltpu.VMEM((nh, LANES, LANES), F32)],
        compiler_params=_params("parallel", "parallel", "arbitrary"),
        name="hgrn_prompt",
    )(proj, proj, proj, proj, lb.reshape(heads, 1, LANES), on.reshape(1, LANES))


def _col(eye, row):
    return jnp.sum(eye * row, axis=1, keepdims=True)


def _hgrn_s_body(q_ref, f_ref, i_ref, g_ref, lb_ref, on_ref, s_ref, o_ref, so_ref, *, nb):
    qr = q_ref[...]
    q = qr * _sigmoid(qr)
    log_f, kk = _hgrn_gates(f_ref[...], lb_ref[0])
    v = i_ref[...]
    f = jnp.exp(log_f)
    eye = (lax.broadcasted_iota(jnp.int32, (LANES, LANES), 0)
           == lax.broadcasted_iota(jnp.int32, (LANES, LANES), 1)).astype(F32)
    qf = (q * f).astype(BF16)
    a = jnp.sum(q * kk, axis=1, keepdims=True)
    rows = []
    for b in range(nb):
        S = s_ref[b, 0, 0]
        so_ref[b, 0] = _col(eye, f[b:b + 1]) * S + _col(eye, kk[b:b + 1]) * v[b:b + 1]
        rows.append(_dot(qf, S.astype(BF16))[b:b + 1])
    o = jnp.concatenate(rows, axis=0) + a * v
    o_ref[...] = _hgrn_out(o, on_ref[...], g_ref[...])


def hgrn_sample(proj, lb, on, state, li, heads):
    nb = proj.shape[0]
    col = lambda k: pl.BlockSpec((nb, LANES), lambda h, k=k: (0, k * heads + h))
    return pl.pallas_call(
        functools.partial(_hgrn_s_body, nb=nb),
        grid=(heads,),
        in_specs=[col(0), col(1), col(2), col(3),
                  pl.BlockSpec((1, 1, LANES), lambda h: (h, 0, 0)),
                  pl.BlockSpec((1, LANES), lambda h: (0, 0)),
                  pl.BlockSpec((nb, 1, 1, LANES, LANES), lambda h: (0, li, h, 0, 0))],
        out_specs=[pl.BlockSpec((nb, LANES), lambda h: (0, h)),
                   pl.BlockSpec((nb, 1, LANES, LANES), lambda h: (0, h, 0, 0))],
        out_shape=[jax.ShapeDtypeStruct((nb, heads * LANES), F32),
                   jax.ShapeDtypeStruct((nb, heads, LANES, LANES), F32)],
        compiler_params=_params("parallel"),
        name="hgrn_sample",
    )(proj, proj, proj, proj, lb.reshape(heads, 1, LANES), on.reshape(1, LANES), state)


def _dil_multiplicity(delta):
    c = np.zeros(delta.shape, np.float32)
    for window, dil in DIL_PATTERNS:
        c += ((delta >= 0) & (delta <= window) & (delta % dil == 0)).astype(np.float32)
    return c


def _dil_body(q_ref, k_ref, v_ref, qn_ref, kn_ref, c_ref, o_ref, ko_ref, vo_ref, kn_s, vb_s,
              *, T, tq, W, pad, nh):
    qi = pl.program_id(2)
    span = pad + tq
    sl = [slice(h * LANES, (h + 1) * LANES) for h in range(nh)]

    @pl.when(qi == 0)
    def _():
        kw = kn_ref[...]
        kn_s[:, 0:pad, :] = jnp.zeros((nh, pad, LANES), BF16)
        vb_s[:, 0:pad, :] = jnp.zeros((nh, pad, LANES), BF16)

        def norm(c, carry):
            rows = pl.ds(pl.multiple_of(c * 512, 512), 512)
            dst = pl.ds(pl.multiple_of(pad + c * 512, 512), 512)
            for h in range(nh):
                kn_s[h, dst, :] = _rms(k_ref[rows, sl[h]], kw).astype(BF16)
                vb_s[h, dst, :] = v_ref[rows, sl[h]].astype(BF16)
            return carry

        lax.fori_loop(0, T // 512, norm, 0)
        for h in range(nh):
            ko_ref[0, :, sl[h]] = _rms(k_ref[T - W:, sl[h]], kw)
        vo_ref[0] = v_ref[T - W:, :]

    rows = pl.ds(pl.multiple_of(qi * tq, tq), span)
    q = [(_rms(q_ref[:, sl[h]], qn_ref[...]) * SCALE).astype(BF16) for h in range(nh)]
    sc = [_dot_nt(q[h], kn_s[h, rows, :]) for h in range(nh)]
    c = c_ref[...]
    valid = (c > 0.0) & (lax.broadcasted_iota(jnp.int32, (tq, span), 1) >= pad - qi * tq)
    sc = [jnp.where(valid, x, NEG_INF) for x in sc]
    p = [c * jnp.exp(x - jnp.max(x, axis=1, keepdims=True)) for x in sc]
    o = [_dot(p[h].astype(BF16), vb_s[h, rows, :]) for h in range(nh)]
    for h in range(nh):
        o_ref[:, sl[h]] = (o[h] / jnp.sum(p[h], axis=1, keepdims=True)).astype(o_ref.dtype)


def dil_prompt(proj, qn, kn, n, T, heads, col0, tq=256, nh=2):
    W = min(DIL_MAX_WINDOW, T)
    pad = DIL_MAX_WINDOW
    nt = T // tq
    ctab = jnp.asarray(_dil_multiplicity(np.arange(tq)[:, None] + pad - np.arange(pad + tq)[None, :]))
    hw = nh * LANES
    cb, hg = col0 // hw, heads // nh
    return pl.pallas_call(
        functools.partial(_dil_body, T=T, tq=tq, W=W, pad=pad, nh=nh),
        grid=(n, hg, nt),
        in_specs=[pl.BlockSpec((tq, hw), lambda b, h, t: (b * nt + t, cb + h)),
                  pl.BlockSpec((T, hw), lambda b, h, t: (b, cb + hg + h)),
                  pl.BlockSpec((T, hw), lambda b, h, t: (b, cb + 2 * hg + h)),
                  pl.BlockSpec((1, LANES), lambda b, h, t: (0, 0)),
                  pl.BlockSpec((1, LANES), lambda b, h, t: (0, 0)),
                  pl.BlockSpec((tq, pad + tq), lambda b, h, t: (0, 0))],
        out_specs=[pl.BlockSpec((tq, hw), lambda b, h, t: (b * nt + t, h)),
                   pl.BlockSpec((1, W, hw), lambda b, h, t: (b, 0, h)),
                   pl.BlockSpec((1, W, hw), lambda b, h, t: (b, 0, h))],
        out_shape=[jax.ShapeDtypeStruct((n * T, heads * LANES), BF16),
                   jax.ShapeDtypeStruct((n, W, heads * LANES), F32),
                   jax.ShapeDtypeStruct((n, W, heads * LANES), F32)],
        scratch_shapes=[pltpu.VMEM((nh, T + pad, LANES), BF16), pltpu.VMEM((nh, T + pad, LANES), BF16)],
        compiler_params=_params("parallel", "parallel", "arbitrary"),
        name="dil_prompt",
    )(proj, proj, proj, qn.reshape(1, LANES), kn.reshape(1, LANES), ctab)


def _shift_in(buf, new_row):
    n = buf.shape[0]
    rolled = pltpu.roll(buf, n - 1, 0)
    return jnp.where(lax.broadcasted_iota(jnp.int32, buf.shape, 0) == n - 1, new_row, rolled)


def _dil_s_body(q_ref, k_ref, v_ref, qn_ref, kn_ref, c_ref, *rest, heads, c_new, li, n_l, has_prev):
    n_src = 1 if has_prev else n_l
    ck_refs, cv_refs = rest[:n_src], rest[n_src:2 * n_src]
    o_ref, ko_hbm, vo_hbm, m_s, l_s, acc_s, new_s, sem = rest[2 * n_src + (2 if has_prev else 0):]
    ck_ref, cv_ref = (ck_refs[0], cv_refs[0]) if has_prev else (ck_refs[li], cv_refs[li])
    H = heads
    b, c = pl.program_id(0), pl.program_id(1)
    nrows, cr = ko_hbm.shape[2], ck_ref.shape[2]
    sub = lax.broadcasted_iota(jnp.int32, (H, LANES), 0)

    def heads_on_rows(ref):
        out = jnp.zeros((H, LANES), F32)
        for h in range(H):
            out = jnp.where(sub == h, ref[:, h * LANES:(h + 1) * LANES], out)
        return out

    q8 = _rms(heads_on_rows(q_ref), qn_ref[...]) * SCALE
    kn8 = _rms(heads_on_rows(k_ref), kn_ref[...])
    v8 = heads_on_rows(v_ref)

    def chunk_copies(first):
        if first:
            src, dst = pl.ds(H, cr - H), pl.ds(0, cr - H)
        else:
            src, dst = pl.ds(0, cr), pl.ds(pl.multiple_of(c * cr - H, H), cr)
        out = []
        for l in range(0 if has_prev else n_l):
            out.append(pltpu.make_async_copy(ck_refs[l].at[0, 0, src], ko_hbm.at[b, l, dst], sem.at[0]))
            out.append(pltpu.make_async_copy(cv_refs[l].at[0, 0, src], vo_hbm.at[b, l, dst], sem.at[1]))
        return out

    def tail_copies():
        tail = pl.ds(nrows - H, H)
        out = [pltpu.make_async_copy(new_s.at[0], ko_hbm.at[b, li, tail], sem.at[2]),
               pltpu.make_async_copy(new_s.at[1], vo_hbm.at[b, li, tail], sem.at[3])]
        for l in range(0 if has_prev else n_l):
            if l != li:
                out.append(pltpu.make_async_copy(new_s.at[2], ko_hbm.at[b, l, tail], sem.at[2]))
                out.append(pltpu.make_async_copy(new_s.at[2], vo_hbm.at[b, l, tail], sem.at[3]))
        return out

    @pl.when(c == 0)
    def _():
        m_s[...] = jnp.full_like(m_s, NEG_INF)
        l_s[...] = jnp.zeros_like(l_s)
        acc_s[...] = jnp.zeros_like(acc_s)
        for cp in chunk_copies(True):
            cp.start()

    @pl.when(c > 0)
    def _():
        for cp in chunk_copies(False):
            cp.start()

    kc, vc = ck_ref[0, 0], cv_ref[0, 0]
    s = _dot_nt(q8.astype(BF16), kc.astype(BF16))
    own = lax.broadcasted_iota(jnp.int32, s.shape, 1) % H == lax.broadcasted_iota(jnp.int32, s.shape, 0)
    cm = jnp.where(own, c_ref[0], 0.0)
    s = jnp.where(cm > 0.0, s, NEG_INF)
    m_old = m_s[...]
    m_new = jnp.maximum(m_old, jnp.max(s, axis=1, keepdims=True))
    alpha = jnp.exp(m_old - m_new)
    p = cm * jnp.exp(s - m_new[:, 0:1])
    l_s[...] = alpha * l_s[...] + jnp.sum(p, axis=1, keepdims=True)
    acc_s[...] = alpha * acc_s[...] + _dot(p.astype(BF16), vc.astype(BF16))
    m_s[...] = m_new

    @pl.when(c == 0)
    def _():
        for cp in chunk_copies(True):
            cp.wait()

    @pl.when(c > 0)
    def _():
        for cp in chunk_copies(False):
            cp.wait()

    @pl.when(c == pl.num_programs(1) - 1)
    def _():
        s_new = jnp.sum(q8 * kn8, axis=1, keepdims=True)
        m_f = jnp.maximum(m_s[...], s_new)
        a = jnp.exp(m_s[...] - m_f)
        p_new = c_new * jnp.exp(s_new - m_f)
        o = (a * acc_s[...] + p_new * v8) / (a * l_s[...] + p_new)
        for h in range(H):
            o_ref[:, h * LANES:(h + 1) * LANES] = o[h:h + 1]
        new_s[0] = kn8
        new_s[1] = v8
        new_s[2] = jnp.zeros((H, LANES), F32)
        for cp in tail_copies():
            cp.start()
        for cp in tail_copies():
            cp.wait()


def dil_sample(proj, qn, kn, cache_k, cache_v, li, heads, col0, prev=None, chunk=512):
    nb, n_l, Wb = cache_k.shape[0], cache_k.shape[1], cache_k.shape[2]
    H = heads
    ck = cache_k.reshape(nb, n_l, Wb * H, LANES)
    cv = cache_v.reshape(nb, n_l, Wb * H, LANES)
    nch = Wb // chunk
    c_buf = jnp.asarray(np.repeat(_dil_multiplicity(Wb - np.arange(Wb)), H).reshape(nch, 1, chunk * H))
    c_new = float(_dil_multiplicity(np.zeros((1,), np.int64))[0])
    hw = H * LANES
    proj = proj.reshape(nb, 1, -1)
    col = lambda k: pl.BlockSpec((None, 1, hw), lambda b, c, k=k: (b, 0, col0 // hw + k))
    vec = pl.BlockSpec((1, LANES), lambda b, c: (0, 0))
    cache = lambda l: pl.BlockSpec((1, 1, chunk * H, LANES), lambda b, c, l=l: (b, l, c, 0))
    anyspec = pl.BlockSpec(memory_space=pl.ANY)
    full = jax.ShapeDtypeStruct((nb, n_l, Wb * H, LANES), F32)
    layers = [li] if prev is not None else list(range(n_l))
    args = [proj, proj, proj, qn.reshape(1, LANES), kn.reshape(1, LANES), c_buf]
    args += [ck] * len(layers) + [cv] * len(layers)
    in_specs = [col(0), col(1), col(2), vec, vec, pl.BlockSpec((1, 1, chunk * H), lambda b, c: (c, 0, 0))]
    in_specs += [cache(l) for l in layers] * 2
    aliases = {}
    if prev is not None:
        aliases = {len(args): 1, len(args) + 1: 2}
        args += list(prev)
        in_specs += [anyspec, anyspec]
    o, ok, ov = pl.pallas_call(
        functools.partial(_dil_s_body, heads=H, c_new=c_new, li=li, n_l=n_l, has_prev=prev is not None),
        grid=(nb, nch),
        in_specs=in_specs,
        out_specs=[pl.BlockSpec((None, 1, hw), lambda b, c: (b, 0, 0)), anyspec, anyspec],
        out_shape=[jax.ShapeDtypeStruct((nb, 1, hw), F32), full, full],
        scratch_shapes=[pltpu.VMEM((H, LANES), F32), pltpu.VMEM((H, LANES), F32), pltpu.VMEM((H, LANES), F32),
                        pltpu.VMEM((3, H, LANES), F32), pltpu.SemaphoreType.DMA((4,))],
        input_output_aliases=aliases,
        compiler_params=_params("arbitrary", "arbitrary"),
        name="dil_sample",
    )(*args)
    return o.reshape(nb, hw), ok, ov


def _gelu(x):
    return 0.5 * x * (1.0 + jnp.tanh(0.7978845608028654 * (x + 0.044715 * x * x * x)))


def _compress_rows(load_j, pe_ref, w1_ref, w2_ref, rows):
    acc = jnp.zeros((rows, LANES), F32)
    for j in range(CMP_LEN):
        acc = acc + _dot((load_j(j) + pe_ref[j:j + 1, :]).astype(BF16), w1_ref[j])
    return _dot(_gelu(acc).astype(BF16), w2_ref[...])


def _nsa_prep_body(kc_ref, vc_ref, ks_ref, vs_ref, kw_ref, vw_ref, knorm_ref, pek_ref, pev_ref,
                   w1k_ref, w2k_ref, w1v_ref, w2v_ref,
                   kcmp_ref, vcmp_ref, ksb_ref, ksf_ref, vst_ref, kwb_ref, kwf_ref, vwt_ref, *, T):
    nblk = T // CMP_LEN
    kcmp = _compress_rows(lambda j: kc_ref[pl.ds(j, nblk, stride=CMP_LEN), :], pek_ref, w1k_ref, w2k_ref, nblk)
    kcmp_ref[0, 0] = _rms(kcmp, knorm_ref[0:1, :])
    vcmp_ref[0, 0] = _compress_rows(lambda j: vc_ref[pl.ds(j, nblk, stride=CMP_LEN), :], pev_ref, w1v_ref, w2v_ref, nblk)
    ks_w, kw_w = knorm_ref[1:2, :], knorm_ref[2:3, :]

    kwb_ref[0, 0, 0:WIN, :] = jnp.zeros((WIN, LANES), BF16)
    for i in range(WIN // LANES):
        vwt_ref[0, 0, i] = jnp.zeros((LANES, LANES), BF16)

    def tile(c, carry):
        rows = pl.ds(pl.multiple_of(c * LANES, LANES), LANES)
        ksn = _rms(ks_ref[rows, :], ks_w)
        ksf_ref[0, rows, :] = ksn
        ksb_ref[0, 0, rows, 0:LANES] = ksn.astype(BF16)
        blk_of_row = c * (LANES // SLC_BLOCK) + lax.broadcasted_iota(jnp.int32, (LANES, LANES), 0) // SLC_BLOCK
        ksb_ref[0, 0, rows, LANES:2 * LANES] = (lax.broadcasted_iota(jnp.int32, (LANES, LANES), 1) == blk_of_row).astype(BF16)
        kwb_ref[0, 0, pl.ds(pl.multiple_of(WIN + c * LANES, LANES), LANES), :] = _rms(kw_ref[rows, :], kw_w).astype(BF16)
        vwt_ref[0, 0, WIN // LANES + c] = vw_ref[rows, :].T.astype(BF16)
        return carry

    lax.fori_loop(0, T // LANES, tile, 0)

    def chunk(c, carry):
        rows = pl.ds(pl.multiple_of(c * SLC_CHUNK, SLC_CHUNK), SLC_CHUNK)
        vst_ref[0, 0, c] = vs_ref[rows, :].T.astype(BF16)
        return carry

    lax.fori_loop(0, T // SLC_CHUNK, chunk, 0)
    ww = min(WIN, T)
    kwf_ref[0] = _rms(kw_ref[T - ww:, :], kw_w)


def nsa_prep(proj, knorm, pe_k, pe_v, w1k, w2k, w1v, w2v, n, T, hk):
    cb = (hk * C_GROUP * HEAD_DIM) // LANES
    nblk = T // CMP_LEN
    nt = T // LANES
    ww = min(WIN, T)
    col = lambda k: pl.BlockSpec((T, LANES), lambda b, h, k=k: (b, cb + k * hk + h))
    full = lambda a: pl.BlockSpec(a.shape, lambda b, h: (0,) * a.ndim)
    per = lambda *s: pl.BlockSpec((1, 1) + s, lambda b, h: (b, h) + (0,) * len(s))
    return pl.pallas_call(
        functools.partial(_nsa_prep_body, T=T),
        grid=(n, hk),
        in_specs=[col(0), col(1), col(2), col(3), col(4), col(5),
                  full(knorm), full(pe_k), full(pe_v), full(w1k), full(w2k), full(w1v), full(w2v)],
        out_specs=[per(nblk, LANES), per(nblk, LANES), per(T, 2 * LANES),
                   pl.BlockSpec((1, T, LANES), lambda b, h: (b, 0, h)),
                   per(T // SLC_CHUNK, LANES, SLC_CHUNK), per(T + WIN, LANES),
                   pl.BlockSpec((1, ww, LANES), lambda b, h: (b, 0, h)),
                   per(nt + WIN // LANES, LANES, LANES)],
        out_shape=[jax.ShapeDtypeStruct((n, hk, nblk, LANES), F32),
                   jax.ShapeDtypeStruct((n, hk, nblk, LANES), F32),
                   jax.ShapeDtypeStruct((n, hk, T, 2 * LANES), BF16),
                   jax.ShapeDtypeStruct((n, T, hk * LANES), F32),
                   jax.ShapeDtypeStruct((n, hk, T // SLC_CHUNK, LANES, SLC_CHUNK), BF16),
                   jax.ShapeDtypeStruct((n, hk, T + WIN, LANES), BF16),
                   jax.ShapeDtypeStruct((n, ww, hk * LANES), F32),
                   jax.ShapeDtypeStruct((n, hk, nt + WIN // LANES, LANES, LANES), BF16)],
        compiler_params=_params("parallel", "parallel"),
        name="nsa_prep",
    )(proj, proj, proj, proj, proj, proj, knorm, pe_k, pe_v, w1k, w2k, w1v, w2v)


def _tile4(x):
    return jnp.concatenate([x] * C_GROUP, axis=1)


def _nsa_body(q_ref, gate_ref, qn_ref, kcmp_ref, vcmp_ref, ks_ref, vst_ref, kw_ref, vwt_ref, o_ref,
              vct_s, pb_s, gt_s, *, tq, nblk, nslc, hk_n, hpb):
    G = C_GROUP
    HB = range(hpb)
    gw = G * LANES
    hk0 = pl.program_id(1) * hpb
    qi = pl.program_id(2)
    t0 = qi * tq

    @pl.when(qi == 0)
    def _():
        for h in HB:
            vct_s[h] = vcmp_ref[0, h].T.astype(BF16)

    qw = qn_ref[...]
    q4 = [jnp.concatenate([(_rms(q_ref[:, h * gw + g * LANES:h * gw + (g + 1) * LANES], qw) * SCALE).astype(BF16)
                           for g in range(G)], axis=0) for h in HB]

    st = [_dot_nt(kcmp_ref[0, h].astype(BF16), q4[h]) for h in HB]
    blk = lax.broadcasted_iota(jnp.int32, (nblk, G * tq), 0)
    tpos = t0 + (lax.broadcasted_iota(jnp.int32, (nblk, G * tq), 1) & (tq - 1))
    valid = (blk + 1) * CMP_LEN - 1 <= tpos
    st = [jnp.where(valid, x, NEG_INF) for x in st]
    p = [jnp.where(valid, jnp.exp(x - jnp.max(x, axis=0, keepdims=True)), 0.0) for x in st]
    p = [x / jnp.maximum(jnp.sum(x, axis=0, keepdims=True), 1.0) for x in p]
    o_cmp = [_dot(vct_s[h], p[h].astype(BF16)) for h in HB]
    ratio = SLC_BLOCK // CMP_LEN
    imp = []
    for h in HB:
        pb = p[h][:, 0:tq]
        for g in range(1, G):
            pb = pb + p[h][:, g * tq:(g + 1) * tq]
        parts = []
        for i in range(tq // LANES):
            pb_s[h, i] = pb[:, i * LANES:(i + 1) * LANES]
            part = pb_s[h, i, pl.ds(0, nslc, stride=ratio), :]
            for r in range(1, ratio):
                part = part + pb_s[h, i, pl.ds(r, nslc, stride=ratio), :]
            parts.append(part)
        imp.append(jnp.concatenate(parts, axis=1))

    jb = lax.broadcasted_iota(jnp.int32, (nslc, tq), 0)
    tp = t0 + lax.broadcasted_iota(jnp.int32, (nslc, tq), 1)
    cur = tp // SLC_BLOCK
    forced = (jb == 0) | (jb == cur) | (jb == cur - 1)
    in_past = jb * SLC_BLOCK <= tp
    score = [jnp.where(in_past, jnp.where(forced, FORCE_SCORE, x), -FORCE_SCORE) for x in imp]
    rank = [jnp.zeros((nslc, tq), F32) for _ in HB]
    for jp in range(nslc):
        later = jb > jp
        for h in HB:
            row = score[h][jp:jp + 1, :]
            rank[h] = rank[h] + ((row > score[h]) | ((row == score[h]) & later)).astype(F32)
    q_aug = []
    for h in HB:
        bias = jnp.where(rank[h] < float(min(SLC_TOPN, nslc)), 0.0, NEG_INF)
        bias = jnp.concatenate([bias, jnp.zeros((LANES - nslc, tq), F32)], axis=0).T.astype(BF16)
        q_aug.append(jnp.concatenate([q4[h], jnp.concatenate([bias] * G, axis=0)], axis=1))

    KC = SLC_CHUNK

    def slc_step(kc, carry, causal):
        m, l, acc = carry
        rows = pl.ds(pl.multiple_of(kc * KC, KC), KC)
        s = [_dot_nt(ks_ref[0, h, rows, :], q_aug[h]) for h in HB]
        if causal:
            kpos = kc * KC + lax.broadcasted_iota(jnp.int32, (KC, tq), 0)
            hide = _tile4(jnp.where(kpos <= t0 + lax.broadcasted_iota(jnp.int32, (KC, tq), 1), 0.0, NEG_INF))
            s = [x + hide for x in s]
        m_new = [jnp.maximum(m[h], jnp.max(s[h], axis=0, keepdims=True)) for h in HB]
        alpha = [jnp.exp(m[h] - m_new[h]) for h in HB]
        pp = [jnp.exp(s[h] - m_new[h]) for h in HB]
        l = [alpha[h] * l[h] + jnp.sum(pp[h], axis=0, keepdims=True) for h in HB]
        pv = [_dot(vst_ref[0, h, kc], pp[h].astype(BF16)) for h in HB]
        acc = [alpha[h] * acc[h] + pv[h] for h in HB]
        return m_new, l, acc

    init = ([jnp.full((1, G * tq), NEG_INF, F32) for _ in HB], [jnp.zeros((1, G * tq), F32) for _ in HB],
            [jnp.zeros((LANES, G * tq), F32) for _ in HB])
    last = (t0 + tq - 1) // KC
    carry = lax.fori_loop(0, last, functools.partial(slc_step, causal=False), init)
    _, l_s, acc_s = slc_step(last, carry, causal=True)
    o_slc = [acc_s[h] / l_s[h] for h in HB]

    wspan = WIN + tq
    wsub = lax.broadcasted_iota(jnp.int32, (wspan, tq), 0)
    dist = lax.broadcasted_iota(jnp.int32, (wspan, tq), 1) + WIN - wsub
    wbias = _tile4(jnp.where((dist >= 0) & (dist <= WIN) & (wsub >= WIN - t0), 0.0, NEG_INF))
    wrows = pl.ds(pl.multiple_of(t0, tq), wspan)
    sw = [_dot_nt(kw_ref[0, h, wrows, :], q4[h]) + wbias for h in HB]
    pw = [jnp.exp(x - jnp.max(x, axis=0, keepdims=True)) for x in sw]
    vw_t = [jnp.concatenate([vwt_ref[0, h, qi * (tq // LANES) + i] for i in range(wspan // LANES)], axis=1)
            for h in HB]
    o_win = [_dot(vw_t[h], pw[h].astype(BF16)) / jnp.sum(pw[h], axis=0, keepdims=True) for h in HB]

    gt_s[...] = _sigmoid(gate_ref[...]).T
    nh = G * hk_n
    for h in HB:
        for g in range(G):
            sl = slice(g * tq, (g + 1) * tq)
            head = (hk0 + h) * G + g
            g0 = gt_s[pl.ds(head, 1), :]
            g1 = gt_s[pl.ds(nh + head, 1), :]
            g2 = gt_s[pl.ds(2 * nh + head, 1), :]
            o = g0 * o_cmp[h][:, sl] + g1 * o_slc[h][:, sl] + g2 * o_win[h][:, sl]
            o_ref[:, h * gw + g * LANES:h * gw + (g + 1) * LANES] = o.T.astype(o_ref.dtype)


def nsa_prompt(proj, gate, qn, kcmp, vcmp, ksb, vst, kwb, vwt, n, T, hk, tq=256, hpb=2):
    nblk, nslc, nt = T // CMP_LEN, T // SLC_BLOCK, T // tq
    gw = hpb * C_GROUP * LANES
    per = lambda *s: pl.BlockSpec((1, hpb) + s, lambda b, h, t: (b, h) + (0,) * len(s))
    return pl.pallas_call(
        functools.partial(_nsa_body, tq=tq, nblk=nblk, nslc=nslc, hk_n=hk, hpb=hpb),
        grid=(n, hk // hpb, nt),
        in_specs=[pl.BlockSpec((tq, gw), lambda b, h, t: (b * nt + t, h)),
                  pl.BlockSpec((tq, LANES), lambda b, h, t: (b * nt + t, 0)),
                  pl.BlockSpec((1, LANES), lambda b, h, t: (0, 0)),
                  per(nblk, LANES), per(nblk, LANES), per(T, 2 * LANES), per(T // SLC_CHUNK, LANES, SLC_CHUNK),
                  per(T + WIN, LANES), per((T + WIN) // LANES, LANES, LANES)],
        out_specs=pl.BlockSpec((tq, gw), lambda b, h, t: (b * nt + t, h)),
        out_shape=jax.ShapeDtypeStruct((n * T, hk * C_GROUP * LANES), BF16),
        scratch_shapes=[pltpu.VMEM((hpb, LANES, nblk), BF16), pltpu.VMEM((hpb, tq // LANES, nblk, LANES), F32),
                        pltpu.VMEM((LANES, tq), F32)],
        compiler_params=_params("parallel", "parallel", "arbitrary"),
        name="nsa_prompt",
    )(proj, gate, qn.reshape(1, LANES), kcmp, vcmp, ksb, vst, kwb, vwt)


def _cmp_pages_body(pt_ref, ck_hbm, cv_hbm, knorm_ref, pek_ref, pev_ref, w1k_ref, w2k_ref, w1v_ref, w2v_ref,
                    ko_ref, vo_ref, kbuf, vbuf, fold_s, sem, *, li, P, hk, rows_per_page):
    s = pl.program_id(0)
    ns = pl.num_programs(0)
    gpp = rows_per_page // hk // CMP_LEN

    def copies(step, slot):
        out = []
        for p in range(P):
            page = pt_ref[step * P + p]
            dst = pl.ds(p * gpp, gpp)
            out.append(pltpu.make_async_copy(ck_hbm.at[page, li], kbuf.at[slot, dst], sem.at[0, slot]))
            out.append(pltpu.make_async_copy(cv_hbm.at[page, li], vbuf.at[slot, dst], sem.at[1, slot]))
        return out

    @pl.when(s == 0)
    def _():
        for c in copies(0, 0):
            c.start()

    @pl.when(s + 1 < ns)
    def _():
        for c in copies(s + 1, (s + 1) % 2):
            c.start()

    slot = s % 2
    for c in copies(s, slot):
        c.wait()

    groups = P * (rows_per_page // hk // CMP_LEN)
    tpv = 8 // hk
    own = [lax.broadcasted_iota(jnp.int32, (groups * 8, LANES), 0) % 8 // hk == u for u in range(tpv)]

    def compress(buf, pe_ref, w1_ref, w2_ref):
        acc = jnp.zeros((groups * 8, LANES), F32)
        for jp in range(CMP_LEN // tpv):
            x = buf[slot, :, pl.ds(jp * 8, 8), :] + pe_ref[jp]
            y = _dot(x.reshape(groups * 8, LANES).astype(BF16), w1_ref[jp])
            part = y[:, 0:LANES]
            for u in range(1, tpv):
                part = jnp.where(own[u], y[:, u * LANES:(u + 1) * LANES], part)
            acc = acc + part
        tot = acc
        for u in range(1, tpv):
            tot = tot + pltpu.roll(acc, u * hk, 0)
        return _dot(_gelu(tot).astype(BF16), w2_ref[...])

    fold_s[0] = _rms(compress(kbuf, pek_ref, w1k_ref, w2k_ref), knorm_ref[0:1, :])
    fold_s[1] = compress(vbuf, pev_ref, w1v_ref, w2v_ref)
    for h in range(hk):
        rows = pl.ds(8 - hk + h, groups, stride=8)
        ko_ref[0, h] = fold_s[0, rows, :]
        vo_ref[0, h] = fold_s[1, rows, :]


def cmp_pages(page_table, cache_k, cache_v, li, knorm, pe_k, pe_v, w1k, w2k, w1v, w2v, P=16):
    nb, n_pages = page_table.shape
    n_pool, n_l, page, hk, dh = cache_k.shape
    rpp = page * hk
    gpp, grows = page // CMP_LEN, CMP_LEN * hk
    ck = cache_k.reshape(n_pool, n_l, gpp, grows, dh)
    cv = cache_v.reshape(n_pool, n_l, gpp, grows, dh)
    tpv = 8 // hk
    slab_pe = lambda pe: jnp.repeat(pe, hk, axis=0).reshape(CMP_LEN // tpv, 8, dh)
    slab_w = lambda w: w.reshape(CMP_LEN // tpv, tpv, dh, dh).transpose(0, 2, 1, 3).reshape(CMP_LEN // tpv, dh, tpv * dh)
    pe_k, pe_v, w1k, w1v = slab_pe(pe_k), slab_pe(pe_v), slab_w(w1k), slab_w(w1v)
    steps_per_b = n_pages // P
    blocks = P * gpp
    nblk = n_pages * gpp
    full = lambda a: pl.BlockSpec(a.shape, lambda s, pt: (0,) * a.ndim)
    out_spec = pl.BlockSpec((1, hk, blocks, dh), lambda s, pt: (s // steps_per_b, 0, s % steps_per_b, 0))
    grid_spec = pltpu.PrefetchScalarGridSpec(
        num_scalar_prefetch=1,
        grid=(nb * steps_per_b,),
        in_specs=[pl.BlockSpec(memory_space=pl.ANY), pl.BlockSpec(memory_space=pl.ANY),
                  full(knorm), full(pe_k), full(pe_v), full(w1k), full(w2k), full(w1v), full(w2v)],
        out_specs=[out_spec, out_spec],
        scratch_shapes=[pltpu.VMEM((2, blocks, grows, dh), F32), pltpu.VMEM((2, blocks, grows, dh), F32),
                        pltpu.VMEM((2, blocks * 8, dh), F32), pltpu.SemaphoreType.DMA((2, 2))])
    return pl.pallas_call(
        functools.partial(_cmp_pages_body, li=li, P=P, hk=hk, rows_per_page=rpp),
        grid_spec=grid_spec,
        out_shape=[jax.ShapeDtypeStruct((nb, hk, nblk, dh), F32)] * 2,
        compiler_params=_params("arbitrary"),
        name="cmp_pages",
    )(page_table.reshape(-1), ck, cv, knorm, pe_k, pe_v, w1k, w2k, w1v, w2v)


def _nsa_s_select_body(q_ref, kc_ref, vc_ref, ks_ref, qn_ref, knorm_ref, pek_ref, pev_ref,
                       w1k_ref, w2k_ref, w1v_ref, w2v_ref, kcmp_ref, vcmp_ref,
                       qo_ref, ocmp_ref, sel_ref, kso_ref, *, hk, qpos, nblk):
    G = C_GROUP
    row = slice(None)
    qw = qn_ref[...]
    nslc = (nblk + 1 + 1) // 2
    lanes_blk = lax.broadcasted_iota(jnp.int32, (8, nblk), 1)
    valid = (lanes_blk + 1) * CMP_LEN - 1 <= qpos
    valid_x = (jnp.full((8, 1), (nblk + 1) * CMP_LEN - 1, jnp.int32) <= qpos)
    pair = (lax.broadcasted_iota(jnp.int32, (nblk, nblk // 2), 0) // 2
            == lax.broadcasted_iota(jnp.int32, (nblk, nblk // 2), 1)).astype(F32)
    sub8 = lax.broadcasted_iota(jnp.int32, (8, 1), 0)
    pe_rest_k = jnp.zeros((8, LANES), F32)
    pe_rest_v = jnp.zeros((8, LANES), F32)
    for j in range(1, CMP_LEN):
        pe_rest_k = pe_rest_k + _dot(jnp.broadcast_to(pek_ref[j:j + 1, :], (8, LANES)).astype(BF16), w1k_ref[j])
        pe_rest_v = pe_rest_v + _dot(jnp.broadcast_to(pev_ref[j:j + 1, :], (8, LANES)).astype(BF16), w1v_ref[j])
    for h in range(hk):
        qs = [_rms(q_ref[row, (h * G + g) * LANES:(h * G + g + 1) * LANES], qw) * SCALE for g in range(G)]
        q8 = jnp.concatenate(qs + [jnp.zeros((8 - G, LANES), F32)], axis=0)
        qo_ref[0, h] = q8
        kso_ref[0, h] = jnp.broadcast_to(_rms(ks_ref[row, h * LANES:(h + 1) * LANES], knorm_ref[1:2, :]), (8, LANES))
        xk = jnp.broadcast_to(kc_ref[row, h * LANES:(h + 1) * LANES] + pek_ref[0:1, :], (8, LANES))
        xv = jnp.broadcast_to(vc_ref[row, h * LANES:(h + 1) * LANES] + pev_ref[0:1, :], (8, LANES))
        k_x = _dot(_gelu(_dot(xk.astype(BF16), w1k_ref[0]) + pe_rest_k).astype(BF16), w2k_ref[...])
        k_x = _rms(k_x, knorm_ref[0:1, :])
        v_x = _dot(_gelu(_dot(xv.astype(BF16), w1v_ref[0]) + pe_rest_v).astype(BF16), w2v_ref[...])
        s = jnp.where(valid, _dot_nt(q8.astype(BF16), kcmp_ref[0, h].astype(BF16)), NEG_INF)
        s_x = jnp.where(valid_x, jnp.sum(q8 * k_x, axis=1, keepdims=True), NEG_INF)
        m = jnp.maximum(jnp.max(s, axis=1, keepdims=True), s_x)
        p = jnp.where(valid, jnp.exp(s - m), 0.0)
        p_x = jnp.where(valid_x, jnp.exp(s_x - m), 0.0)
        den = jnp.maximum(jnp.sum(p, axis=1, keepdims=True) + p_x, 1.0)
        p = jnp.where(sub8 < G, p / den, 0.0)
        p_x = jnp.where(sub8 < G, p_x / den, 0.0)
        ocmp_ref[0, h] = _dot(p.astype(BF16), vcmp_ref[0, h].astype(BF16)) + p_x * v_x
        pb = jnp.sum(p, axis=0, keepdims=True)
        pb_x = jnp.sum(p_x, axis=0, keepdims=True)
        imp = _dot(jnp.broadcast_to(pb, (8, nblk)), pair, precision=HIGHEST)[0:1]
        lane = lax.broadcasted_iota(jnp.int32, (1, LANES), 1)
        tail = jnp.where(lane == 0, pb_x, -jnp.inf)
        imp = jnp.concatenate([imp, tail], axis=1)
        width = imp.shape[1]
        jb = lax.broadcasted_iota(jnp.int32, (1, width), 1)
        cur = qpos // SLC_BLOCK
        forced = (jb == 0) | (jb == cur) | (jb == cur - 1)
        score = jnp.where(jb * SLC_BLOCK <= qpos, jnp.where(forced, FORCE_SCORE, imp), -FORCE_SCORE)
        score = jnp.where(jb < nslc, score, -jnp.inf)
        sel = jnp.zeros((1, LANES), jnp.int32)
        for r in range(SLC_TOPN):
            best = jnp.max(score, axis=1, keepdims=True)
            idx = jnp.min(jnp.where(score == best, jb, width), axis=1, keepdims=True)
            sel = jnp.where(lane == r, idx, sel)
            score = jnp.where(jb == idx, -jnp.inf, score)
        sel_ref[0, h] = jnp.broadcast_to(sel, (8, LANES))


def nsa_sample_select(proj, qn, knorm, pe_k, pe_v, w1k, w2k, w1v, w2v, kcmp, vcmp, hk, qpos):
    nb = proj.shape[0]
    nblk = kcmp.shape[2]
    heads = hk * C_GROUP
    cq, ckv = heads * LANES, hk * LANES
    full = lambda a: pl.BlockSpec(a.shape, lambda b: (0,) * a.ndim)
    per = lambda *s: pl.BlockSpec((1,) + s, lambda b: (b,) + (0,) * len(s))
    colspec = lambda c0, w: pl.BlockSpec((None, 1, w), lambda b: (b, 0, c0 // w))
    proj = proj.reshape(nb, 1, -1)
    out8 = jax.ShapeDtypeStruct((nb, hk, 8, LANES), F32)
    return pl.pallas_call(
        functools.partial(_nsa_s_select_body, hk=hk, qpos=qpos, nblk=nblk),
        grid=(nb,),
        in_specs=[colspec(0, cq), colspec(cq, ckv), colspec(cq + ckv, ckv), colspec(cq + 2 * ckv, ckv),
                  pl.BlockSpec((1, LANES), lambda b: (0, 0)),
                  full(knorm), full(pe_k), full(pe_v), full(w1k), full(w2k), full(w1v), full(w2v),
                  per(hk, nblk, LANES), per(hk, nblk, LANES)],
        out_specs=[per(hk, 8, LANES)] * 4,
        out_shape=[out8, out8, jax.ShapeDtypeStruct((nb, hk, 8, LANES), jnp.int32), out8],
        compiler_params=_params("arbitrary"),
        name="nsa_sample_select",
    )(proj, proj, proj, proj, qn.reshape(1, LANES), knorm, pe_k, pe_v, w1k, w2k, w1v, w2v, kcmp, vcmp)


def _nsa_s_slc_body(sel_ref, pt_ref, q_ref, ksn_ref, vsn_ref, ck_hbm, cv_hbm, o_ref, kbuf, vbuf, sem,
                    *, hk, li, n_past_blocks, per_page, n_pages):
    s = pl.program_id(0)
    ns = pl.num_programs(0)
    rows = SLC_BLOCK * hk

    def copies(step, slot):
        out = []
        for r in range(SLC_TOPN):
            blk = jnp.minimum(sel_ref[step * SLC_TOPN + r], n_past_blocks - 1)
            page = pt_ref[(step // hk) * n_pages + blk // per_page]
            src = pl.ds(pl.multiple_of((blk % per_page) * rows, rows), rows)
            dst = pl.ds(r * rows, rows)
            out.append(pltpu.make_async_copy(ck_hbm.at[page, li, src], kbuf.at[slot, dst], sem.at[0, slot]))
            out.append(pltpu.make_async_copy(cv_hbm.at[page, li, src], vbuf.at[slot, dst], sem.at[1, slot]))
        return out

    @pl.when(s == 0)
    def _():
        for c in copies(0, 0):
            c.start()

    @pl.when(s + 1 < ns)
    def _():
        for c in copies(s + 1, (s + 1) % 2):
            c.start()

    slot = s % 2
    for c in copies(s, slot):
        c.wait()

    h = s % hk
    nkeys = SLC_TOPN * SLC_BLOCK
    first = lax.broadcasted_iota(jnp.int32, (SLC_BLOCK, LANES), 0) == 0
    lane = lax.broadcasted_iota(jnp.int32, (1, nkeys), 1)
    k_new, v_new = ksn_ref[0, 0, 0:1, :], vsn_ref[...]
    ks, vs = [], []
    okf = jnp.ones((1, nkeys), F32)
    for r in range(SLC_TOPN):
        is_new = sel_ref[s * SLC_TOPN + r] >= n_past_blocks
        rws = pl.ds(r * rows + h, SLC_BLOCK, stride=hk)
        ks.append(jnp.where(is_new, jnp.where(first, k_new, 0.0), kbuf[slot, rws, :]))
        vs.append(jnp.where(is_new, jnp.where(first, v_new, 0.0), vbuf[slot, rws, :]))
        okf = jnp.where((lane // SLC_BLOCK == r) & is_new, jnp.where(lane == r * SLC_BLOCK, 1.0, 0.0), okf)
    ok = okf > 0.5
    k = jnp.concatenate(ks, axis=0).astype(BF16)
    v = jnp.concatenate(vs, axis=0).astype(BF16)
    sc = jnp.where(ok, _dot_nt(q_ref[0, 0].astype(BF16), k), NEG_INF)
    p = jnp.where(ok, jnp.exp(sc - jnp.max(sc, axis=1, keepdims=True)), 0.0)
    o_ref[0, 0] = _dot(p.astype(BF16), v) / jnp.sum(p, axis=1, keepdims=True)


def nsa_sample_selected(sel, page_table, q8, ksn, proj, vs_col, cache_k, cache_v, li, hk):
    nb, n_pages = page_table.shape
    n_pool, n_l, page, _, dh = cache_k.shape
    rpp = page * hk
    per_page = page // SLC_BLOCK
    n_past_blocks = n_pages * per_page
    ck = cache_k.reshape(n_pool, n_l, rpp, dh)
    cv = cache_v.reshape(n_pool, n_l, rpp, dh)
    sel_flat = sel[:, :, 0, :SLC_TOPN].reshape(-1)
    proj = proj.reshape(nb, 1, -1)
    per = pl.BlockSpec((1, 1, 8, LANES), lambda s, s_, p_: (s // hk, s % hk, 0, 0))
    buf = pltpu.VMEM((2, SLC_TOPN * SLC_BLOCK * hk, dh), F32)
    grid_spec = pltpu.PrefetchScalarGridSpec(
        num_scalar_prefetch=2,
        grid=(nb * hk,),
        in_specs=[per, per,
                  pl.BlockSpec((None, 1, LANES), lambda s, s_, p_: (s // hk, 0, vs_col // LANES + s % hk)),
                  pl.BlockSpec(memory_space=pl.ANY), pl.BlockSpec(memory_space=pl.ANY)],
        out_specs=per,
        scratch_shapes=[buf, buf, pltpu.SemaphoreType.DMA((2, 2))])
    return pl.pallas_call(
        functools.partial(_nsa_s_slc_body, hk=hk, li=li, n_past_blocks=n_past_blocks, per_page=per_page,
                          n_pages=n_pages),
        grid_spec=grid_spec,
        out_shape=jax.ShapeDtypeStruct((nb, hk, 8, LANES), F32),
        compiler_params=_params("arbitrary"),
        name="nsa_sample_selected",
    )(sel_flat, page_table.reshape(-1), q8, ksn, proj, ck, cv)


def _nsa_s_win_body(q_ref, ocmp_ref, oslc_ref, kw_ref, vw_ref, gate_ref, knorm_ref, wk_ref, wv_ref,
                    o_ref, wko_ref, wvo_ref, *, hk):
    G = C_GROUP
    row = slice(None)
    eye = (lax.broadcasted_iota(jnp.int32, (LANES, LANES), 0)
           == lax.broadcasted_iota(jnp.int32, (LANES, LANES), 1)).astype(F32)
    gcol = _col(eye, _sigmoid(gate_ref[row, :]))
    nh = G * hk
    for h in range(hk):
        lanes = slice(h * LANES, (h + 1) * LANES)
        q8 = q_ref[0, h]
        kn = _rms(kw_ref[row, lanes], knorm_ref[2:3, :])
        v = vw_ref[row, lanes]
        kb, vb = wk_ref[0, 0, :, lanes], wv_ref[0, 0, :, lanes]
        s = _dot_nt(q8.astype(BF16), kb.astype(BF16))
        s_new = jnp.sum(q8 * kn, axis=1, keepdims=True)
        m = jnp.maximum(jnp.max(s, axis=1, keepdims=True), s_new)
        p = jnp.exp(s - m)
        p_new = jnp.exp(s_new - m)
        l = jnp.sum(p, axis=1, keepdims=True) + p_new
        o_win = (_dot(p.astype(BF16), vb.astype(BF16)) + p_new * v) / l
        g0 = gcol[h * G:h * G + 8]
        g1 = gcol[nh + h * G:nh + h * G + 8]
        g2 = gcol[2 * nh + h * G:2 * nh + h * G + 8]
        o = g0 * ocmp_ref[0, h] + g1 * oslc_ref[0, h] + g2 * o_win
        for g in range(G):
            o_ref[row, (h * G + g) * LANES:(h * G + g + 1) * LANES] = o[g:g + 1]
        wko_ref[0, :, lanes] = _shift_in(kb, kn)
        wvo_ref[0, :, lanes] = _shift_in(vb, v)


def nsa_sample_window(q8, ocmp, oslc, proj, kw_col, gate, knorm, win_k, win_v, li, hk):
    nb = proj.shape[0]
    Wb = win_k.shape[2]
    ckv = hk * LANES
    wk = win_k.reshape(nb, win_k.shape[1], Wb, ckv)
    wv = win_v.reshape(nb, win_v.shape[1], Wb, ckv)
    per = lambda *s: pl.BlockSpec((1,) + s, lambda b: (b,) + (0,) * len(s))
    proj = proj.reshape(nb, 1, -1)
    cw = hk * C_GROUP * LANES
    o, wko, wvo = pl.pallas_call(
        functools.partial(_nsa_s_win_body, hk=hk),
        grid=(nb,),
        in_specs=[per(hk, 8, LANES), per(hk, 8, LANES), per(hk, 8, LANES),
                  pl.BlockSpec((None, 1, ckv), lambda b: (b, 0, kw_col // ckv)),
                  pl.BlockSpec((None, 1, ckv), lambda b: (b, 0, kw_col // ckv + 1)),
                  pl.BlockSpec((None, 1, LANES), lambda b: (b, 0, 0)),
                  pl.BlockSpec(knorm.shape, lambda b: (0, 0)),
                  pl.BlockSpec((1, 1, Wb, ckv), lambda b: (b, li, 0, 0)),
                  pl.BlockSpec((1, 1, Wb, ckv), lambda b: (b, li, 0, 0))],
        out_specs=[pl.BlockSpec((None, 1, cw), lambda b: (b, 0, 0)),
                   per(Wb, ckv), per(Wb, ckv)],
        out_shape=[jax.ShapeDtypeStruct((nb, 1, cw), F32),
                   jax.ShapeDtypeStruct((nb, Wb, ckv), F32),
                   jax.ShapeDtypeStruct((nb, Wb, ckv), F32)],
        compiler_params=_params("parallel"),
        name="nsa_sample_window",
    )(q8, ocmp, oslc, proj, proj, gate.reshape(nb, 1, LANES), knorm, wk, wv)
    return o.reshape(nb, cw), wko, wvo


def _pad_cols(w, mult):
    pad = (-w.shape[-1]) % mult
    return jnp.pad(w, ((0, 0),) * (w.ndim - 1) + ((0, pad),)) if pad else w


def kernel(x_prompt, x_sample, state_hgrn, cache_dil_k, cache_dil_v, cache_cmp_k, cache_cmp_v, cache_slc_k, cache_slc_v, cache_win_k, cache_win_v, page_table, norm_mix, norm_mlp, w_in_even, w_out_even, hgrn_lb_logits, hgrn_out_norm, dil_q_norm, dil_k_norm, w_in_odd, w_out_odd, nsa_q_norm, nsa_k_norm, nsa_pe_k, nsa_pe_v, nsa_phi_k1, nsa_phi_k2, nsa_phi_v1, nsa_phi_v2, w_mlp_up, w_mlp_down):
    n, T, D = x_prompt.shape
    nb = x_sample.shape[0]
    assert x_sample.shape[1] == 1
    depth = norm_mix.shape[0]
    a_heads = hgrn_lb_logits.shape[1] // LANES
    b_heads = cache_dil_k.shape[3]
    hk = cache_win_k.shape[3]
    c_heads = hk * C_GROUP
    past_len = page_table.shape[1] * cache_cmp_k.shape[2]
    a_w = a_heads * LANES
    TN = 896
    TM = 512
    TM_IN = 1024 if (n * T) % 1024 == 0 else TM

    lb_cum = jnp.cumsum(jax.nn.softmax(hgrn_lb_logits.astype(F32), axis=0), axis=0)
    lower_bounds = lb_cum - lb_cum[0:1]

    cq, ckv = c_heads * LANES, hk * LANES
    gate_col = cq + 6 * ckv
    gate_w = w_in_odd[:, :, gate_col:].reshape(-1, D, hk, C_GROUP, 3).transpose(0, 1, 4, 2, 3).reshape(-1, D, 3 * c_heads)
    gate_wp = _pad_cols(gate_w, LANES)
    TN_ODD = 512
    w_in_even_b = w_in_even
    w_out_even_b = w_out_even.astype(BF16)
    w_out_odd_b = w_out_odd.astype(BF16)
    w_up_b = w_mlp_up.astype(BF16)
    w_down_b = w_mlp_down.astype(BF16)
    phi_k1 = nsa_phi_k1.reshape(-1, CMP_LEN, LANES, LANES).astype(BF16)
    phi_v1 = nsa_phi_v1.reshape(-1, CMP_LEN, LANES, LANES).astype(BF16)
    phi_k2 = nsa_phi_k2.astype(BF16)
    phi_v2 = nsa_phi_v2.astype(BF16)

    xp = x_prompt.reshape(n * T, D)
    xs = x_sample.reshape(nb, D)
    outs = {k: [] for k in ("hg_p", "hg_s", "dk_p", "dv_p", "dk_s", "dv_s", "ck_p", "cv_p", "sk_p", "sv_p",
                            "wk_p", "wv_p", "ck_s", "cv_s", "sk_s", "sv_s", "wk_s", "wv_s")}
    dil_bufs = None
    for layer in range(depth):
        li = layer // 2
        if layer % 2 == 0:
            lb, on = lower_bounds[li], hgrn_out_norm[li]
            qn, kn = dil_q_norm[li], dil_k_norm[li]
            pp = rms_matmul(xp, norm_mix[layer], w_in_even_b, li, TM_IN, TN)
            ps = rms_matmul(xs, norm_mix[layer], w_in_even_b, li, nb, TN)
            oa_p, st_p = hgrn_prompt(pp, lb, on, n, T, a_heads)
            ob_p, dk, dv = dil_prompt(pp, qn, kn, n, T, b_heads, 4 * a_w)
            oa_s, st_s = hgrn_sample(ps, lb, on, state_hgrn, li, a_heads)
            ob_s, *dil_bufs = dil_sample(ps, qn, kn, cache_dil_k, cache_dil_v, li, b_heads, 4 * a_w, prev=dil_bufs)
            xp = proj_residual([oa_p, ob_p], w_out_even_b, li, xp, TM)
            xs = proj_residual([oa_s, ob_s], w_out_even_b, li, xs, nb)
            outs["hg_p"].append(st_p); outs["hg_s"].append(st_s)
            outs["dk_p"].append(dk.reshape(n, -1, b_heads, LANES)); outs["dv_p"].append(dv.reshape(n, -1, b_heads, LANES))
        else:
            knorm = nsa_k_norm[li]
            cmp_w = (nsa_pe_k[li], nsa_pe_v[li], phi_k1[li], phi_k2[li], phi_v1[li], phi_v2[li])
            pp = rms_matmul(xp, norm_mix[layer], w_in_odd, li, TM_IN, TN_ODD, n_out=gate_col)
            ps = rms_matmul(xs, norm_mix[layer], w_in_odd, li, nb, TN_ODD, n_out=gate_col)
            gp = rms_matmul(xp, norm_mix[layer], gate_wp, li, TM_IN, LANES)
            gs = rms_matmul(xs, norm_mix[layer], gate_wp, li, nb, LANES)
            kcmp, vcmp, ksb, ksf, vst, kwb, kwf, vwt = nsa_prep(pp, knorm, *cmp_w, n, T, hk)
            o_p = nsa_prompt(pp, gp, nsa_q_norm[li], kcmp, vcmp, ksb, vst, kwb, vwt, n, T, hk)
            kcs, vcs = cmp_pages(page_table, cache_cmp_k, cache_cmp_v, li, knorm, *cmp_w)
            q8, ocmp, sel, ksn = nsa_sample_select(ps, nsa_q_norm[li], knorm, *cmp_w, kcs, vcs, hk, past_len)
            oslc = nsa_sample_selected(sel, page_table, q8, ksn, ps, cq + 3 * ckv, cache_slc_k, cache_slc_v, li, hk)
            o_s, wks, wvs = nsa_sample_window(q8, ocmp, oslc, ps, cq + 4 * ckv, gs, knorm,
                                              cache_win_k, cache_win_v, li, hk)
            xp = proj_residual([o_p], w_out_odd_b, li, xp, TM)
            xs = proj_residual([o_s], w_out_odd_b, li, xs, nb)
            kvp = lambda k: pp[:, cq + k * ckv:cq + (k + 1) * ckv].reshape(n, T, hk, LANES)
            kvs = lambda k: ps[:, cq + k * ckv:cq + (k + 1) * ckv].reshape(nb, 1, hk, LANES)
            ww = kwf.shape[1]
            outs["ck_p"].append(kvp(0)); outs["cv_p"].append(kvp(1))
            outs["sk_p"].append(ksf.reshape(n, T, hk, LANES)); outs["sv_p"].append(kvp(3))
            outs["wk_p"].append(kwf.reshape(n, ww, hk, LANES)); outs["wv_p"].append(kvp(5)[:, T - ww:])
            outs["ck_s"].append(kvs(0)); outs["cv_s"].append(kvs(1))
            outs["sk_s"].append(ksn[:, :, 0, :].reshape(nb, 1, hk, LANES)); outs["sv_s"].append(kvs(3))
            outs["wk_s"].append(wks.reshape(nb, -1, hk, LANES)); outs["wv_s"].append(wvs.reshape(nb, -1, hk, LANES))
        xp = mlp_residual(xp, norm_mlp[layer], w_up_b, w_down_b, layer, TM_IN, 512)
        xs = mlp_residual(xs, norm_mlp[layer], w_up_b, w_down_b, layer, nb, 512)
    st = lambda k: jnp.stack(outs[k], axis=1)
    return (xp.reshape(n, T, D), xs.reshape(nb, 1, D),
            st("hg_p"), st("hg_s"), st("dk_p"), st("dv_p"),
            dil_bufs[0].reshape(cache_dil_k.shape), dil_bufs[1].reshape(cache_dil_v.shape),
            st("ck_p"), st("cv_p"), st("sk_p"), st("sv_p"), st("wk_p"), st("wv_p"),
            st("ck_s"), st("cv_s"), st("sk_s"), st("sv_s"), st("wk_s"), st("wv_s"))
```
